```python
import jax, jax.numpy as jnp
from jax import lax
import numpy as np

D_MODEL = 1024
BATCH = 1
SEQ = 16384
DEPTH = 2

N_A = DEPTH // 2
N_B = DEPTH - N_A
HEAD_DIM = 64
MIX_HEADS = 12
MIX_WIDTH = MIX_HEADS * HEAD_DIM
MEM_HEADS = 4
MEM_WIDTH = MEM_HEADS * HEAD_DIM
MEM_TOKENS = 256
D_FF = 2752
DECAY_LORA = 64
AAA_LORA = 64
GATE_LORA = 128
SHIFT_WIDTH = 3 * MIX_WIDTH + DECAY_LORA + AAA_LORA + GATE_LORA
A_IN_WIDTH = SHIFT_WIDTH + MEM_WIDTH
B_IN_WIDTH = MIX_WIDTH + MEM_WIDTH
KV_WIDTH = 2 * MIX_WIDTH
MOBA_BLOCK = 256
MOBA_TOPK = 3
Q_CHUNK = 128
ALIBI_MAX = 8.0
NORM_EPS = 1e-6
GN_EPS = 64e-5

kernel_name = "yoco_rwkv7_moba_macaron_memory"


def rms_norm(x, g, eps=NORM_EPS):
    xf = x.astype(jnp.float32)
    y = xf * lax.rsqrt(jnp.mean(xf * xf, axis=-1, keepdims=True) + eps)
    return (y * g.astype(jnp.float32)).astype(x.dtype)


def swiglu_ffn(x, g, w1, w3, w2):
    h = rms_norm(x, g)
    return (jax.nn.silu(h @ w1) * (h @ w3)) @ w2


def rwkv7_mixer(u, mu, w0, w2, a0, a2, g2, k_k, k_a, r_k, ln_w, ln_b):
    B_, S_, _ = u.shape
    f32 = jnp.float32
    prev = jnp.pad(u[:, :-1], ((0, 0), (1, 0), (0, 0)))
    u = u + (prev - u) * mu
    cuts = [MIX_WIDTH, 2 * MIX_WIDTH, 3 * MIX_WIDTH,
            3 * MIX_WIDTH + DECAY_LORA, 3 * MIX_WIDTH + DECAY_LORA + AAA_LORA]
    r, k, v, wd, ad, gd = jnp.split(u, cuts, axis=-1)
    w_log = -jax.nn.softplus(-(w0 + jnp.tanh(wd) @ w2).astype(f32)) - 0.5
    decay = jnp.exp(-jnp.exp(w_log))
    a = jax.nn.sigmoid((a0 + ad @ a2).astype(f32))
    g = jax.nn.sigmoid(gd) @ g2

    def heads(t):
        return t.astype(f32).reshape(B_, S_, MIX_HEADS, HEAD_DIM)

    def per_head(p):
        return p.astype(f32).reshape(MIX_HEADS, HEAD_DIM)

    r, k, v, decay, a = heads(r), heads(k), heads(v), heads(decay), heads(a)
    kk = k * per_head(k_k)
    kk = kk / jnp.maximum(jnp.sqrt(jnp.sum(kk * kk, axis=-1, keepdims=True)), 1e-12)
    k = k * (1.0 + (a - 1.0) * per_head(k_a))
    b = kk * a

    def step(state, inp):
        r_t, w_t, k_t, v_t, kk_t, b_t = inp
        sa = jnp.einsum('bhvk,bhk->bhv', state, kk_t)
        state = (state * w_t[:, :, None, :] - sa[..., None] * b_t[:, :, None, :]
                 + v_t[..., None] * k_t[:, :, None, :])
        y_t = jnp.einsum('bhvk,bhk->bhv', state, r_t)
        return state, y_t

    xs = tuple(jnp.moveaxis(t, 1, 0) for t in (r, decay, k, v, kk, b))
    state0 = jnp.zeros((B_, MIX_HEADS, HEAD_DIM, HEAD_DIM), f32)
    _, y = lax.scan(step, state0, xs)
    y = jnp.moveaxis(y, 0, 1)
    mean = jnp.mean(y, axis=-1, keepdims=True)
    var = jnp.mean(jnp.square(y - mean), axis=-1, keepdims=True)
    y = ((y - mean) * lax.rsqrt(var + GN_EPS)).reshape(B_, S_, MIX_WIDTH)
    y = y * ln_w.astype(f32) + ln_b.astype(f32)
    bonus = jnp.sum(r * k * per_head(r_k), axis=-1, keepdims=True) * v
    y = (y + bonus.reshape(B_, S_, MIX_WIDTH)) * g.astype(f32)
    return y.astype(u.dtype)


def memory_attention(q, mem, mem_norm, w_mem_kv, q_norm, k_norm):
    B_, S_, _ = q.shape
    M_ = mem.shape[1]
    q = rms_norm(q.reshape(B_, S_, MEM_HEADS, HEAD_DIM), q_norm)
    kv = rms_norm(mem, mem_norm) @ w_mem_kv
    k, v = jnp.split(kv, [MEM_WIDTH], axis=-1)
    k = rms_norm(k.reshape(B_, M_, MEM_HEADS, HEAD_DIM), k_norm)
    v = v.reshape(B_, M_, MEM_HEADS, HEAD_DIM)
    s = jnp.einsum('bshd,bmhd->bhsm', q, k, preferred_element_type=jnp.float32) * (HEAD_DIM ** -0.5)
    p = jax.nn.softmax(s, axis=-1).astype(v.dtype)
    o = jnp.einsum('bhsm,bmhd->bshd', p, v)
    return o.reshape(B_, S_, MEM_WIDTH)


def shared_kv(x, kv_norm, w_kv, k_norm):
    B_, S_, _ = x.shape
    nb = -(-S_ // MOBA_BLOCK)
    pad = nb * MOBA_BLOCK - S_
    kv = rms_norm(x, kv_norm) @ w_kv
    k, v = jnp.split(kv, [MIX_WIDTH], axis=-1)
    k = rms_norm(k.reshape(B_, S_, MIX_HEADS, HEAD_DIM), k_norm)
    v = v.reshape(B_, S_, MIX_HEADS, HEAD_DIM)
    k = jnp.pad(k, ((0, 0), (0, pad), (0, 0), (0, 0)))
    v = jnp.pad(v, ((0, 0), (0, pad), (0, 0), (0, 0)))
    k_blocks = k.reshape(B_, nb, MOBA_BLOCK, MIX_HEADS, HEAD_DIM).transpose(0, 3, 1, 2, 4)
    v_blocks = v.reshape(B_, nb, MOBA_BLOCK, MIX_HEADS, HEAD_DIM).transpose(0, 3, 1, 2, 4)
    k_mean = jnp.mean(k_blocks.astype(jnp.float32), axis=3).astype(k.dtype)
    return k_blocks, v_blocks, k_mean


def moba_attention(q, k_blocks, v_blocks, k_mean, q_norm):
    B_, S_, _ = q.shape
    nb = k_mean.shape[2]
    nc = S_ // Q_CHUNK
    topk = min(MOBA_TOPK, nb)
    scale = HEAD_DIM ** -0.5
    q = rms_norm(q.reshape(B_, S_, MIX_HEADS, HEAD_DIM), q_norm)
    q_chunks = q.reshape(B_, nc, Q_CHUNK, MIX_HEADS, HEAD_DIM).transpose(1, 0, 3, 2, 4)
    slopes = jnp.exp2(-ALIBI_MAX * jnp.arange(1, MIX_HEADS + 1, dtype=jnp.float32) / MIX_HEADS)
    b_idx = jnp.arange(B_)[:, None, None, None]
    h_idx = jnp.arange(MIX_HEADS)[None, :, None, None]
    offs = jnp.arange(MOBA_BLOCK)

    def one_chunk(args):
        qc, c = args
        t = c * Q_CHUNK + jnp.arange(Q_CHUNK)
        i_blk = (c * Q_CHUNK) // MOBA_BLOCK
        gate = jnp.einsum('bhcd,bhnd->bhcn', qc, k_mean, preferred_element_type=jnp.float32)
        gate = jnp.where(jnp.arange(nb) < i_blk, gate, -jnp.inf)
        _, sel = lax.top_k(gate, topk)
        valid = sel < i_blk
        kg = k_blocks[b_idx, h_idx, sel]
        vg = v_blocks[b_idx, h_idx, sel]
        s_sel = jnp.einsum('bhcd,bhckjd->bhckj', qc, kg, preferred_element_type=jnp.float32) * scale
        dist_sel = t[None, None, :, None, None] - (sel[..., None] * MOBA_BLOCK + offs)
        s_sel = jnp.where(valid[..., None],
                          s_sel - slopes[None, :, None, None, None] * dist_sel, -jnp.inf)
        k_own = lax.dynamic_index_in_dim(k_blocks, i_blk, axis=2, keepdims=False)
        v_own = lax.dynamic_index_in_dim(v_blocks, i_blk, axis=2, keepdims=False)
        s_own = jnp.einsum('bhcd,bhjd->bhcj', qc, k_own, preferred_element_type=jnp.float32) * scale
        dist_own = t[:, None] - (i_blk * MOBA_BLOCK + offs)[None, :]
        s_own = jnp.where(dist_own[None, None] >= 0,
                          s_own - slopes[None, :, None, None] * dist_own[None, None], -jnp.inf)
        scores = jnp.concatenate(
            [s_sel.reshape(B_, MIX_HEADS, Q_CHUNK, topk * MOBA_BLOCK), s_own], axis=-1)
        p = jax.nn.softmax(scores, axis=-1).astype(v_blocks.dtype)
        p_sel = p[..., :topk * MOBA_BLOCK].reshape(B_, MIX_HEADS, Q_CHUNK, topk, MOBA_BLOCK)
        p_own = p[..., topk * MOBA_BLOCK:]
        return (jnp.einsum('bhckj,bhckjd->bhcd', p_sel, vg)
                + jnp.einsum('bhcj,bhjd->bhcd', p_own, v_own))

    out = lax.map(one_chunk, (q_chunks, jnp.arange(nc)))
    return out.transpose(1, 0, 3, 2, 4).reshape(B_, S_, MIX_WIDTH)


def setup_inputs(seed: int = 0) -> dict:
    key = jax.random.key(seed)
    keys = jax.random.split(key, 40)
    ks = iter([keys[i] for i in range(40)])
    f32 = jnp.float32

    def w(shape, fan_in, scale=1.0):
        return jax.random.normal(next(ks), shape, f32) * (scale * fan_in ** -0.5)

    def gain(shape):
        return 1.0 + 0.02 * jax.random.normal(next(ks), shape, f32)

    def small(shape, s):
        return s * jax.random.normal(next(ks), shape, f32)

    D, L = D_MODEL, DEPTH
    return {
        "x": jax.random.normal(next(ks), (BATCH, SEQ, D), f32),
        "mem": jax.random.normal(next(ks), (BATCH, MEM_TOKENS, D), f32),
        "ffn1_norm": gain((L, D)),
        "ffn1_w1": w((L, D, D_FF), D),
        "ffn1_w3": w((L, D, D_FF), D),
        "ffn1_w2": w((L, D_FF, D), D_FF),
        "mix_norm": gain((L, D)),
        "w_out": w((L, MIX_WIDTH + MEM_WIDTH, D), MIX_WIDTH + MEM_WIDTH),
        "mem_norm": gain((L, D)),
        "w_mem_kv": w((L, D, 2 * MEM_WIDTH), D),
        "mem_q_norm": gain((L, HEAD_DIM)),
        "mem_k_norm": gain((L, HEAD_DIM)),
        "ffn2_norm": gain((L, D)),
        "ffn2_w1": w((L, D, D_FF), D),
        "ffn2_w3": w((L, D, D_FF), D),
        "ffn2_w2": w((L, D_FF, D), D_FF),
        "rwkv_w_in": w((N_A, D, A_IN_WIDTH), D),
        "rwkv_mu": jax.random.uniform(next(ks), (N_A, SHIFT_WIDTH), f32),
        "rwkv_w0": jax.random.uniform(next(ks), (N_A, MIX_WIDTH), f32, -5.0, 0.0),
        "rwkv_w2": w((N_A, DECAY_LORA, MIX_WIDTH), DECAY_LORA, 0.5),
        "rwkv_a0": small((N_A, MIX_WIDTH), 0.5),
        "rwkv_a2": w((N_A, AAA_LORA, MIX_WIDTH), AAA_LORA, 0.5),
        "rwkv_g2": w((N_A, GATE_LORA, MIX_WIDTH), GATE_LORA),
        "rwkv_k_k": 0.85 + small((N_A, MIX_WIDTH), 0.05),
        "rwkv_k_a": 1.0 + small((N_A, MIX_WIDTH), 0.05),
        "rwkv_r_k": small((N_A, MIX_HEADS, HEAD_DIM), 0.1),
        "rwkv_ln_w": gain((N_A, MIX_WIDTH)),
        "rwkv_ln_b": small((N_A, MIX_WIDTH), 0.02),
        "kv_norm": gain((D,)),
        "w_kv": w((D, KV_WIDTH), D),
        "kv_k_norm": gain((HEAD_DIM,)),
        "moba_w_in": w((N_B, D, B_IN_WIDTH), D),
        "moba_q_norm": gain((N_B, HEAD_DIM)),
    }


def reference(x, mem, ffn1_norm, ffn1_w1, ffn1_w3, ffn1_w2, mix_norm, w_out,
              mem_norm, w_mem_kv, mem_q_norm, mem_k_norm,
              ffn2_norm, ffn2_w1, ffn2_w3, ffn2_w2,
              rwkv_w_in, rwkv_mu, rwkv_w0, rwkv_w2, rwkv_a0, rwkv_a2, rwkv_g2,
              rwkv_k_k, rwkv_k_a, rwkv_r_k, rwkv_ln_w, rwkv_ln_b,
              kv_norm, w_kv, kv_k_norm, moba_w_in, moba_q_norm):
    shared = None
    for l in range(DEPTH):
        x = x + 0.5 * swiglu_ffn(x, ffn1_norm[l], ffn1_w1[l], ffn1_w3[l], ffn1_w2[l])
        h = rms_norm(x, mix_norm[l])
        if l < N_A:
            proj = h @ rwkv_w_in[l]
            mix = rwkv7_mixer(proj[..., :SHIFT_WIDTH], rwkv_mu[l], rwkv_w0[l], rwkv_w2[l],
                              rwkv_a0[l], rwkv_a2[l], rwkv_g2[l], rwkv_k_k[l], rwkv_k_a[l],
                              rwkv_r_k[l], rwkv_ln_w[l], rwkv_ln_b[l])
            mem_q = proj[..., SHIFT_WIDTH:]
        else:
            j = l - N_A
            proj = h @ moba_w_in[j]
            k_blocks, v_blocks, k_mean = shared
            mix = moba_attention(proj[..., :MIX_WIDTH], k_blocks, v_blocks, k_mean, moba_q_norm[j])
            mem_q = proj[..., MIX_WIDTH:]
        mem_out = memory_attention(mem_q, mem, mem_norm[l], w_mem_kv[l], mem_q_norm[l], mem_k_norm[l])
        x = x + jnp.concatenate([mix, mem_out], axis=-1) @ w_out[l]
        x = x + 0.5 * swiglu_ffn(x, ffn2_norm[l], ffn2_w1[l], ffn2_w3[l], ffn2_w2[l])
        if l == N_A - 1:
            shared = shared_kv(x, kv_norm, w_kv, kv_k_norm)
    return x
```

```python
import functools

import jax
import jax.numpy as jnp
from jax import lax
from jax.experimental import pallas as pl
from jax.experimental.pallas import tpu as pltpu

F32 = jnp.float32
BF16 = jnp.bfloat16
HI = lax.Precision.HIGHEST

HEAD = 64
LANES = 128
MXU = 256
CHUNK = 64
MOBA_BLOCK = 256
MOBA_TOPK = 3
ALIBI_MAX = 8.0
NORM_EPS = 1e-6
GN_EPS = 64e-5
VMEM_LIMIT = 56 * 1024 * 1024
NEG_INF = float("-inf")


def _cparams(*sem):
    return pltpu.CompilerParams(dimension_semantics=sem, vmem_limit_bytes=VMEM_LIMIT)


def _const_spec(shape):
    return pl.BlockSpec(shape, lambda *_: (0,) * len(shape), pipeline_mode=pl.Buffered(1))


def _rms(x, g):
    return x * lax.rsqrt(jnp.mean(x * x, axis=-1, keepdims=True) + NORM_EPS) * g


def _dot(a, b, precision=None):
    return jnp.dot(a, b, preferred_element_type=F32, precision=precision)


def _dot_nt(a, b, precision=None):
    return lax.dot_general(a, b, (((1,), (1,)), ((), ())), preferred_element_type=F32, precision=precision)


def _dot_tn(a, b, precision=None):
    return lax.dot_general(a, b, (((0,), (0,)), ((), ())), preferred_element_type=F32, precision=precision)


def _head_sum_matrix(scale=1.0):
    i = lax.broadcasted_iota(jnp.int32, (LANES, LANES), 0) // HEAD
    j = lax.broadcasted_iota(jnp.int32, (LANES, LANES), 1) // HEAD
    return jnp.where(i == j, F32(scale), F32(0.0))


def _head_rms(x, g):
    ms = _dot(x * x, _head_sum_matrix(1.0 / HEAD), HI)
    return x * lax.rsqrt(ms + NORM_EPS) * g


def _ffn_kernel(x_ref, g_ref, w1_ref, w3_ref, w2_ref, o_ref, acc_ref, *, f_chunk):
    x = x_ref[...]
    h = _rms(x, g_ref[...]).astype(BF16)
    n_chunks = w1_ref.shape[1] // f_chunk
    for c in range(n_chunks):
        sl = slice(c * f_chunk, (c + 1) * f_chunk)
        a = _dot(h, w1_ref[:, sl])
        b = _dot(h, w3_ref[:, sl])
        act = (a * jax.nn.sigmoid(a) * b).astype(BF16)
        part = _dot(act, w2_ref[sl, :])
        if c == 0:
            acc_ref[...] = part
        else:
            acc_ref[...] += part
    o_ref[...] = x + 0.5 * acc_ref[...]


def _ffn(x, g, w1, w3, w2, *, tm=512, f_chunk=MXU):
    s, d = x.shape
    f = w1.shape[1]
    fp = -(-f // f_chunk) * f_chunk
    w1 = jnp.pad(w1.astype(BF16), ((0, 0), (0, fp - f)))
    w3 = jnp.pad(w3.astype(BF16), ((0, 0), (0, fp - f)))
    w2 = jnp.pad(w2.astype(BF16), ((0, fp - f), (0, 0)))
    tm = min(tm, s)
    return pl.pallas_call(
        functools.partial(_ffn_kernel, f_chunk=f_chunk),
        grid=(s // tm,),
        in_specs=[pl.BlockSpec((tm, d), lambda i: (i, 0)),
                  _const_spec((1, d)), _const_spec((d, fp)), _const_spec((d, fp)), _const_spec((fp, d))],
        out_specs=pl.BlockSpec((tm, d), lambda i: (i, 0)),
        out_shape=jax.ShapeDtypeStruct((s, d), F32),
        scratch_shapes=[pltpu.VMEM((tm, d), F32)],
        compiler_params=_cparams("parallel"),
        name="ffn",
    )(x, g.reshape(1, d), w1, w3, w2)


def _norm_proj_kernel(x_ref, g_ref, w_ref, o_ref):
    h = _rms(x_ref[...], g_ref[...]).astype(BF16)
    o_ref[...] = _dot(h, w_ref[...])


def _norm_proj(x, g, w, *, tm=512):
    s, d = x.shape
    n = w.shape[1]
    tm = min(tm, s)
    return pl.pallas_call(
        _norm_proj_kernel,
        grid=(s // tm,),
        in_specs=[pl.BlockSpec((tm, d), lambda i: (i, 0)), _const_spec((1, d)), _const_spec((d, n))],
        out_specs=pl.BlockSpec((tm, n), lambda i: (i, 0)),
        out_shape=jax.ShapeDtypeStruct((s, n), F32),
        compiler_params=_cparams("parallel"),
        name="norm_proj",
    )(x, g.reshape(1, d), w.astype(BF16))


def _out_proj_kernel(x_ref, mix_ref, mem_ref, wa_ref, wb_ref, o_ref):
    o_ref[...] = (x_ref[...] + _dot(mix_ref[...].astype(BF16), wa_ref[...])
                  + _dot(mem_ref[...].astype(BF16), wb_ref[...]))


def _out_proj(x, mix, mem_out, w_out, *, tm=512):
    s, d = x.shape
    na, nb = mix.shape[1], mem_out.shape[1]
    tm = min(tm, s)
    w = w_out.astype(BF16)
    return pl.pallas_call(
        _out_proj_kernel,
        grid=(s // tm,),
        in_specs=[pl.BlockSpec((tm, d), lambda i: (i, 0)),
                  pl.BlockSpec((tm, na), lambda i: (i, 0)),
                  pl.BlockSpec((tm, nb), lambda i: (i, 0)),
                  _const_spec((na, d)), _const_spec((nb, d))],
        out_specs=pl.BlockSpec((tm, d), lambda i: (i, 0)),
        out_shape=jax.ShapeDtypeStruct((s, d), F32),
        compiler_params=_cparams("parallel"),
        name="out_proj",
    )(x, mix, mem_out, w[:na], w[na:])


def _mem_kv_kernel(mem_ref, g_ref, w_ref, kn_ref, k_ref, v_ref):
    h = _rms(mem_ref[...], g_ref[...]).astype(BF16)
    kv = _dot(h, w_ref[...])
    width = k_ref.shape[1]
    for p in range(width // LANES):
        sl = slice(p * LANES, (p + 1) * LANES)
        k_ref[:, sl] = _head_rms(kv[:, sl], kn_ref[...])
    v_ref[...] = kv[:, width:]


def _mem_kv(mem, g, w, k_norm):
    m, d = mem.shape
    width = w.shape[1] // 2
    kn = jnp.tile(k_norm, LANES // HEAD).reshape(1, LANES)
    return pl.pallas_call(
        _mem_kv_kernel,
        grid=(1,),
        in_specs=[_const_spec((m, d)), _const_spec((1, d)), _const_spec((d, 2 * width)), _const_spec((1, LANES))],
        out_specs=[pl.BlockSpec((m, width), lambda i: (0, 0)), pl.BlockSpec((m, width), lambda i: (0, 0))],
        out_shape=[jax.ShapeDtypeStruct((m, width), F32)] * 2,
        compiler_params=_cparams("arbitrary"),
        name="mem_kv",
    )(mem, g.reshape(1, d), w.astype(BF16), kn)


def _mem_attn_kernel(q_ref, qn_ref, k_ref, v_ref, o_ref):
    lane = lax.broadcasted_iota(jnp.int32, (1, LANES), 1)
    width = q_ref.shape[1]
    for p in range(width // LANES):
        sl = slice(p * LANES, (p + 1) * LANES)
        q = _head_rms(q_ref[:, sl], qn_ref[...]) * (HEAD ** -0.5)
        k = k_ref[:, sl].astype(BF16)
        v = v_ref[:, sl].astype(BF16)
        out = None
        for h in range(LANES // HEAD):
            in_head = (lane // HEAD) == h
            qh = jnp.where(in_head, q, 0.0).astype(BF16)
            s = _dot_nt(qh, k)
            s = s - jnp.max(s, axis=-1, keepdims=True)
            e = jnp.exp(s)
            pr = e / jnp.sum(e, axis=-1, keepdims=True)
            oh = _dot(pr.astype(BF16), v)
            out = oh if out is None else jnp.where(in_head, oh, out)
        o_ref[:, sl] = out


def _mem_attn(proj, col_block, q_norm, k, v, *, tm=512):
    s = proj.shape[0]
    m, width = k.shape
    tm = min(tm, s)
    qn = jnp.tile(q_norm, LANES // HEAD).reshape(1, LANES)
    return pl.pallas_call(
        _mem_attn_kernel,
        grid=(s // tm,),
        in_specs=[pl.BlockSpec((tm, width), lambda i: (i, col_block)),
                  _const_spec((1, LANES)), _const_spec((m, width)), _const_spec((m, width))],
        out_specs=pl.BlockSpec((tm, width), lambda i: (i, 0)),
        out_shape=jax.ShapeDtypeStruct((s, width), F32),
        compiler_params=_cparams("parallel"),
        name="mem_attn",
    )(proj, qn, k, v)


def _stack_heads(z):
    lane = lax.broadcasted_iota(jnp.int32, z.shape, 1)
    return jnp.concatenate([jnp.where(lane < HEAD, z, 0.0), jnp.where(lane >= HEAD, z, 0.0)], axis=0)


def _rwkv_kernel(proj_ref, mu_ref, wl_ref, g2_ref, w0_ref, a0_ref, kk_ref, ka_ref, rk_ref, lnw_ref, lnb_ref,
                 o_ref, last_ref, h_ref, r_s, k_s, v_s, a_s, lw_s, cum_s, g_s, y_s):
    c = CHUNK
    n_pairs = r_s.shape[0]
    width = n_pairs * LANES

    @pl.when(pl.program_id(0) == 0)
    def _():
        last_ref[...] = jnp.zeros_like(last_ref)
        h_ref[...] = jnp.zeros_like(h_ref)

    u = proj_ref[...]
    row = lax.broadcasted_iota(jnp.int32, u.shape, 0)
    prev = jnp.where(row == 0, last_ref[...], pltpu.roll(u, 1, 0))
    last_ref[...] = u[c - 1:c, :]
    u = u + (prev - u) * mu_ref[...]

    lora_in = u[:, 3 * width:3 * width + LANES]
    lane = lax.broadcasted_iota(jnp.int32, lora_in.shape, 1)
    lora_in = jnp.where(lane < HEAD, jnp.tanh(lora_in), lora_in)
    lo = _dot(lora_in, wl_ref[...], HI)
    z = -(w0_ref[...] + lo[:, :width])
    softplus = jnp.maximum(z, 0.0) + jnp.log(1.0 + jnp.exp(-jnp.abs(z)))
    lw = -jnp.exp(-softplus - 0.5)
    a = jax.nn.sigmoid(a0_ref[...] + lo[:, width:])
    g = _dot(jax.nn.sigmoid(u[:, 3 * width + LANES:]), g2_ref[...], HI)
    ti = lax.broadcasted_iota(jnp.int32, (c, c), 0)
    si = lax.broadcasted_iota(jnp.int32, (c, c), 1)
    cum = _dot(jnp.where(si <= ti, F32(1.0), F32(0.0)), lw, HI)

    for p in range(n_pairs):
        sl = slice(p * LANES, (p + 1) * LANES)
        r_s[p] = u[:, sl]
        k_s[p] = u[:, width + p * LANES:width + (p + 1) * LANES]
        v_s[p] = u[:, 2 * width + p * LANES:2 * width + (p + 1) * LANES]
        a_s[p] = a[:, sl]
        lw_s[p] = lw[:, sl]
        cum_s[p] = cum[:, sl]
        g_s[p] = g[:, sl]

    hsum = _head_sum_matrix()
    col = lax.broadcasted_iota(jnp.int32, (c, LANES), 1) % c
    trow = lax.broadcasted_iota(jnp.int32, (c, LANES), 0)
    strict = col < trow
    incl = col <= trow
    bi = lax.broadcasted_iota(jnp.int32, (LANES, LANES), 0)
    bj = lax.broadcasted_iota(jnp.int32, (LANES, LANES), 1)
    eye = bi == bj
    same_head = (bi // HEAD) == (bj // HEAD)

    def pair_body(p, carry):
        r, k, v, a_, lw_, cum_ = r_s[p], k_s[p], v_s[p], a_s[p], lw_s[p], cum_s[p]
        kk = k * kk_ref[p]
        kk = kk / jnp.maximum(jnp.sqrt(_dot(kk * kk, hsum, HI)), 1e-12)
        kmod = k * (1.0 + (a_ - 1.0) * ka_ref[p])
        b = kk * a_
        bonus = _dot(r * kmod * rk_ref[p], hsum, HI) * v

        tot = cum_[c - 1:c, :]
        at = -kk * jnp.exp(cum_ - lw_)
        rt = r * jnp.exp(cum_)
        einv = jnp.exp(-cum_)
        eend = jnp.exp(tot - cum_)
        bh, kh = b * einv, kmod * einv
        bt, kt = b * eend, kmod * eend

        aa = _dot_nt(jnp.concatenate([at, rt], axis=0),
                     jnp.concatenate([_stack_heads(bh), _stack_heads(kh)], axis=0), HI)
        a_ab = jnp.where(strict, aa[:c, :LANES], 0.0)
        a_ak = jnp.where(strict, aa[:c, LANES:], 0.0)
        a_rb = jnp.where(incl, aa[c:, :LANES], 0.0)
        a_rk = jnp.where(incl, aa[c:, LANES:], 0.0)

        n_blk = _stack_heads(a_ab)
        t_inv = jnp.where(eye, 1.0, 0.0) + jnp.where((bi == bj + 1) & (bi % 2 == 1), n_blk, 0.0)
        size = 2
        while size < c:
            e_mask = ((bi // (2 * size)) == (bj // (2 * size))) & (bi % (2 * size) >= size) & (bj % (2 * size) < size)
            e = jnp.where(e_mask, n_blk, 0.0)
            t_inv = t_inv + _dot(_dot(t_inv, e, HI), t_inv, HI)
            size *= 2
        t_pair = t_inv[:c] + t_inv[c:]

        v_st = _stack_heads(v)
        g0 = _dot(a_ak, v_st, HI)
        uw = _dot(t_pair, jnp.concatenate([_stack_heads(g0), _stack_heads(at)], axis=1), HI)
        u0, w = uw[:, :LANES], uw[:, LANES:]
        rw = rt + _dot(a_rb, _stack_heads(w), HI)
        y0 = _dot(a_rb, _stack_heads(u0), HI) + _dot(a_rk, v_st, HI)
        m1 = jnp.where(eye, jnp.exp(tot), 0.0) + jnp.where(same_head, _dot_tn(bt, w, HI), 0.0)
        m0 = jnp.where(same_head, _dot_tn(bt, u0, HI) + _dot_tn(kt, v, HI), 0.0)

        h0 = h_ref[p]
        y = _dot(rw, h0, HI) + y0
        h_ref[p] = _dot(m1, h0, HI) + m0

        mean = _dot(y, hsum, HI) * (1.0 / HEAD)
        yc = y - mean
        var = _dot(yc * yc, hsum, HI) * (1.0 / HEAD)
        yn = yc * lax.rsqrt(var + GN_EPS) * lnw_ref[p] + lnb_ref[p]
        y_s[p] = (yn + bonus) * g_s[p]
        return carry

    lax.fori_loop(0, n_pairs, pair_body, 0)
    for p in range(n_pairs):
        o_ref[:, p * LANES:(p + 1) * LANES] = y_s[p]


def _rwkv(proj, mu, w0, w2, a0, a2, g2, k_k, k_a, r_k, ln_w, ln_b):
    s = proj.shape[0]
    width = w0.shape[0]
    n_pairs = width // LANES
    shift_w = mu.shape[0]
    dl, al = w2.shape[0], a2.shape[0]
    assert dl + al == LANES and g2.shape[0] == LANES and shift_w == 3 * width + 2 * LANES
    wl = jnp.zeros((LANES, 2 * width), F32).at[:dl, :width].set(w2).at[dl:, width:].set(a2)

    def per_pair(t):
        return t.reshape(n_pairs, 1, LANES)

    c = CHUNK
    pair_spec = _const_spec((n_pairs, 1, LANES))
    return pl.pallas_call(
        _rwkv_kernel,
        grid=(s // c,),
        in_specs=[pl.BlockSpec((c, shift_w), lambda i: (i, 0)),
                  _const_spec((1, shift_w)), _const_spec((LANES, 2 * width)), _const_spec((LANES, width)),
                  _const_spec((1, width)), _const_spec((1, width)),
                  pair_spec, pair_spec, pair_spec, pair_spec, pair_spec],
        out_specs=pl.BlockSpec((c, width), lambda i: (i, 0)),
        out_shape=jax.ShapeDtypeStruct((s, width), F32),
        scratch_shapes=[pltpu.VMEM((1, shift_w), F32), pltpu.VMEM((n_pairs, LANES, LANES), F32)]
                       + [pltpu.VMEM((n_pairs, c, LANES), F32)] * 8,
        compiler_params=_cparams("arbitrary"),
        name="rwkv7",
    )(proj, mu.reshape(1, shift_w), wl, g2, w0.reshape(1, width), a0.reshape(1, width),
      per_pair(k_k), per_pair(k_a), per_pair(r_k.reshape(-1)), per_pair(ln_w), per_pair(ln_b))


def _shared_kv_kernel(x_ref, g_ref, wk_ref, wvt_ref, kn_ref, k_ref, vt_ref, km_ref):
    h = _rms(x_ref[...], g_ref[...]).astype(BF16)
    k = _dot(h, wk_ref[...])
    n_pairs = k_ref.shape[0]
    for p in range(n_pairs):
        kp = _head_rms(k[:, p * LANES:(p + 1) * LANES], kn_ref[...])
        k_ref[p, 0] = kp.astype(BF16)
        km_ref[0, :, p * LANES:(p + 1) * LANES] = jnp.mean(kp, axis=0, keepdims=True)
    vt = _dot_nt(wvt_ref[...], h)
    vt_ref[:, 0] = vt.reshape(n_pairs, LANES, vt.shape[1]).astype(BF16)


def _shared_kv(x, g, w_kv, k_norm):
    s, d = x.shape
    width = w_kv.shape[1] // 2
    n_pairs = width // LANES
    blk = MOBA_BLOCK
    nb = s // blk
    w = w_kv.astype(BF16)
    kn = jnp.tile(k_norm, LANES // HEAD).reshape(1, LANES)
    k, vt, km = pl.pallas_call(
        _shared_kv_kernel,
        grid=(nb,),
        in_specs=[pl.BlockSpec((blk, d), lambda i: (i, 0)), _const_spec((1, d)),
                  _const_spec((d, width)), _const_spec((width, d)), _const_spec((1, LANES))],
        out_specs=[pl.BlockSpec((n_pairs, 1, blk, LANES), lambda i: (0, i, 0, 0)),
                   pl.BlockSpec((n_pairs, 1, LANES, blk), lambda i: (0, i, 0, 0)),
                   pl.BlockSpec((1, 1, width), lambda i: (i, 0, 0))],
        out_shape=[jax.ShapeDtypeStruct((n_pairs, nb, blk, LANES), BF16),
                   jax.ShapeDtypeStruct((n_pairs, nb, LANES, blk), BF16),
                   jax.ShapeDtypeStruct((nb, 1, width), F32)],
        compiler_params=_cparams("parallel"),
        name="shared_kv",
    )(x, g.reshape(1, d), w[:, :width], w[:, width:].T, kn)
    return k, vt, km.reshape(nb, width)


def _moba_kernel(slopes_ref, q_ref, qn_ref, km_ref, k_ref, vt_ref, o_ref, sel_ref):
    p = pl.program_id(0)
    i = pl.program_id(1)
    blk = MOBA_BLOCK
    nb = km_ref.shape[0]
    q = _head_rms(q_ref[...], qn_ref[...]) * (HEAD ** -0.5)
    km = km_ref[...]
    lane = lax.broadcasted_iota(jnp.int32, (1, LANES), 1)
    n_idx = lax.broadcasted_iota(jnp.int32, (nb, blk), 0)
    key_pos = lax.broadcasted_iota(jnp.int32, (blk, blk), 0)
    qry_pos = lax.broadcasted_iota(jnp.int32, (blk, blk), 1)
    dist = (qry_pos - key_pos).astype(F32)
    out_row = lax.broadcasted_iota(jnp.int32, (LANES, blk), 0)
    out_t = jnp.zeros((LANES, blk), F32)

    for h in range(LANES // HEAD):
        slope = slopes_ref[2 * p + h]
        qh = jnp.where((lane // HEAD) == h, q, 0.0)
        qb = qh.astype(BF16)

        gate = jnp.where(n_idx < i, _dot_nt(km, qh, HI), NEG_INF)
        sel_bias = jnp.full(gate.shape, NEG_INF, F32)
        for _ in range(MOBA_TOPK):
            mx = jnp.max(gate, axis=0, keepdims=True)
            first = jnp.min(jnp.where(gate == mx, n_idx, nb), axis=0, keepdims=True)
            pick = (n_idx == first) & (mx > NEG_INF)
            sel_bias = jnp.where(pick, 0.0, sel_bias)
            gate = jnp.where(pick, NEG_INF, gate)
        sel_ref[...] = sel_bias

        s = _dot_nt(k_ref[i], qb)
        s = jnp.where(dist >= 0.0, s - slope * dist, NEG_INF)
        m = jnp.max(s, axis=0, keepdims=True)
        e = jnp.exp(s - m)
        l = jnp.sum(e, axis=0, keepdims=True)
        acc = _dot(vt_ref[i], e.astype(BF16))
        bias = -slope * dist

        def past_block(n, carry):
            m, l, acc = carry
            rowb = sel_ref[pl.ds(n, 1), :] - slope * ((i - n) * blk).astype(F32)
            s = _dot_nt(k_ref[n], qb) + bias + rowb
            m_new = jnp.maximum(m, jnp.max(s, axis=0, keepdims=True))
            alpha = jnp.exp(m - m_new)
            e = jnp.exp(s - m_new)
            l = alpha * l + jnp.sum(e, axis=0, keepdims=True)
            acc = alpha * acc + _dot(vt_ref[n], e.astype(BF16))
            return m_new, l, acc

        m, l, acc = lax.fori_loop(0, i, past_block, (m, l, acc))
        out_t = jnp.where((out_row // HEAD) == h, acc / l, out_t)

    o_ref[...] = out_t.T


def _moba(proj, q_norm, k, vt, km):
    s = proj.shape[0]
    n_pairs, nb, blk, _ = k.shape
    heads = n_pairs * (LANES // HEAD)
    slopes = jnp.exp2(-ALIBI_MAX * jnp.arange(1, heads + 1, dtype=F32) / heads)
    qn = jnp.tile(q_norm, LANES // HEAD).reshape(1, LANES)
    return pl.pallas_call(
        _moba_kernel,
        grid=(n_pairs, nb),
        in_specs=[pl.BlockSpec(memory_space=pltpu.SMEM),
                  pl.BlockSpec((blk, LANES), lambda p, i: (i, p)),
                  pl.BlockSpec((1, LANES), lambda p, i: (0, 0)),
                  pl.BlockSpec((nb, LANES), lambda p, i: (0, p)),
                  pl.BlockSpec((None, nb, blk, LANES), lambda p, i: (p, 0, 0, 0)),
                  pl.BlockSpec((None, nb, LANES, blk), lambda p, i: (p, 0, 0, 0))],
        out_specs=pl.BlockSpec((blk, LANES), lambda p, i: (i, p)),
        out_shape=jax.ShapeDtypeStruct((s, n_pairs * LANES), F32),
        scratch_shapes=[pltpu.VMEM((nb, blk), F32)],
        compiler_params=_cparams("parallel", "arbitrary"),
        name="moba",
    )(slopes, proj, qn, km, k, vt)


def kernel(x, mem, ffn1_norm, ffn1_w1, ffn1_w3, ffn1_w2, mix_norm, w_out, mem_norm, w_mem_kv, mem_q_norm, mem_k_norm, ffn2_norm, ffn2_w1, ffn2_w3, ffn2_w2, rwkv_w_in, rwkv_mu, rwkv_w0, rwkv_w2, rwkv_a0, rwkv_a2, rwkv_g2, rwkv_k_k, rwkv_k_a, rwkv_r_k, rwkv_ln_w, rwkv_ln_b, kv_norm, w_kv, kv_k_norm, moba_w_in, moba_q_norm):
    batch = x.shape[0]
    depth = ffn1_norm.shape[0]
    n_a = rwkv_w_in.shape[0]
    mix_w = rwkv_w0.shape[1]
    mem_w = w_mem_kv.shape[2] // 2
    outs = []
    for bi in range(batch):
        xb, memb = x[bi], mem[bi]
        shared = None
        for l in range(depth):
            xb = _ffn(xb, ffn1_norm[l], ffn1_w1[l], ffn1_w3[l], ffn1_w2[l])
            if l < n_a:
                proj = _norm_proj(xb, mix_norm[l], rwkv_w_in[l])
                mix = _rwkv(proj, rwkv_mu[l], rwkv_w0[l], rwkv_w2[l], rwkv_a0[l], rwkv_a2[l], rwkv_g2[l],
                            rwkv_k_k[l], rwkv_k_a[l], rwkv_r_k[l], rwkv_ln_w[l], rwkv_ln_b[l])
                mem_col = rwkv_mu.shape[1] // mem_w
            else:
                j = l - n_a
                proj = _norm_proj(xb, mix_norm[l], moba_w_in[j])
                mix = _moba(proj, moba_q_norm[j], *shared)
                mem_col = mix_w // mem_w
            mk, mv = _mem_kv(memb, mem_norm[l], w_mem_kv[l], mem_k_norm[l])
            mem_out = _mem_attn(proj, mem_col, mem_q_norm[l], mk, mv)
            xb = _out_proj(xb, mix, mem_out, w_out[l])
            xb = _ffn(xb, ffn2_norm[l], ffn2_w1[l], ffn2_w3[l], ffn2_w2[l])
            if l == n_a - 1:
                shared = _shared_kv(xb, kv_norm, w_kv, kv_k_norm)
        outs.append(xb)
    return jnp.stack(outs)
```

```python
import functools

import jax
import jax.numpy as jnp
from jax import lax
from jax.experimental import pallas as pl
from jax.experimental.pallas import tpu as pltpu

F32 = jnp.float32
BF16 = jnp.bfloat16
HI = lax.Precision.HIGHEST

HEAD = 64
LANES = 128
MXU = 256
CHUNK = 64
MOBA_BLOCK = 256
MOBA_TOPK = 3
ALIBI_MAX = 8.0
NORM_EPS = 1e-6
GN_EPS = 64e-5
VMEM_LIMIT = 56 * 1024 * 1024
NEG_INF = float("-inf")


def _cparams(*sem):
    return pltpu.CompilerParams(dimension_semantics=sem, vmem_limit_bytes=VMEM_LIMIT)


def _const_spec(shape):
    return pl.BlockSpec(shape, lambda *_: (0,) * len(shape), pipeline_mode=pl.Buffered(1))


def _rms(x, g):
    return x * lax.rsqrt(jnp.mean(x * x, axis=-1, keepdims=True) + NORM_EPS) * g


def _b16(x):
    return x.astype(BF16)


def _dot(a, b, precision=None):
    return jnp.dot(a, b, preferred_element_type=F32, precision=precision)


def _dot_nt(a, b, precision=None):
    return lax.dot_general(a, b, (((1,), (1,)), ((), ())), preferred_element_type=F32, precision=precision)


def _dot_tn(a, b, precision=None):
    return lax.dot_general(a, b, (((0,), (0,)), ((), ())), preferred_element_type=F32, precision=precision)


def _head_sum_matrix(scale=1.0):
    i = lax.broadcasted_iota(jnp.int32, (LANES, LANES), 0) // HEAD
    j = lax.broadcasted_iota(jnp.int32, (LANES, LANES), 1) // HEAD
    return jnp.where(i == j, F32(scale), F32(0.0))


def _head_rms(x, g):
    ms = _dot(x * x, _head_sum_matrix(1.0 / HEAD), HI)
    return x * lax.rsqrt(ms + NORM_EPS) * g


def _ffn_kernel(x_ref, g_ref, w1_ref, w3_ref, w2_ref, o_ref, acc_ref, *, f_chunk):
    x = x_ref[...]
    h = _rms(x, g_ref[...]).astype(BF16)
    n_chunks = w1_ref.shape[1] // f_chunk
    for c in range(n_chunks):
        sl = slice(c * f_chunk, (c + 1) * f_chunk)
        a = _dot(h, w1_ref[:, sl])
        b = _dot(h, w3_ref[:, sl])
        act = (a * jax.nn.sigmoid(a) * b).astype(BF16)
        part = _dot(act, w2_ref[sl, :])
        if c == 0:
            acc_ref[...] = part
        else:
            acc_ref[...] += part
    o_ref[...] = x + 0.5 * acc_ref[...]


def _ffn(x, g, w1, w3, w2, *, tm=512, f_chunk=MXU):
    s, d = x.shape
    f = w1.shape[1]
    fp = -(-f // f_chunk) * f_chunk
    w1 = jnp.pad(w1.astype(BF16), ((0, 0), (0, fp - f)))
    w3 = jnp.pad(w3.astype(BF16), ((0, 0), (0, fp - f)))
    w2 = jnp.pad(w2.astype(BF16), ((0, fp - f), (0, 0)))
    tm = min(tm, s)
    return pl.pallas_call(
        functools.partial(_ffn_kernel, f_chunk=f_chunk),
        grid=(s // tm,),
        in_specs=[pl.BlockSpec((tm, d), lambda i: (i, 0)),
                  _const_spec((1, d)), _const_spec((d, fp)), _const_spec((d, fp)), _const_spec((fp, d))],
        out_specs=pl.BlockSpec((tm, d), lambda i: (i, 0)),
        out_shape=jax.ShapeDtypeStruct((s, d), F32),
        scratch_shapes=[pltpu.VMEM((tm, d), F32)],
        compiler_params=_cparams("parallel"),
        name="ffn",
    )(x, g.reshape(1, d), w1, w3, w2)


def _norm_proj_kernel(x_ref, g_ref, w_ref, o_ref):
    h = _rms(x_ref[...], g_ref[...]).astype(BF16)
    o_ref[...] = _dot(h, w_ref[...])


def _norm_proj(x, g, w, *, tm=512):
    s, d = x.shape
    n = w.shape[1]
    tm = min(tm, s)
    return pl.pallas_call(
        _norm_proj_kernel,
        grid=(s // tm,),
        in_specs=[pl.BlockSpec((tm, d), lambda i: (i, 0)), _const_spec((1, d)), _const_spec((d, n))],
        out_specs=pl.BlockSpec((tm, n), lambda i: (i, 0)),
        out_shape=jax.ShapeDtypeStruct((s, n), F32),
        compiler_params=_cparams("parallel"),
        name="norm_proj",
    )(x, g.reshape(1, d), w.astype(BF16))


def _out_proj_kernel(x_ref, mix_ref, mem_ref, wa_ref, wb_ref, o_ref):
    o_ref[...] = (x_ref[...] + _dot(mix_ref[...].astype(BF16), wa_ref[...])
                  + _dot(mem_ref[...].astype(BF16), wb_ref[...]))


def _out_proj(x, mix, mem_out, w_out, *, tm=512):
    s, d = x.shape
    na, nb = mix.shape[1], mem_out.shape[1]
    tm = min(tm, s)
    w = w_out.astype(BF16)
    return pl.pallas_call(
        _out_proj_kernel,
        grid=(s // tm,),
        in_specs=[pl.BlockSpec((tm, d), lambda i: (i, 0)),
                  pl.BlockSpec((tm, na), lambda i: (i, 0)),
                  pl.BlockSpec((tm, nb), lambda i: (i, 0)),
                  _const_spec((na, d)), _const_spec((nb, d))],
        out_specs=pl.BlockSpec((tm, d), lambda i: (i, 0)),
        out_shape=jax.ShapeDtypeStruct((s, d), F32),
        compiler_params=_cparams("parallel"),
        name="out_proj",
    )(x, mix, mem_out, w[:na], w[na:])


def _mem_kv_kernel(mem_ref, g_ref, w_ref, kn_ref, k_ref, v_ref):
    h = _rms(mem_ref[...], g_ref[...]).astype(BF16)
    kv = _dot(h, w_ref[...])
    width = k_ref.shape[1]
    for p in range(width // LANES):
        sl = slice(p * LANES, (p + 1) * LANES)
        k_ref[:, sl] = _head_rms(kv[:, sl], kn_ref[...])
    v_ref[...] = kv[:, width:]


def _mem_kv(mem, g, w, k_norm):
    m, d = mem.shape
    width = w.shape[1] // 2
    kn = jnp.tile(k_norm, LANES // HEAD).reshape(1, LANES)
    return pl.pallas_call(
        _mem_kv_kernel,
        grid=(1,),
        in_specs=[_const_spec((m, d)), _const_spec((1, d)), _const_spec((d, 2 * width)), _const_spec((1, LANES))],
        out_specs=[pl.BlockSpec((m, width), lambda i: (0, 0)), pl.BlockSpec((m, width), lambda i: (0, 0))],
        out_shape=[jax.ShapeDtypeStruct((m, width), F32)] * 2,
        compiler_params=_cparams("arbitrary"),
        name="mem_kv",
    )(mem, g.reshape(1, d), w.astype(BF16), kn)


def _mem_attn_kernel(q_ref, qn_ref, k_ref, v_ref, o_ref):
    lane = lax.broadcasted_iota(jnp.int32, (1, LANES), 1)
    width = q_ref.shape[1]
    for p in range(width // LANES):
        sl = slice(p * LANES, (p + 1) * LANES)
        q = _head_rms(q_ref[:, sl], qn_ref[...]) * (HEAD ** -0.5)
        k = k_ref[:, sl].astype(BF16)
        v = v_ref[:, sl].astype(BF16)
        out = None
        for h in range(LANES // HEAD):
            in_head = (lane // HEAD) == h
            qh = jnp.where(in_head, q, 0.0).astype(BF16)
            s = _dot_nt(qh, k)
            s = s - jnp.max(s, axis=-1, keepdims=True)
            e = jnp.exp(s)
            pr = e / jnp.sum(e, axis=-1, keepdims=True)
            oh = _dot(pr.astype(BF16), v)
            out = oh if out is None else jnp.where(in_head, oh, out)
        o_ref[:, sl] = out


def _mem_attn(proj, col_block, q_norm, k, v, *, tm=512):
    s = proj.shape[0]
    m, width = k.shape
    tm = min(tm, s)
    qn = jnp.tile(q_norm, LANES // HEAD).reshape(1, LANES)
    return pl.pallas_call(
        _mem_attn_kernel,
        grid=(s // tm,),
        in_specs=[pl.BlockSpec((tm, width), lambda i: (i, col_block)),
                  _const_spec((1, LANES)), _const_spec((m, width)), _const_spec((m, width))],
        out_specs=pl.BlockSpec((tm, width), lambda i: (i, 0)),
        out_shape=jax.ShapeDtypeStruct((s, width), F32),
        compiler_params=_cparams("parallel"),
        name="mem_attn",
    )(proj, qn, k, v)


def _stack_heads(z):
    lane = lax.broadcasted_iota(jnp.int32, z.shape, 1)
    return jnp.concatenate([jnp.where(lane < HEAD, z, 0.0), jnp.where(lane >= HEAD, z, 0.0)], axis=0)


def _rwkv_kernel(proj_ref, mu_ref, wl_ref, g2_ref, w0_ref, a0_ref, kk_ref, ka_ref, rk_ref, lnw_ref, lnb_ref,
                 o_ref, last_ref, h_ref, r_s, k_s, v_s, a_s, lw_s, cum_s, g_s, y_s):
    c = CHUNK
    n_pairs = r_s.shape[0]
    width = n_pairs * LANES

    @pl.when(pl.program_id(0) == 0)
    def _():
        last_ref[...] = jnp.zeros_like(last_ref)
        h_ref[...] = jnp.zeros_like(h_ref)

    u = proj_ref[...]
    row = lax.broadcasted_iota(jnp.int32, u.shape, 0)
    prev = jnp.where(row == 0, last_ref[...], pltpu.roll(u, 1, 0))
    last_ref[...] = u[c - 1:c, :]
    u = u + (prev - u) * mu_ref[...]

    lora_in = u[:, 3 * width:3 * width + LANES]
    lane = lax.broadcasted_iota(jnp.int32, lora_in.shape, 1)
    lora_in = jnp.where(lane < HEAD, jnp.tanh(lora_in), lora_in)
    lo = _dot(lora_in, wl_ref[...], HI)
    z = -(w0_ref[...] + lo[:, :width])
    softplus = jnp.maximum(z, 0.0) + jnp.log(1.0 + jnp.exp(-jnp.abs(z)))
    lw = -jnp.exp(-softplus - 0.5)
    a = jax.nn.sigmoid(a0_ref[...] + lo[:, width:])
    g = _dot(jax.nn.sigmoid(u[:, 3 * width + LANES:]), g2_ref[...], HI)
    ti = lax.broadcasted_iota(jnp.int32, (c, c), 0)
    si = lax.broadcasted_iota(jnp.int32, (c, c), 1)
    cum = _dot(jnp.where(si <= ti, F32(1.0), F32(0.0)), lw, HI)

    for p in range(n_pairs):
        sl = slice(p * LANES, (p + 1) * LANES)
        r_s[p] = u[:, sl]
        k_s[p] = u[:, width + p * LANES:width + (p + 1) * LANES]
        v_s[p] = u[:, 2 * width + p * LANES:2 * width + (p + 1) * LANES]
        a_s[p] = a[:, sl]
        lw_s[p] = lw[:, sl]
        cum_s[p] = cum[:, sl]
        g_s[p] = g[:, sl]

    hsum = _b16(_head_sum_matrix())
    col = lax.broadcasted_iota(jnp.int32, (c, LANES), 1) % c
    trow = lax.broadcasted_iota(jnp.int32, (c, LANES), 0)
    strict = col < trow
    incl = col <= trow
    bi = lax.broadcasted_iota(jnp.int32, (LANES, LANES), 0)
    bj = lax.broadcasted_iota(jnp.int32, (LANES, LANES), 1)
    eye = bi == bj
    same_head = (bi // HEAD) == (bj // HEAD)

    def pair_body(p, carry):
        r, k, v, a_, lw_, cum_ = r_s[p], k_s[p], v_s[p], a_s[p], lw_s[p], cum_s[p]
        kk = k * kk_ref[p]
        kk = kk / jnp.maximum(jnp.sqrt(_dot(_b16(kk * kk), hsum)), 1e-12)
        kmod = k * (1.0 + (a_ - 1.0) * ka_ref[p])
        b = kk * a_
        bonus = _dot(_b16(r * kmod * rk_ref[p]), hsum) * v

        tot = cum_[c - 1:c, :]
        at = -kk * jnp.exp(cum_ - lw_)
        rt = r * jnp.exp(cum_)
        einv = jnp.exp(-cum_)
        eend = jnp.exp(tot - cum_)
        bh, kh = _b16(b * einv), _b16(kmod * einv)
        bt, kt = _b16(b * eend), _b16(kmod * eend)
        at_b = _b16(at)

        aa = _dot_nt(jnp.concatenate([at_b, _b16(rt)], axis=0),
                     jnp.concatenate([_stack_heads(bh), _stack_heads(kh)], axis=0))
        a_ab = jnp.where(strict, aa[:c, :LANES], 0.0)
        a_ak = _b16(jnp.where(strict, aa[:c, LANES:], 0.0))
        a_rb = _b16(jnp.where(incl, aa[c:, :LANES], 0.0))
        a_rk = _b16(jnp.where(incl, aa[c:, LANES:], 0.0))

        n_blk = _stack_heads(a_ab)
        t_inv = jnp.where(eye, 1.0, 0.0) + jnp.where((bi == bj + 1) & (bi % 2 == 1), n_blk, 0.0)
        size = 2
        while size < c:
            e_mask = ((bi // (2 * size)) == (bj // (2 * size))) & (bi % (2 * size) >= size) & (bj % (2 * size) < size)
            t_b = _b16(t_inv)
            te = _dot(t_b, _b16(jnp.where(e_mask, n_blk, 0.0)))
            t_inv = t_inv + _dot(_b16(te), t_b)
            size *= 2
        t_pair = _b16(t_inv[:c] + t_inv[c:])

        v_b = _b16(v)
        v_st = _stack_heads(v_b)
        g0 = _dot(a_ak, v_st)
        uw = _dot(t_pair, jnp.concatenate([_stack_heads(_b16(g0)), _stack_heads(at_b)], axis=1))
        u0, w = _b16(uw[:, :LANES]), _b16(uw[:, LANES:])
        rw = rt + _dot(a_rb, _stack_heads(w))
        y0 = _dot(a_rb, _stack_heads(u0)) + _dot(a_rk, v_st)
        m1 = jnp.where(eye, jnp.exp(tot), 0.0) + jnp.where(same_head, _dot_tn(bt, w), 0.0)
        m0 = jnp.where(same_head, _dot_tn(bt, u0) + _dot_tn(kt, v_b), 0.0)

        h0 = _b16(h_ref[p])
        y = _dot(_b16(rw), h0) + y0
        h_ref[p] = _dot(_b16(m1), h0) + m0

        mean = _dot(_b16(y), hsum) * (1.0 / HEAD)
        yc = y - mean
        var = _dot(_b16(yc * yc), hsum) * (1.0 / HEAD)
        yn = yc * lax.rsqrt(var + GN_EPS) * lnw_ref[p] + lnb_ref[p]
        y_s[p] = (yn + bonus) * g_s[p]
        return carry

    lax.fori_loop(0, n_pairs, pair_body, 0, unroll=True)
    for p in range(n_pairs):
        o_ref[:, p * LANES:(p + 1) * LANES] = y_s[p]


def _rwkv(proj, mu, w0, w2, a0, a2, g2, k_k, k_a, r_k, ln_w, ln_b):
    s = proj.shape[0]
    width = w0.shape[0]
    n_pairs = width // LANES
    shift_w = mu.shape[0]
    dl, al = w2.shape[0], a2.shape[0]
    assert dl + al == LANES and g2.shape[0] == LANES and shift_w == 3 * width + 2 * LANES
    wl = jnp.zeros((LANES, 2 * width), F32).at[:dl, :width].set(w2).at[dl:, width:].set(a2)

    def per_pair(t):
        return t.reshape(n_pairs, 1, LANES)

    c = CHUNK
    pair_spec = _const_spec((n_pairs, 1, LANES))
    return pl.pallas_call(
        _rwkv_kernel,
        grid=(s // c,),
        in_specs=[pl.BlockSpec((c, shift_w), lambda i: (i, 0)),
                  _const_spec((1, shift_w)), _const_spec((LANES, 2 * width)), _const_spec((LANES, width)),
                  _const_spec((1, width)), _const_spec((1, width)),
                  pair_spec, pair_spec, pair_spec, pair_spec, pair_spec],
        out_specs=pl.BlockSpec((c, width), lambda i: (i, 0)),
        out_shape=jax.ShapeDtypeStruct((s, width), F32),
        scratch_shapes=[pltpu.VMEM((1, shift_w), F32), pltpu.VMEM((n_pairs, LANES, LANES), F32)]
                       + [pltpu.VMEM((n_pairs, c, LANES), F32)] * 8,
        compiler_params=_cparams("arbitrary"),
        name="rwkv7",
    )(proj, mu.reshape(1, shift_w), wl, g2, w0.reshape(1, width), a0.reshape(1, width),
      per_pair(k_k), per_pair(k_a), per_pair(r_k.reshape(-1)), per_pair(ln_w), per_pair(ln_b))


def _shared_kv_kernel(x_ref, g_ref, wk_ref, wvt_ref, kn_ref, k_ref, vt_ref, km_ref):
    h = _rms(x_ref[...], g_ref[...]).astype(BF16)
    k = _dot(h, wk_ref[...])
    n_pairs = k_ref.shape[0]
    for p in range(n_pairs):
        kp = _head_rms(k[:, p * LANES:(p + 1) * LANES], kn_ref[...])
        k_ref[p, 0] = kp.astype(BF16)
        km_ref[0, :, p * LANES:(p + 1) * LANES] = jnp.mean(kp, axis=0, keepdims=True)
    vt = _dot_nt(wvt_ref[...], h)
    vt_ref[:, 0] = vt.reshape(n_pairs, LANES, vt.shape[1]).astype(BF16)


def _shared_kv(x, g, w_kv, k_norm):
    s, d = x.shape
    width = w_kv.shape[1] // 2
    n_pairs = width // LANES
    blk = MOBA_BLOCK
    nb = s // blk
    w = w_kv.astype(BF16)
    kn = jnp.tile(k_norm, LANES // HEAD).reshape(1, LANES)
    k, vt, km = pl.pallas_call(
        _shared_kv_kernel,
        grid=(nb,),
        in_specs=[pl.BlockSpec((blk, d), lambda i: (i, 0)), _const_spec((1, d)),
                  _const_spec((d, width)), _const_spec((width, d)), _const_spec((1, LANES))],
        out_specs=[pl.BlockSpec((n_pairs, 1, blk, LANES), lambda i: (0, i, 0, 0)),
                   pl.BlockSpec((n_pairs, 1, LANES, blk), lambda i: (0, i, 0, 0)),
                   pl.BlockSpec((1, 1, width), lambda i: (i, 0, 0))],
        out_shape=[jax.ShapeDtypeStruct((n_pairs, nb, blk, LANES), BF16),
                   jax.ShapeDtypeStruct((n_pairs, nb, LANES, blk), BF16),
                   jax.ShapeDtypeStruct((nb, 1, width), F32)],
        compiler_params=_cparams("parallel"),
        name="shared_kv",
    )(x, g.reshape(1, d), w[:, :width], w[:, width:].T, kn)
    return k, vt, km.reshape(nb, width)


def _moba_kernel(slopes_ref, q_ref, qn_ref, km_ref, k_ref, vt_ref, o_ref, sel_ref, bias_ref):
    p = pl.program_id(0)
    i = pl.program_id(1)
    blk = MOBA_BLOCK
    nb = km_ref.shape[0]
    q = _head_rms(q_ref[...], qn_ref[...]) * (HEAD ** -0.5)
    lane = lax.broadcasted_iota(jnp.int32, (1, LANES), 1)
    q2 = jnp.concatenate([jnp.where(lane < HEAD, q, 0.0), jnp.where(lane >= HEAD, q, 0.0)], axis=0)
    qb = q2.astype(BF16)
    col = lax.broadcasted_iota(jnp.int32, (1, 2 * blk), 1)
    slope = jnp.where(col < blk, slopes_ref[2 * p], slopes_ref[2 * p + 1])

    n_idx = lax.broadcasted_iota(jnp.int32, (nb, 2 * blk), 0)
    gate = jnp.where(n_idx < i, _dot_nt(km_ref[...], q2, HI), NEG_INF)
    sel_bias = jnp.full(gate.shape, NEG_INF, F32)
    for _ in range(MOBA_TOPK):
        mx = jnp.max(gate, axis=0, keepdims=True)
        first = jnp.min(jnp.where(gate == mx, n_idx, nb), axis=0, keepdims=True)
        pick = (n_idx == first) & (mx > NEG_INF)
        sel_bias = jnp.where(pick, 0.0, sel_bias)
        gate = jnp.where(pick, NEG_INF, gate)
    sel_ref[...] = sel_bias - slope * ((i - n_idx) * blk).astype(F32)

    key_pos = lax.broadcasted_iota(jnp.int32, (blk, 2 * blk), 0)
    qry_pos = lax.broadcasted_iota(jnp.int32, (blk, 2 * blk), 1) % blk
    dist = (qry_pos - key_pos).astype(F32)
    bias_ref[...] = -slope * dist

    def weighted_values(n, e):
        eb = e.astype(BF16)
        return jnp.concatenate([_dot(vt_ref[n, :HEAD, :], eb[:, :blk]), _dot(vt_ref[n, HEAD:, :], eb[:, blk:])],
                               axis=1)

    s = jnp.where(dist >= 0.0, _dot_nt(k_ref[i], qb) + bias_ref[...], NEG_INF)
    m = jnp.max(s, axis=0, keepdims=True)
    e = jnp.exp(s - m)
    l = jnp.sum(e, axis=0, keepdims=True)
    acc = weighted_values(i, e)

    def two_past_blocks(t, carry):
        m, l, acc = carry
        n0 = 2 * t
        n1 = n0 + 1
        s0 = _dot_nt(k_ref[n0], qb) + bias_ref[...] + sel_ref[pl.ds(n0, 1), :]
        s1 = _dot_nt(k_ref[n1], qb) + bias_ref[...] + sel_ref[pl.ds(n1, 1), :]
        m_new = jnp.maximum(m, jnp.max(jnp.maximum(s0, s1), axis=0, keepdims=True))
        alpha = jnp.exp(m - m_new)
        e0 = jnp.exp(s0 - m_new)
        e1 = jnp.exp(s1 - m_new)
        l = alpha * l + jnp.sum(e0 + e1, axis=0, keepdims=True)
        acc = alpha * acc + (weighted_values(n0, e0) + weighted_values(n1, e1))
        return m_new, l, acc

    m, l, acc = lax.fori_loop(0, (i + 1) // 2, two_past_blocks, (m, l, acc))
    out = acc / l
    o_ref[...] = jnp.concatenate([out[:, :blk], out[:, blk:]], axis=0).T


def _moba(proj, q_norm, k, vt, km):
    s = proj.shape[0]
    n_pairs, nb, blk, _ = k.shape
    heads = n_pairs * (LANES // HEAD)
    slopes = jnp.exp2(-ALIBI_MAX * jnp.arange(1, heads + 1, dtype=F32) / heads)
    qn = jnp.tile(q_norm, LANES // HEAD).reshape(1, LANES)
    return pl.pallas_call(
        _moba_kernel,
        grid=(n_pairs, nb),
        in_specs=[pl.BlockSpec(memory_space=pltpu.SMEM),
                  pl.BlockSpec((blk, LANES), lambda p, i: (i, p)),
                  pl.BlockSpec((1, LANES), lambda p, i: (0, 0)),
                  pl.BlockSpec((nb, LANES), lambda p, i: (0, p)),
                  pl.BlockSpec((None, nb, blk, LANES), lambda p, i: (p, 0, 0, 0)),
                  pl.BlockSpec((None, nb, LANES, blk), lambda p, i: (p, 0, 0, 0))],
        out_specs=pl.BlockSpec((blk, LANES), lambda p, i: (i, p)),
        out_shape=jax.ShapeDtypeStruct((s, n_pairs * LANES), F32),
        scratch_shapes=[pltpu.VMEM((nb, 2 * blk), F32), pltpu.VMEM((blk, 2 * blk), F32)],
        compiler_params=_cparams("parallel", "arbitrary"),
        name="moba",
    )(slopes, proj, qn, km, k, vt)


def kernel(x, mem, ffn1_norm, ffn1_w1, ffn1_w3, ffn1_w2, mix_norm, w_out, mem_norm, w_mem_kv, mem_q_norm, mem_k_norm, ffn2_norm, ffn2_w1, ffn2_w3, ffn2_w2, rwkv_w_in, rwkv_mu, rwkv_w0, rwkv_w2, rwkv_a0, rwkv_a2, rwkv_g2, rwkv_k_k, rwkv_k_a, rwkv_r_k, rwkv_ln_w, rwkv_ln_b, kv_norm, w_kv, kv_k_norm, moba_w_in, moba_q_norm):
    batch = x.shape[0]
    depth = ffn1_norm.shape[0]
    n_a = rwkv_w_in.shape[0]
    mix_w = rwkv_w0.shape[1]
    mem_w = w_mem_kv.shape[2] // 2
    outs = []
    for bi in range(batch):
        xb, memb = x[bi], mem[bi]
        shared = None
        for l in range(depth):
            xb = _ffn(xb, ffn1_norm[l], ffn1_w1[l], ffn1_w3[l], ffn1_w2[l])
            if l < n_a:
                proj = _norm_proj(xb, mix_norm[l], rwkv_w_in[l])
                mix = _rwkv(proj, rwkv_mu[l], rwkv_w0[l], rwkv_w2[l], rwkv_a0[l], rwkv_a2[l], rwkv_g2[l],
                            rwkv_k_k[l], rwkv_k_a[l], rwkv_r_k[l], rwkv_ln_w[l], rwkv_ln_b[l])
                mem_col = rwkv_mu.shape[1] // mem_w
            else:
                j = l - n_a
                proj = _norm_proj(xb, mix_norm[l], moba_w_in[j])
                mix = _moba(proj, moba_q_norm[j], *shared)
                mem_col = mix_w // mem_w
            mk, mv = _mem_kv(memb, mem_norm[l], w_mem_kv[l], mem_k_norm[l])
            mem_out = _mem_attn(proj, mem_col, mem_q_norm[l], mk, mv)
            xb = _out_proj(xb, mix, mem_out, w_out[l])
            xb = _ffn(xb, ffn2_norm[l], ffn2_w1[l], ffn2_w3[l], ffn2_w2[l])
            if l == n_a - 1:
                shared = _shared_kv(xb, kv_norm, w_kv, kv_k_norm)
        outs.append(xb)
    return jnp.stack(outs)
```

```python
import functools

import jax
import jax.numpy as jnp
from jax import lax
from jax.experimental import pallas as pl
from jax.experimental.pallas import tpu as pltpu

F32 = jnp.float32
BF16 = jnp.bfloat16
HI = lax.Precision.HIGHEST

HEAD = 64
LANES = 128
MXU = 256
CHUNK = 64
MOBA_BLOCK = 256
MOBA_TOPK = 3
ALIBI_MAX = 8.0
NORM_EPS = 1e-6
GN_EPS = 64e-5
VMEM_LIMIT = 56 * 1024 * 1024
NEG_INF = float("-inf")
LOG2E = 1.4426950408889634


def _cparams(*sem):
    return pltpu.CompilerParams(dimension_semantics=sem, vmem_limit_bytes=VMEM_LIMIT)


def _const_spec(shape):
    return pl.BlockSpec(shape, lambda *_: (0,) * len(shape), pipeline_mode=pl.Buffered(1))


def _rms(x, g):
    return x * lax.rsqrt(jnp.mean(x * x, axis=-1, keepdims=True) + NORM_EPS) * g


def _b16(x):
    return x.astype(BF16)


def _split_bf16(x):
    hi = x.astype(BF16)
    return hi, (x - hi.astype(F32)).astype(BF16)


def _dot(a, b, precision=None):
    return jnp.dot(a, b, preferred_element_type=F32, precision=precision)


def _dot_nt(a, b, precision=None):
    return lax.dot_general(a, b, (((1,), (1,)), ((), ())), preferred_element_type=F32, precision=precision)


def _dot_tn(a, b, precision=None):
    return lax.dot_general(a, b, (((0,), (0,)), ((), ())), preferred_element_type=F32, precision=precision)


def _head_sum_matrix(scale=1.0):
    i = lax.broadcasted_iota(jnp.int32, (LANES, LANES), 0) // HEAD
    j = lax.broadcasted_iota(jnp.int32, (LANES, LANES), 1) // HEAD
    return jnp.where(i == j, F32(scale), F32(0.0))


def _head_rms(x, g):
    ms = _dot(x * x, _head_sum_matrix(1.0 / HEAD), HI)
    return x * lax.rsqrt(ms + NORM_EPS) * g


def _ffn_kernel(x_ref, g_ref, w1_ref, w3_ref, w2_ref, o_ref, acc_ref, *, f_chunk):
    x = x_ref[...]
    h = _rms(x, g_ref[...]).astype(BF16)
    n_chunks = w1_ref.shape[1] // f_chunk
    for c in range(n_chunks):
        sl = slice(c * f_chunk, (c + 1) * f_chunk)
        a = _dot(h, w1_ref[:, sl])
        b = _dot(h, w3_ref[:, sl])
        act = (a * jax.nn.sigmoid(a) * b).astype(BF16)
        part = _dot(act, w2_ref[sl, :])
        if c == 0:
            acc_ref[...] = part
        else:
            acc_ref[...] += part
    o_ref[...] = x + 0.5 * acc_ref[...]


def _ffn(x, g, w1, w3, w2, *, tm=512, f_chunk=MXU):
    s, d = x.shape
    f = w1.shape[1]
    fp = -(-f // f_chunk) * f_chunk
    w1 = jnp.pad(w1.astype(BF16), ((0, 0), (0, fp - f)))
    w3 = jnp.pad(w3.astype(BF16), ((0, 0), (0, fp - f)))
    w2 = jnp.pad(w2.astype(BF16), ((0, fp - f), (0, 0)))
    tm = min(tm, s)
    return pl.pallas_call(
        functools.partial(_ffn_kernel, f_chunk=f_chunk),
        grid=(s // tm,),
        in_specs=[pl.BlockSpec((tm, d), lambda i: (i, 0)),
                  _const_spec((1, d)), _const_spec((d, fp)), _const_spec((d, fp)), _const_spec((fp, d))],
        out_specs=pl.BlockSpec((tm, d), lambda i: (i, 0)),
        out_shape=jax.ShapeDtypeStruct((s, d), F32),
        scratch_shapes=[pltpu.VMEM((tm, d), F32)],
        compiler_params=_cparams("parallel"),
        name="ffn",
    )(x, g.reshape(1, d), w1, w3, w2)


def _norm_proj_kernel(x_ref, g_ref, w_ref, o_ref):
    h = _rms(x_ref[...], g_ref[...]).astype(BF16)
    o_ref[...] = _dot(h, w_ref[...])


def _norm_proj(x, g, w, *, tm=512):
    s, d = x.shape
    n = w.shape[1]
    tm = min(tm, s)
    return pl.pallas_call(
        _norm_proj_kernel,
        grid=(s // tm,),
        in_specs=[pl.BlockSpec((tm, d), lambda i: (i, 0)), _const_spec((1, d)), _const_spec((d, n))],
        out_specs=pl.BlockSpec((tm, n), lambda i: (i, 0)),
        out_shape=jax.ShapeDtypeStruct((s, n), F32),
        compiler_params=_cparams("parallel"),
        name="norm_proj",
    )(x, g.reshape(1, d), w.astype(BF16))


def _out_proj_kernel(x_ref, mix_ref, mem_ref, wa_ref, wb_ref, o_ref):
    o_ref[...] = (x_ref[...] + _dot(mix_ref[...].astype(BF16), wa_ref[...])
                  + _dot(mem_ref[...].astype(BF16), wb_ref[...]))


def _out_proj(x, mix, mem_out, w_out, *, tm=512):
    s, d = x.shape
    na, nb = mix.shape[1], mem_out.shape[1]
    tm = min(tm, s)
    w = w_out.astype(BF16)
    return pl.pallas_call(
        _out_proj_kernel,
        grid=(s // tm,),
        in_specs=[pl.BlockSpec((tm, d), lambda i: (i, 0)),
                  pl.BlockSpec((tm, na), lambda i: (i, 0)),
                  pl.BlockSpec((tm, nb), lambda i: (i, 0)),
                  _const_spec((na, d)), _const_spec((nb, d))],
        out_specs=pl.BlockSpec((tm, d), lambda i: (i, 0)),
        out_shape=jax.ShapeDtypeStruct((s, d), F32),
        compiler_params=_cparams("parallel"),
        name="out_proj",
    )(x, mix, mem_out, w[:na], w[na:])


def _mem_kv_kernel(mem_ref, g_ref, w_ref, kn_ref, k_ref, v_ref):
    h = _rms(mem_ref[...], g_ref[...]).astype(BF16)
    kv = _dot(h, w_ref[...])
    width = k_ref.shape[1]
    for p in range(width // LANES):
        sl = slice(p * LANES, (p + 1) * LANES)
        k_ref[:, sl] = _head_rms(kv[:, sl], kn_ref[...])
    v_ref[...] = kv[:, width:]


def _mem_kv(mem, g, w, k_norm):
    m, d = mem.shape
    width = w.shape[1] // 2
    kn = jnp.tile(k_norm, LANES // HEAD).reshape(1, LANES)
    return pl.pallas_call(
        _mem_kv_kernel,
        grid=(1,),
        in_specs=[_const_spec((m, d)), _const_spec((1, d)), _const_spec((d, 2 * width)), _const_spec((1, LANES))],
        out_specs=[pl.BlockSpec((m, width), lambda i: (0, 0)), pl.BlockSpec((m, width), lambda i: (0, 0))],
        out_shape=[jax.ShapeDtypeStruct((m, width), F32)] * 2,
        compiler_params=_cparams("arbitrary"),
        name="mem_kv",
    )(mem, g.reshape(1, d), w.astype(BF16), kn)


def _mem_attn_kernel(q_ref, qn_ref, k_ref, v_ref, o_ref):
    lane = lax.broadcasted_iota(jnp.int32, (1, LANES), 1)
    width = q_ref.shape[1]
    for p in range(width // LANES):
        sl = slice(p * LANES, (p + 1) * LANES)
        q = _head_rms(q_ref[:, sl], qn_ref[...]) * (HEAD ** -0.5)
        k = k_ref[:, sl].astype(BF16)
        v = v_ref[:, sl].astype(BF16)
        out = None
        for h in range(LANES // HEAD):
            in_head = (lane // HEAD) == h
            qh = jnp.where(in_head, q, 0.0).astype(BF16)
            s = _dot_nt(qh, k)
            s = s - jnp.max(s, axis=-1, keepdims=True)
            e = jnp.exp(s)
            pr = e / jnp.sum(e, axis=-1, keepdims=True)
            oh = _dot(pr.astype(BF16), v)
            out = oh if out is None else jnp.where(in_head, oh, out)
        o_ref[:, sl] = out


def _mem_attn(proj, col_block, q_norm, k, v, *, tm=512):
    s = proj.shape[0]
    m, width = k.shape
    tm = min(tm, s)
    qn = jnp.tile(q_norm, LANES // HEAD).reshape(1, LANES)
    return pl.pallas_call(
        _mem_attn_kernel,
        grid=(s // tm,),
        in_specs=[pl.BlockSpec((tm, width), lambda i: (i, col_block)),
                  _const_spec((1, LANES)), _const_spec((m, width)), _const_spec((m, width))],
        out_specs=pl.BlockSpec((tm, width), lambda i: (i, 0)),
        out_shape=jax.ShapeDtypeStruct((s, width), F32),
        compiler_params=_cparams("parallel"),
        name="mem_attn",
    )(proj, qn, k, v)


def _stack_heads(z):
    lane = lax.broadcasted_iota(jnp.int32, z.shape, 2)
    zero = jnp.zeros_like(z)
    return jnp.concatenate([jnp.where(lane < HEAD, z, zero), jnp.where(lane >= HEAD, z, zero)], axis=1)


def _bdot(a, b):
    return lax.dot_general(a, b, (((2,), (1,)), ((0,), (0,))), preferred_element_type=F32)


def _bdot_nt(a, b):
    return lax.dot_general(a, b, (((2,), (2,)), ((0,), (0,))), preferred_element_type=F32)


def _bdot_tn(a, b):
    return lax.dot_general(a, b, (((1,), (1,)), ((0,), (0,))), preferred_element_type=F32)


def _rwkv_kernel(proj_ref, mu_ref, wlh_ref, wll_ref, g2_ref, w0_ref, a0_ref, kk_ref, ka_ref, rk_ref, lnw_ref, lnb_ref,
                 o_ref, last_ref, h_ref, r_s, k_s, v_s, a_s, lw_s, cum_s, g_s):
    c = CHUNK
    n_pairs = r_s.shape[0]
    width = n_pairs * LANES

    @pl.when(pl.program_id(0) == 0)
    def _():
        last_ref[...] = jnp.zeros_like(last_ref)
        h_ref[...] = jnp.zeros_like(h_ref)

    u = proj_ref[...]
    row = lax.broadcasted_iota(jnp.int32, u.shape, 0)
    prev = jnp.where(row == 0, last_ref[...], pltpu.roll(u, 1, 0))
    last_ref[...] = u[c - 1:c, :]
    u = u + (prev - u) * mu_ref[...]

    lora_in = u[:, 3 * width:3 * width + LANES]
    lane = lax.broadcasted_iota(jnp.int32, lora_in.shape, 1)
    lora_in = jnp.where(lane < HEAD, jnp.tanh(lora_in), lora_in)
    x_hi, x_lo = _split_bf16(lora_in)
    lo = _dot(x_hi, wlh_ref[...]) + (_dot(x_lo, wlh_ref[...]) + _dot(x_hi, wll_ref[...]))
    z = -(w0_ref[...] + lo[:, :width])
    softplus = jnp.maximum(z, 0.0) + jnp.log(1.0 + jnp.exp(-jnp.abs(z)))
    lw = -jnp.exp(-softplus - 0.5)
    a = jax.nn.sigmoid(a0_ref[...] + lo[:, width:])
    g = _dot(_b16(jax.nn.sigmoid(u[:, 3 * width + LANES:])), g2_ref[...])
    ti = lax.broadcasted_iota(jnp.int32, (c, c), 0)
    si = lax.broadcasted_iota(jnp.int32, (c, c), 1)
    tri = _b16(jnp.where(si <= ti, F32(1.0), F32(0.0)))
    lw_hi, lw_lo = _split_bf16(lw)
    cum = _dot(tri, lw_hi) + _dot(tri, lw_lo)

    for p in range(n_pairs):
        sl = slice(p * LANES, (p + 1) * LANES)
        r_s[p] = u[:, sl]
        k_s[p] = u[:, width + p * LANES:width + (p + 1) * LANES]
        v_s[p] = u[:, 2 * width + p * LANES:2 * width + (p + 1) * LANES]
        a_s[p] = a[:, sl]
        lw_s[p] = lw[:, sl]
        cum_s[p] = cum[:, sl]
        g_s[p] = g[:, sl]

    np_ = n_pairs
    hsum = _b16(_head_sum_matrix())

    def head_sum(z):
        return _dot(_b16(z).reshape(np_ * c, LANES), hsum).reshape(np_, c, LANES)

    col = lax.broadcasted_iota(jnp.int32, (np_, c, LANES), 2) % c
    trow = lax.broadcasted_iota(jnp.int32, (np_, c, LANES), 1)
    strict = col < trow
    incl = col <= trow
    bi = lax.broadcasted_iota(jnp.int32, (np_, LANES, LANES), 1)
    bj = lax.broadcasted_iota(jnp.int32, (np_, LANES, LANES), 2)
    eye = bi == bj
    same_head = (bi // HEAD) == (bj // HEAD)

    r, k, v, a_, lw_, cum_ = r_s[...], k_s[...], v_s[...], a_s[...], lw_s[...], cum_s[...]
    kk = k * kk_ref[...]
    kk = kk / jnp.maximum(jnp.sqrt(head_sum(kk * kk)), 1e-12)
    kmod = k * (1.0 + (a_ - 1.0) * ka_ref[...])
    b = kk * a_
    bonus = head_sum(r * kmod * rk_ref[...]) * v

    tot = cum_[:, c - 1:c, :]
    at = -kk * jnp.exp(cum_ - lw_)
    rt = r * jnp.exp(cum_)
    einv = jnp.exp(-cum_)
    eend = jnp.exp(tot - cum_)
    bh, kh = _b16(b * einv), _b16(kmod * einv)
    bt, kt = _b16(b * eend), _b16(kmod * eend)
    at_b = _b16(at)

    aa = _bdot_nt(jnp.concatenate([at_b, _b16(rt)], axis=1),
                  jnp.concatenate([_stack_heads(bh), _stack_heads(kh)], axis=1))
    a_ab = jnp.where(strict, aa[:, :c, :LANES], 0.0)
    a_ak = _b16(jnp.where(strict, aa[:, :c, LANES:], 0.0))
    a_rb = _b16(jnp.where(incl, aa[:, c:, :LANES], 0.0))
    a_rk = _b16(jnp.where(incl, aa[:, c:, LANES:], 0.0))

    n_blk = _stack_heads(a_ab)
    t_inv = jnp.where(eye, 1.0, 0.0) + jnp.where((bi == bj + 1) & (bi % 2 == 1), n_blk, 0.0)
    size = 2
    while size < c:
        e_mask = ((bi // (2 * size)) == (bj // (2 * size))) & (bi % (2 * size) >= size) & (bj % (2 * size) < size)
        t_b = _b16(t_inv)
        te = _bdot(t_b, _b16(jnp.where(e_mask, n_blk, 0.0)))
        t_inv = t_inv + _bdot(_b16(te), t_b)
        size *= 2
    t_pair = _b16(t_inv[:, :c] + t_inv[:, c:])

    v_b = _b16(v)
    v_st = _stack_heads(v_b)
    g0 = _bdot(a_ak, v_st)
    uw = _bdot(t_pair, jnp.concatenate([_stack_heads(_b16(g0)), _stack_heads(at_b)], axis=2))
    u0, w = _b16(uw[:, :, :LANES]), _b16(uw[:, :, LANES:])
    rw = rt + _bdot(a_rb, _stack_heads(w))
    y0 = _bdot(a_rb, _stack_heads(u0)) + _bdot(a_rk, v_st)
    m1 = jnp.where(eye, jnp.exp(tot), 0.0) + jnp.where(same_head, _bdot_tn(bt, w), 0.0)
    m0 = jnp.where(same_head, _bdot_tn(bt, u0) + _bdot_tn(kt, v_b), 0.0)

    h0 = _b16(h_ref[...])
    y = _bdot(_b16(rw), h0) + y0
    h_ref[...] = _bdot(_b16(m1), h0) + m0

    mean = head_sum(y) * (1.0 / HEAD)
    yc = y - mean
    var = head_sum(yc * yc) * (1.0 / HEAD)
    yn = yc * lax.rsqrt(var + GN_EPS) * lnw_ref[...] + lnb_ref[...]
    out = (yn + bonus) * g_s[...]
    for p in range(n_pairs):
        o_ref[:, p * LANES:(p + 1) * LANES] = out[p]


def _rwkv(proj, mu, w0, w2, a0, a2, g2, k_k, k_a, r_k, ln_w, ln_b):
    s = proj.shape[0]
    width = w0.shape[0]
    n_pairs = width // LANES
    shift_w = mu.shape[0]
    dl, al = w2.shape[0], a2.shape[0]
    assert dl + al == LANES and g2.shape[0] == LANES and shift_w == 3 * width + 2 * LANES
    wl = jnp.zeros((LANES, 2 * width), F32).at[:dl, :width].set(w2).at[dl:, width:].set(a2)
    wl_hi, wl_lo = _split_bf16(wl)

    def per_pair(t):
        return t.reshape(n_pairs, 1, LANES)

    c = CHUNK
    pair_spec = _const_spec((n_pairs, 1, LANES))
    return pl.pallas_call(
        _rwkv_kernel,
        grid=(s // c,),
        in_specs=[pl.BlockSpec((c, shift_w), lambda i: (i, 0)),
                  _const_spec((1, shift_w)), _const_spec((LANES, 2 * width)), _const_spec((LANES, 2 * width)),
                  _const_spec((LANES, width)),
                  _const_spec((1, width)), _const_spec((1, width)),
                  pair_spec, pair_spec, pair_spec, pair_spec, pair_spec],
        out_specs=pl.BlockSpec((c, width), lambda i: (i, 0)),
        out_shape=jax.ShapeDtypeStruct((s, width), F32),
        scratch_shapes=[pltpu.VMEM((1, shift_w), F32), pltpu.VMEM((n_pairs, LANES, LANES), F32)]
                       + [pltpu.VMEM((n_pairs, c, LANES), F32)] * 7,
        compiler_params=_cparams("arbitrary"),
        name="rwkv7",
    )(proj, mu.reshape(1, shift_w), wl_hi, wl_lo, g2.astype(BF16), w0.reshape(1, width), a0.reshape(1, width),
      per_pair(k_k), per_pair(k_a), per_pair(r_k.reshape(-1)), per_pair(ln_w), per_pair(ln_b))


def _shared_kv_kernel(x_ref, g_ref, wk_ref, wvt_ref, kn_ref, k_ref, vt_ref, km_ref):
    h = _rms(x_ref[...], g_ref[...]).astype(BF16)
    k = _dot(h, wk_ref[...])
    n_pairs = k_ref.shape[0]
    for p in range(n_pairs):
        kp = _head_rms(k[:, p * LANES:(p + 1) * LANES], kn_ref[...])
        k_ref[p, 0] = kp.astype(BF16)
        km_ref[0, :, p * LANES:(p + 1) * LANES] = jnp.mean(kp, axis=0, keepdims=True)
    vt = _dot_nt(wvt_ref[...], h)
    vt_ref[:, 0] = vt.reshape(n_pairs, LANES, vt.shape[1]).astype(BF16)


def _shared_kv(x, g, w_kv, k_norm):
    s, d = x.shape
    width = w_kv.shape[1] // 2
    n_pairs = width // LANES
    blk = MOBA_BLOCK
    nb = s // blk
    w = w_kv.astype(BF16)
    kn = jnp.tile(k_norm, LANES // HEAD).reshape(1, LANES)
    k, vt, km = pl.pallas_call(
        _shared_kv_kernel,
        grid=(nb,),
        in_specs=[pl.BlockSpec((blk, d), lambda i: (i, 0)), _const_spec((1, d)),
                  _const_spec((d, width)), _const_spec((width, d)), _const_spec((1, LANES))],
        out_specs=[pl.BlockSpec((n_pairs, 1, blk, LANES), lambda i: (0, i, 0, 0)),
                   pl.BlockSpec((n_pairs, 1, LANES, blk), lambda i: (0, i, 0, 0)),
                   pl.BlockSpec((1, 1, width), lambda i: (i, 0, 0))],
        out_shape=[jax.ShapeDtypeStruct((n_pairs, nb, blk, LANES), BF16),
                   jax.ShapeDtypeStruct((n_pairs, nb, LANES, blk), BF16),
                   jax.ShapeDtypeStruct((nb, 1, width), F32)],
        compiler_params=_cparams("parallel"),
        name="shared_kv",
    )(x, g.reshape(1, d), w[:, :width], w[:, width:].T, kn)
    return k, vt, km.reshape(nb, width)


def _moba_kernel(slopes_ref, q_ref, qn_ref, km_ref, k_ref, vt_ref, o_ref, sel_ref, bias_ref):
    p = pl.program_id(0)
    i = pl.program_id(1)
    blk = MOBA_BLOCK
    nb = km_ref.shape[0]
    q = _head_rms(q_ref[...], qn_ref[...]) * (HEAD ** -0.5)
    lane = lax.broadcasted_iota(jnp.int32, (1, LANES), 1)
    q2 = jnp.concatenate([jnp.where(lane < HEAD, q, 0.0), jnp.where(lane >= HEAD, q, 0.0)], axis=0)
    qb = (q2 * LOG2E).astype(BF16)
    col = lax.broadcasted_iota(jnp.int32, (1, 2 * blk), 1)
    slope = jnp.where(col < blk, slopes_ref[2 * p], slopes_ref[2 * p + 1]) * LOG2E

    n_idx = lax.broadcasted_iota(jnp.int32, (nb, 2 * blk), 0)
    gate = jnp.where(n_idx < i, _dot_nt(km_ref[...], q2, HI), NEG_INF)
    sel_bias = jnp.full(gate.shape, NEG_INF, F32)
    for _ in range(MOBA_TOPK):
        mx = jnp.max(gate, axis=0, keepdims=True)
        first = jnp.min(jnp.where(gate == mx, n_idx, nb), axis=0, keepdims=True)
        pick = (n_idx == first) & (mx > NEG_INF)
        sel_bias = jnp.where(pick, 0.0, sel_bias)
        gate = jnp.where(pick, NEG_INF, gate)
    sel_ref[...] = sel_bias - slope * ((i - n_idx) * blk).astype(F32)

    key_pos = lax.broadcasted_iota(jnp.int32, (blk, 2 * blk), 0)
    qry_pos = lax.broadcasted_iota(jnp.int32, (blk, 2 * blk), 1) % blk
    dist = (qry_pos - key_pos).astype(F32)
    bias_ref[...] = -slope * dist

    def weighted_values(n, e):
        eb = e.astype(BF16)
        return jnp.concatenate([_dot(vt_ref[n, :HEAD, :], eb[:, :blk]), _dot(vt_ref[n, HEAD:, :], eb[:, blk:])],
                               axis=1)

    s = jnp.where(dist >= 0.0, _dot_nt(k_ref[i], qb) + bias_ref[...], NEG_INF)
    m = jnp.max(s, axis=0, keepdims=True)
    e = jnp.exp2(s - m)
    l = jnp.sum(e, axis=0, keepdims=True)
    acc = weighted_values(i, e)

    def two_blocks(n0, n1, m, l, acc):
        r0 = sel_ref[pl.ds(n0, 1), :]
        r1 = sel_ref[pl.ds(n1, 1), :]
        s0 = _dot_nt(k_ref[n0], qb) + bias_ref[...]
        s1 = _dot_nt(k_ref[n1], qb) + bias_ref[...]
        t0 = jnp.max(s0, axis=0, keepdims=True) + r0
        t1 = jnp.max(s1, axis=0, keepdims=True) + r1
        m_new = jnp.maximum(m, jnp.maximum(t0, t1))
        alpha = jnp.exp2(m - m_new)
        e0 = jnp.exp2(s0 - (m_new - r0))
        e1 = jnp.exp2(s1 - (m_new - r1))
        l = alpha * l + jnp.sum(e0 + e1, axis=0, keepdims=True)
        acc = alpha * acc + (weighted_values(n0, e0) + weighted_values(n1, e1))
        return m_new, l, acc

    def four_past_blocks(t, carry):
        n = [jnp.minimum(4 * t + j, nb - 1) for j in range(4)]
        carry = two_blocks(n[0], n[1], *carry)
        return two_blocks(n[2], n[3], *carry)

    m, l, acc = lax.fori_loop(0, (i + 3) // 4, four_past_blocks, (m, l, acc))
    out = acc / l
    o_ref[...] = jnp.concatenate([out[:, :blk], out[:, blk:]], axis=0).T


def _moba(proj, q_norm, k, vt, km):
    s = proj.shape[0]
    n_pairs, nb, blk, _ = k.shape
    heads = n_pairs * (LANES // HEAD)
    slopes = jnp.exp2(-ALIBI_MAX * jnp.arange(1, heads + 1, dtype=F32) / heads)
    qn = jnp.tile(q_norm, LANES // HEAD).reshape(1, LANES)
    return pl.pallas_call(
        _moba_kernel,
        grid=(n_pairs, nb),
        in_specs=[pl.BlockSpec(memory_space=pltpu.SMEM),
                  pl.BlockSpec((blk, LANES), lambda p, i: (i, p)),
                  pl.BlockSpec((1, LANES), lambda p, i: (0, 0)),
                  pl.BlockSpec((nb, LANES), lambda p, i: (0, p)),
                  pl.BlockSpec((None, nb, blk, LANES), lambda p, i: (p, 0, 0, 0)),
                  pl.BlockSpec((None, nb, LANES, blk), lambda p, i: (p, 0, 0, 0))],
        out_specs=pl.BlockSpec((blk, LANES), lambda p, i: (i, p)),
        out_shape=jax.ShapeDtypeStruct((s, n_pairs * LANES), F32),
        scratch_shapes=[pltpu.VMEM((nb, 2 * blk), F32), pltpu.VMEM((blk, 2 * blk), F32)],
        compiler_params=_cparams("parallel", "arbitrary"),
        name="moba",
    )(slopes, proj, qn, km, k, vt)


def kernel(x, mem, ffn1_norm, ffn1_w1, ffn1_w3, ffn1_w2, mix_norm, w_out, mem_norm, w_mem_kv, mem_q_norm, mem_k_norm, ffn2_norm, ffn2_w1, ffn2_w3, ffn2_w2, rwkv_w_in, rwkv_mu, rwkv_w0, rwkv_w2, rwkv_a0, rwkv_a2, rwkv_g2, rwkv_k_k, rwkv_k_a, rwkv_r_k, rwkv_ln_w, rwkv_ln_b, kv_norm, w_kv, kv_k_norm, moba_w_in, moba_q_norm):
    batch = x.shape[0]
    depth = ffn1_norm.shape[0]
    n_a = rwkv_w_in.shape[0]
    mix_w = rwkv_w0.shape[1]
    mem_w = w_mem_kv.shape[2] // 2
    outs = []
    for bi in range(batch):
        xb, memb = x[bi], mem[bi]
        shared = None
        for l in range(depth):
            xb = _ffn(xb, ffn1_norm[l], ffn1_w1[l], ffn1_w3[l], ffn1_w2[l])
            if l < n_a:
                proj = _norm_proj(xb, mix_norm[l], rwkv_w_in[l])
                mix = _rwkv(proj, rwkv_mu[l], rwkv_w0[l], rwkv_w2[l], rwkv_a0[l], rwkv_a2[l], rwkv_g2[l],
                            rwkv_k_k[l], rwkv_k_a[l], rwkv_r_k[l], rwkv_ln_w[l], rwkv_ln_b[l])
                mem_col = rwkv_mu.shape[1] // mem_w
            else:
                j = l - n_a
                proj = _norm_proj(xb, mix_norm[l], moba_w_in[j])
                mix = _moba(proj, moba_q_norm[j], *shared)
                mem_col = mix_w // mem_w
            mk, mv = _mem_kv(memb, mem_norm[l], w_mem_kv[l], mem_k_norm[l])
            mem_out = _mem_attn(proj, mem_col, mem_q_norm[l], mk, mv)
            xb = _out_proj(xb, mix, mem_out, w_out[l])
            xb = _ffn(xb, ffn2_norm[l], ffn2_w1[l], ffn2_w3[l], ffn2_w2[l])
            if l == n_a - 1:
                shared = _shared_kv(xb, kv_norm, w_kv, kv_k_norm)
        outs.append(xb)
    return jnp.stack(outs)
```

```python
import functools

import jax
import jax.numpy as jnp
from jax import lax
from jax.experimental import pallas as pl
from jax.experimental.pallas import tpu as pltpu

F32 = jnp.float32
BF16 = jnp.bfloat16
HI = lax.Precision.HIGHEST

HEAD = 64
LANES = 128
MXU = 256
CHUNK = 64
MOBA_BLOCK = 256
MOBA_TOPK = 3
ALIBI_MAX = 8.0
NORM_EPS = 1e-6
GN_EPS = 64e-5
VMEM_LIMIT = 56 * 1024 * 1024
NEG_INF = float("-inf")
LOG2E = 1.4426950408889634


def _cparams(*sem):
    return pltpu.CompilerParams(dimension_semantics=sem, vmem_limit_bytes=VMEM_LIMIT)


def _const_spec(shape):
    return pl.BlockSpec(shape, lambda *_: (0,) * len(shape), pipeline_mode=pl.Buffered(1))


def _rms(x, g):
    return x * lax.rsqrt(jnp.mean(x * x, axis=-1, keepdims=True) + NORM_EPS) * g


def _b16(x):
    return x.astype(BF16)


def _split_bf16(x):
    hi = x.astype(BF16)
    return hi, (x - hi.astype(F32)).astype(BF16)


def _dot(a, b, precision=None):
    return jnp.dot(a, b, preferred_element_type=F32, precision=precision)


def _dot_nt(a, b, precision=None):
    return lax.dot_general(a, b, (((1,), (1,)), ((), ())), preferred_element_type=F32, precision=precision)


def _dot_tn(a, b, precision=None):
    return lax.dot_general(a, b, (((0,), (0,)), ((), ())), preferred_element_type=F32, precision=precision)


def _head_sum_matrix(scale=1.0):
    i = lax.broadcasted_iota(jnp.int32, (LANES, LANES), 0) // HEAD
    j = lax.broadcasted_iota(jnp.int32, (LANES, LANES), 1) // HEAD
    return jnp.where(i == j, F32(scale), F32(0.0))


def _head_rms(x, g):
    ms = _dot(x * x, _head_sum_matrix(1.0 / HEAD), HI)
    return x * lax.rsqrt(ms + NORM_EPS) * g


def _ffn_kernel(x_ref, g_ref, w1_ref, w3_ref, w2_ref, o_ref, acc_ref, *, f_chunk):
    x = x_ref[...]
    h = _rms(x, g_ref[...]).astype(BF16)
    n_chunks = w1_ref.shape[1] // f_chunk
    for c in range(n_chunks):
        sl = slice(c * f_chunk, (c + 1) * f_chunk)
        a = _dot(h, w1_ref[:, sl])
        b = _dot(h, w3_ref[:, sl])
        act = (a * jax.nn.sigmoid(a) * b).astype(BF16)
        part = _dot(act, w2_ref[sl, :])
        if c == 0:
            acc_ref[...] = part
        else:
            acc_ref[...] += part
    o_ref[...] = x + 0.5 * acc_ref[...]


def _ffn(x, g, w1, w3, w2, *, tm=512, f_chunk=MXU):
    s, d = x.shape
    f = w1.shape[1]
    fp = -(-f // f_chunk) * f_chunk
    w1 = jnp.pad(w1.astype(BF16), ((0, 0), (0, fp - f)))
    w3 = jnp.pad(w3.astype(BF16), ((0, 0), (0, fp - f)))
    w2 = jnp.pad(w2.astype(BF16), ((0, fp - f), (0, 0)))
    tm = min(tm, s)
    return pl.pallas_call(
        functools.partial(_ffn_kernel, f_chunk=f_chunk),
        grid=(s // tm,),
        in_specs=[pl.BlockSpec((tm, d), lambda i: (i, 0)),
                  _const_spec((1, d)), _const_spec((d, fp)), _const_spec((d, fp)), _const_spec((fp, d))],
        out_specs=pl.BlockSpec((tm, d), lambda i: (i, 0)),
        out_shape=jax.ShapeDtypeStruct((s, d), F32),
        scratch_shapes=[pltpu.VMEM((tm, d), F32)],
        compiler_params=_cparams("parallel"),
        name="ffn",
    )(x, g.reshape(1, d), w1, w3, w2)


def _norm_proj_kernel(x_ref, g_ref, w_ref, o_ref):
    h = _rms(x_ref[...], g_ref[...]).astype(BF16)
    o_ref[...] = _dot(h, w_ref[...])


def _norm_proj(x, g, w, *, tm=512):
    s, d = x.shape
    n = w.shape[1]
    tm = min(tm, s)
    return pl.pallas_call(
        _norm_proj_kernel,
        grid=(s // tm,),
        in_specs=[pl.BlockSpec((tm, d), lambda i: (i, 0)), _const_spec((1, d)), _const_spec((d, n))],
        out_specs=pl.BlockSpec((tm, n), lambda i: (i, 0)),
        out_shape=jax.ShapeDtypeStruct((s, n), F32),
        compiler_params=_cparams("parallel"),
        name="norm_proj",
    )(x, g.reshape(1, d), w.astype(BF16))


def _out_proj_kernel(x_ref, mix_ref, mem_ref, wa_ref, wb_ref, o_ref):
    o_ref[...] = (x_ref[...] + _dot(mix_ref[...].astype(BF16), wa_ref[...])
                  + _dot(mem_ref[...].astype(BF16), wb_ref[...]))


def _out_proj(x, mix, mem_out, w_out, *, tm=512):
    s, d = x.shape
    na, nb = mix.shape[1], mem_out.shape[1]
    tm = min(tm, s)
    w = w_out.astype(BF16)
    return pl.pallas_call(
        _out_proj_kernel,
        grid=(s // tm,),
        in_specs=[pl.BlockSpec((tm, d), lambda i: (i, 0)),
                  pl.BlockSpec((tm, na), lambda i: (i, 0)),
                  pl.BlockSpec((tm, nb), lambda i: (i, 0)),
                  _const_spec((na, d)), _const_spec((nb, d))],
        out_specs=pl.BlockSpec((tm, d), lambda i: (i, 0)),
        out_shape=jax.ShapeDtypeStruct((s, d), F32),
        compiler_params=_cparams("parallel"),
        name="out_proj",
    )(x, mix, mem_out, w[:na], w[na:])


def _mem_kv_kernel(mem_ref, g_ref, w_ref, kn_ref, k_ref, v_ref):
    h = _rms(mem_ref[...], g_ref[...]).astype(BF16)
    kv = _dot(h, w_ref[...])
    width = k_ref.shape[1]
    for p in range(width // LANES):
        sl = slice(p * LANES, (p + 1) * LANES)
        k_ref[:, sl] = _head_rms(kv[:, sl], kn_ref[...])
    v_ref[...] = kv[:, width:]


def _mem_kv(mem, g, w, k_norm):
    m, d = mem.shape
    width = w.shape[1] // 2
    kn = jnp.tile(k_norm, LANES // HEAD).reshape(1, LANES)
    return pl.pallas_call(
        _mem_kv_kernel,
        grid=(1,),
        in_specs=[_const_spec((m, d)), _const_spec((1, d)), _const_spec((d, 2 * width)), _const_spec((1, LANES))],
        out_specs=[pl.BlockSpec((m, width), lambda i: (0, 0)), pl.BlockSpec((m, width), lambda i: (0, 0))],
        out_shape=[jax.ShapeDtypeStruct((m, width), F32)] * 2,
        compiler_params=_cparams("arbitrary"),
        name="mem_kv",
    )(mem, g.reshape(1, d), w.astype(BF16), kn)


def _mem_attn_kernel(q_ref, qn_ref, k_ref, v_ref, o_ref):
    lane = lax.broadcasted_iota(jnp.int32, (1, LANES), 1)
    width = q_ref.shape[1]
    for p in range(width // LANES):
        sl = slice(p * LANES, (p + 1) * LANES)
        q = _head_rms(q_ref[:, sl], qn_ref[...]) * (HEAD ** -0.5)
        k = k_ref[:, sl].astype(BF16)
        v = v_ref[:, sl].astype(BF16)
        out = None
        for h in range(LANES // HEAD):
            in_head = (lane // HEAD) == h
            qh = jnp.where(in_head, q, 0.0).astype(BF16)
            s = _dot_nt(qh, k)
            s = s - jnp.max(s, axis=-1, keepdims=True)
            e = jnp.exp(s)
            pr = e / jnp.sum(e, axis=-1, keepdims=True)
            oh = _dot(pr.astype(BF16), v)
            out = oh if out is None else jnp.where(in_head, oh, out)
        o_ref[:, sl] = out


def _mem_attn(proj, col_block, q_norm, k, v, *, tm=512):
    s = proj.shape[0]
    m, width = k.shape
    tm = min(tm, s)
    qn = jnp.tile(q_norm, LANES // HEAD).reshape(1, LANES)
    return pl.pallas_call(
        _mem_attn_kernel,
        grid=(s // tm,),
        in_specs=[pl.BlockSpec((tm, width), lambda i: (i, col_block)),
                  _const_spec((1, LANES)), _const_spec((m, width)), _const_spec((m, width))],
        out_specs=pl.BlockSpec((tm, width), lambda i: (i, 0)),
        out_shape=jax.ShapeDtypeStruct((s, width), F32),
        compiler_params=_cparams("parallel"),
        name="mem_attn",
    )(proj, qn, k, v)


def _stack_heads(z):
    lane = lax.broadcasted_iota(jnp.int32, z.shape, 2)
    zero = jnp.zeros_like(z)
    return jnp.concatenate([jnp.where(lane < HEAD, z, zero), jnp.where(lane >= HEAD, z, zero)], axis=1)


def _bdot(a, b):
    return lax.dot_general(a, b, (((2,), (1,)), ((0,), (0,))), preferred_element_type=F32)


def _bdot_nt(a, b):
    return lax.dot_general(a, b, (((2,), (2,)), ((0,), (0,))), preferred_element_type=F32)


def _bdot_tn(a, b):
    return lax.dot_general(a, b, (((1,), (1,)), ((0,), (0,))), preferred_element_type=F32)


def _rwkv_kernel(proj_ref, mu_ref, wlh_ref, wll_ref, g2_ref, w0_ref, a0_ref, kk_ref, ka_ref, rk_ref, lnw_ref, lnb_ref,
                 o_ref, last_ref, h_ref, r_s, k_s, v_s, a_s, lw_s, cum_s, g_s):
    c = CHUNK
    n_pairs = r_s.shape[0]
    width = n_pairs * LANES

    @pl.when(pl.program_id(0) == 0)
    def _():
        last_ref[...] = jnp.zeros_like(last_ref)
        h_ref[...] = jnp.zeros_like(h_ref)

    u = proj_ref[...]
    row = lax.broadcasted_iota(jnp.int32, u.shape, 0)
    prev = jnp.where(row == 0, last_ref[...], pltpu.roll(u, 1, 0))
    last_ref[...] = u[c - 1:c, :]
    u = u + (prev - u) * mu_ref[...]

    lora_in = u[:, 3 * width:3 * width + LANES]
    lane = lax.broadcasted_iota(jnp.int32, lora_in.shape, 1)
    lora_in = jnp.where(lane < HEAD, jnp.tanh(lora_in), lora_in)
    x_hi, x_lo = _split_bf16(lora_in)
    lo = _dot(x_hi, wlh_ref[...]) + (_dot(x_lo, wlh_ref[...]) + _dot(x_hi, wll_ref[...]))
    z = -(w0_ref[...] + lo[:, :width])
    softplus = jnp.maximum(z, 0.0) + jnp.log(1.0 + jnp.exp(-jnp.abs(z)))
    lw = -jnp.exp(-softplus - 0.5)
    a = jax.nn.sigmoid(a0_ref[...] + lo[:, width:])
    g = _dot(_b16(jax.nn.sigmoid(u[:, 3 * width + LANES:])), g2_ref[...])
    ti = lax.broadcasted_iota(jnp.int32, (c, c), 0)
    si = lax.broadcasted_iota(jnp.int32, (c, c), 1)
    tri = _b16(jnp.where(si <= ti, F32(1.0), F32(0.0)))
    lw_hi, lw_lo = _split_bf16(lw)
    cum = _dot(tri, lw_hi) + _dot(tri, lw_lo)

    for p in range(n_pairs):
        sl = slice(p * LANES, (p + 1) * LANES)
        r_s[p] = u[:, sl]
        k_s[p] = u[:, width + p * LANES:width + (p + 1) * LANES]
        v_s[p] = u[:, 2 * width + p * LANES:2 * width + (p + 1) * LANES]
        a_s[p] = a[:, sl]
        lw_s[p] = lw[:, sl]
        cum_s[p] = cum[:, sl]
        g_s[p] = g[:, sl]

    np_ = n_pairs
    hsum = _b16(_head_sum_matrix())

    def head_sum(z):
        return _dot(_b16(z).reshape(np_ * c, LANES), hsum).reshape(np_, c, LANES)

    col = lax.broadcasted_iota(jnp.int32, (np_, c, LANES), 2) % c
    trow = lax.broadcasted_iota(jnp.int32, (np_, c, LANES), 1)
    strict = col < trow
    incl = col <= trow
    bi = lax.broadcasted_iota(jnp.int32, (np_, LANES, LANES), 1)
    bj = lax.broadcasted_iota(jnp.int32, (np_, LANES, LANES), 2)
    eye = bi == bj
    same_head = (bi // HEAD) == (bj // HEAD)

    r, k, v, a_, lw_, cum_ = r_s[...], k_s[...], v_s[...], a_s[...], lw_s[...], cum_s[...]
    kk = k * kk_ref[...]
    kk = kk / jnp.maximum(jnp.sqrt(head_sum(kk * kk)), 1e-12)
    kmod = k * (1.0 + (a_ - 1.0) * ka_ref[...])
    b = kk * a_
    bonus = head_sum(r * kmod * rk_ref[...]) * v

    tot = cum_[:, c - 1:c, :]
    at = -kk * jnp.exp(cum_ - lw_)
    rt = r * jnp.exp(cum_)
    einv = jnp.exp(-cum_)
    eend = jnp.exp(tot - cum_)
    bh, kh = _b16(b * einv), _b16(kmod * einv)
    bt, kt = _b16(b * eend), _b16(kmod * eend)
    at_b = _b16(at)

    aa = _bdot_nt(jnp.concatenate([at_b, _b16(rt)], axis=1),
                  jnp.concatenate([_stack_heads(bh), _stack_heads(kh)], axis=1))
    a_ab = jnp.where(strict, aa[:, :c, :LANES], 0.0)
    a_ak = _b16(jnp.where(strict, aa[:, :c, LANES:], 0.0))
    a_rb = _b16(jnp.where(incl, aa[:, c:, :LANES], 0.0))
    a_rk = _b16(jnp.where(incl, aa[:, c:, LANES:], 0.0))

    n_blk = _stack_heads(a_ab)
    t_inv = jnp.where(eye, 1.0, 0.0) + jnp.where((bi == bj + 1) & (bi % 2 == 1), n_blk, 0.0)
    size = 2
    while size < c:
        e_mask = ((bi // (2 * size)) == (bj // (2 * size))) & (bi % (2 * size) >= size) & (bj % (2 * size) < size)
        t_b = _b16(t_inv)
        te = _bdot(t_b, _b16(jnp.where(e_mask, n_blk, 0.0)))
        t_inv = t_inv + _bdot(_b16(te), t_b)
        size *= 2
    t_pair = _b16(t_inv[:, :c] + t_inv[:, c:])

    v_b = _b16(v)
    v_st = _stack_heads(v_b)
    g0 = _bdot(a_ak, v_st)
    uw = _bdot(t_pair, jnp.concatenate([_stack_heads(_b16(g0)), _stack_heads(at_b)], axis=2))
    u0, w = _b16(uw[:, :, :LANES]), _b16(uw[:, :, LANES:])
    rw = rt + _bdot(a_rb, _stack_heads(w))
    y0 = _bdot(a_rb, _stack_heads(u0)) + _bdot(a_rk, v_st)
    m1 = jnp.where(eye, jnp.exp(tot), 0.0) + jnp.where(same_head, _bdot_tn(bt, w), 0.0)
    m0 = jnp.where(same_head, _bdot_tn(bt, u0) + _bdot_tn(kt, v_b), 0.0)

    h0 = _b16(h_ref[...])
    y = _bdot(_b16(rw), h0) + y0
    h_ref[...] = _bdot(_b16(m1), h0) + m0

    mean = head_sum(y) * (1.0 / HEAD)
    yc = y - mean
    var = head_sum(yc * yc) * (1.0 / HEAD)
    yn = yc * lax.rsqrt(var + GN_EPS) * lnw_ref[...] + lnb_ref[...]
    out = (yn + bonus) * g_s[...]
    for p in range(n_pairs):
        o_ref[:, p * LANES:(p + 1) * LANES] = out[p]


def _rwkv(proj, mu, w0, w2, a0, a2, g2, k_k, k_a, r_k, ln_w, ln_b):
    s = proj.shape[0]
    width = w0.shape[0]
    n_pairs = width // LANES
    shift_w = mu.shape[0]
    dl, al = w2.shape[0], a2.shape[0]
    assert dl + al == LANES and g2.shape[0] == LANES and shift_w == 3 * width + 2 * LANES
    wl = jnp.zeros((LANES, 2 * width), F32).at[:dl, :width].set(w2).at[dl:, width:].set(a2)
    wl_hi, wl_lo = _split_bf16(wl)

    def per_pair(t):
        return t.reshape(n_pairs, 1, LANES)

    c = CHUNK
    pair_spec = _const_spec((n_pairs, 1, LANES))
    return pl.pallas_call(
        _rwkv_kernel,
        grid=(s // c,),
        in_specs=[pl.BlockSpec((c, shift_w), lambda i: (i, 0)),
                  _const_spec((1, shift_w)), _const_spec((LANES, 2 * width)), _const_spec((LANES, 2 * width)),
                  _const_spec((LANES, width)),
                  _const_spec((1, width)), _const_spec((1, width)),
                  pair_spec, pair_spec, pair_spec, pair_spec, pair_spec],
        out_specs=pl.BlockSpec((c, width), lambda i: (i, 0)),
        out_shape=jax.ShapeDtypeStruct((s, width), F32),
        scratch_shapes=[pltpu.VMEM((1, shift_w), F32), pltpu.VMEM((n_pairs, LANES, LANES), F32)]
                       + [pltpu.VMEM((n_pairs, c, LANES), F32)] * 7,
        compiler_params=_cparams("arbitrary"),
        name="rwkv7",
    )(proj, mu.reshape(1, shift_w), wl_hi, wl_lo, g2.astype(BF16), w0.reshape(1, width), a0.reshape(1, width),
      per_pair(k_k), per_pair(k_a), per_pair(r_k.reshape(-1)), per_pair(ln_w), per_pair(ln_b))


def _shared_kv_kernel(x_ref, g_ref, wk_ref, wvt_ref, kn_ref, k_ref, vt_ref, km_ref):
    h = _rms(x_ref[...], g_ref[...]).astype(BF16)
    k = _dot(h, wk_ref[...])
    n_pairs = k_ref.shape[0]
    for p in range(n_pairs):
        kp = _head_rms(k[:, p * LANES:(p + 1) * LANES], kn_ref[...])
        k_ref[p, 0] = kp.astype(BF16)
        km_ref[0, :, p * LANES:(p + 1) * LANES] = jnp.mean(kp, axis=0, keepdims=True)
    vt = _dot_nt(wvt_ref[...], h)
    vt_ref[:, 0] = vt.reshape(n_pairs, LANES, vt.shape[1]).astype(BF16)


def _shared_kv(x, g, w_kv, k_norm):
    s, d = x.shape
    width = w_kv.shape[1] // 2
    n_pairs = width // LANES
    blk = MOBA_BLOCK
    nb = s // blk
    w = w_kv.astype(BF16)
    kn = jnp.tile(k_norm, LANES // HEAD).reshape(1, LANES)
    k, vt, km = pl.pallas_call(
        _shared_kv_kernel,
        grid=(nb,),
        in_specs=[pl.BlockSpec((blk, d), lambda i: (i, 0)), _const_spec((1, d)),
                  _const_spec((d, width)), _const_spec((width, d)), _const_spec((1, LANES))],
        out_specs=[pl.BlockSpec((n_pairs, 1, blk, LANES), lambda i: (0, i, 0, 0)),
                   pl.BlockSpec((n_pairs, 1, LANES, blk), lambda i: (0, i, 0, 0)),
                   pl.BlockSpec((1, 1, width), lambda i: (i, 0, 0))],
        out_shape=[jax.ShapeDtypeStruct((n_pairs, nb, blk, LANES), BF16),
                   jax.ShapeDtypeStruct((n_pairs, nb, LANES, blk), BF16),
                   jax.ShapeDtypeStruct((nb, 1, width), F32)],
        compiler_params=_cparams("parallel"),
        name="shared_kv",
    )(x, g.reshape(1, d), w[:, :width], w[:, width:].T, kn)
    return k, vt, km.reshape(nb, width)


def _moba_kernel(slopes_ref, q_ref, qn_ref, km_ref, k_ref, vt_ref, o_ref, sel_ref, bias_ref, sa_ref, sb_ref):
    p = pl.program_id(0)
    i = pl.program_id(1)
    blk = MOBA_BLOCK
    nb = km_ref.shape[0]
    q = _head_rms(q_ref[...], qn_ref[...]) * (HEAD ** -0.5)
    lane = lax.broadcasted_iota(jnp.int32, (1, LANES), 1)
    q2 = jnp.concatenate([jnp.where(lane < HEAD, q, 0.0), jnp.where(lane >= HEAD, q, 0.0)], axis=0)
    qb = (q2 * LOG2E).astype(BF16)
    col = lax.broadcasted_iota(jnp.int32, (1, 2 * blk), 1)
    slope = jnp.where(col < blk, slopes_ref[2 * p], slopes_ref[2 * p + 1]) * LOG2E

    n_idx = lax.broadcasted_iota(jnp.int32, (nb, 2 * blk), 0)
    gate = jnp.where(n_idx < i, _dot_nt(km_ref[...], q2, HI), NEG_INF)
    sel_bias = jnp.full(gate.shape, NEG_INF, F32)
    for _ in range(MOBA_TOPK):
        mx = jnp.max(gate, axis=0, keepdims=True)
        first = jnp.min(jnp.where(gate == mx, n_idx, nb), axis=0, keepdims=True)
        pick = (n_idx == first) & (mx > NEG_INF)
        sel_bias = jnp.where(pick, 0.0, sel_bias)
        gate = jnp.where(pick, NEG_INF, gate)
    sel_ref[...] = sel_bias - slope * ((i - n_idx) * blk).astype(F32)

    key_pos = lax.broadcasted_iota(jnp.int32, (blk, 2 * blk), 0)
    qry_pos = lax.broadcasted_iota(jnp.int32, (blk, 2 * blk), 1) % blk
    dist = (qry_pos - key_pos).astype(F32)
    bias_ref[...] = -slope * dist

    def weighted_values(n, e):
        eb = e.astype(BF16)
        return jnp.concatenate([_dot(vt_ref[n, :HEAD, :], eb[:, :blk]), _dot(vt_ref[n, HEAD:, :], eb[:, blk:])],
                               axis=1)

    def block_of(g, j):
        return jnp.minimum(2 * g + j, nb - 1)

    def score_group(g, s_ref):
        col_max = []
        for j in range(2):
            n = block_of(g, j)
            s = _dot_nt(k_ref[n], qb) + bias_ref[...]
            s_ref[j] = s
            col_max.append(jnp.max(s, axis=0, keepdims=True) + sel_ref[pl.ds(n, 1), :])
        return col_max

    def consume_group(g, s_ref, col_max, m, l, acc):
        n0, n1 = block_of(g, 0), block_of(g, 1)
        m_new = jnp.maximum(m, jnp.maximum(col_max[0], col_max[1]))
        alpha = jnp.exp2(m - m_new)
        e0 = jnp.exp2(s_ref[0] - (m_new - sel_ref[pl.ds(n0, 1), :]))
        e1 = jnp.exp2(s_ref[1] - (m_new - sel_ref[pl.ds(n1, 1), :]))
        l = alpha * l + jnp.sum(e0 + e1, axis=0, keepdims=True)
        acc = alpha * acc + (weighted_values(n0, e0) + weighted_values(n1, e1))
        return m_new, l, acc

    max_a = score_group(0, sa_ref)

    s = jnp.where(dist >= 0.0, _dot_nt(k_ref[i], qb) + bias_ref[...], NEG_INF)
    m = jnp.max(s, axis=0, keepdims=True)
    e = jnp.exp2(s - m)
    l = jnp.sum(e, axis=0, keepdims=True)
    acc = weighted_values(i, e)

    def two_groups(u, carry):
        max_a0, max_a1, m, l, acc = carry
        max_b = score_group(2 * u + 1, sb_ref)
        m, l, acc = consume_group(2 * u, sa_ref, (max_a0, max_a1), m, l, acc)
        max_a = score_group(2 * u + 2, sa_ref)
        m, l, acc = consume_group(2 * u + 1, sb_ref, max_b, m, l, acc)
        return max_a[0], max_a[1], m, l, acc

    _, _, m, l, acc = lax.fori_loop(0, (i + 3) // 4, two_groups, (max_a[0], max_a[1], m, l, acc))
    out = acc / l
    o_ref[...] = jnp.concatenate([out[:, :blk], out[:, blk:]], axis=0).T


def _moba(proj, q_norm, k, vt, km):
    s = proj.shape[0]
    n_pairs, nb, blk, _ = k.shape
    heads = n_pairs * (LANES // HEAD)
    slopes = jnp.exp2(-ALIBI_MAX * jnp.arange(1, heads + 1, dtype=F32) / heads)
    qn = jnp.tile(q_norm, LANES // HEAD).reshape(1, LANES)
    return pl.pallas_call(
        _moba_kernel,
        grid=(n_pairs, nb),
        in_specs=[pl.BlockSpec(memory_space=pltpu.SMEM),
                  pl.BlockSpec((blk, LANES), lambda p, i: (i, p)),
                  pl.BlockSpec((1, LANES), lambda p, i: (0, 0)),
                  pl.BlockSpec((nb, LANES), lambda p, i: (0, p)),
                  pl.BlockSpec((None, nb, blk, LANES), lambda p, i: (p, 0, 0, 0)),
                  pl.BlockSpec((None, nb, LANES, blk), lambda p, i: (p, 0, 0, 0))],
        out_specs=pl.BlockSpec((blk, LANES), lambda p, i: (i, p)),
        out_shape=jax.ShapeDtypeStruct((s, n_pairs * LANES), F32),
        scratch_shapes=[pltpu.VMEM((nb, 2 * blk), F32), pltpu.VMEM((blk, 2 * blk), F32),
                        pltpu.VMEM((2, blk, 2 * blk), F32), pltpu.VMEM((2, blk, 2 * blk), F32)],
        compiler_params=_cparams("parallel", "arbitrary"),
        name="moba",
    )(slopes, proj, qn, km, k, vt)


def kernel(x, mem, ffn1_norm, ffn1_w1, ffn1_w3, ffn1_w2, mix_norm, w_out, mem_norm, w_mem_kv, mem_q_norm, mem_k_norm, ffn2_norm, ffn2_w1, ffn2_w3, ffn2_w2, rwkv_w_in, rwkv_mu, rwkv_w0, rwkv_w2, rwkv_a0, rwkv_a2, rwkv_g2, rwkv_k_k, rwkv_k_a, rwkv_r_k, rwkv_ln_w, rwkv_ln_b, kv_norm, w_kv, kv_k_norm, moba_w_in, moba_q_norm):
    batch = x.shape[0]
    depth = ffn1_norm.shape[0]
    n_a = rwkv_w_in.shape[0]
    mix_w = rwkv_w0.shape[1]
    mem_w = w_mem_kv.shape[2] // 2
    outs = []
    for bi in range(batch):
        xb, memb = x[bi], mem[bi]
        shared = None
        for l in range(depth):
            xb = _ffn(xb, ffn1_norm[l], ffn1_w1[l], ffn1_w3[l], ffn1_w2[l])
            if l < n_a:
                proj = _norm_proj(xb, mix_norm[l], rwkv_w_in[l])
                mix = _rwkv(proj, rwkv_mu[l], rwkv_w0[l], rwkv_w2[l], rwkv_a0[l], rwkv_a2[l], rwkv_g2[l],
                            rwkv_k_k[l], rwkv_k_a[l], rwkv_r_k[l], rwkv_ln_w[l], rwkv_ln_b[l])
                mem_col = rwkv_mu.shape[1] // mem_w
            else:
                j = l - n_a
                proj = _norm_proj(xb, mix_norm[l], moba_w_in[j])
                mix = _moba(proj, moba_q_norm[j], *shared)
                mem_col = mix_w // mem_w
            mk, mv = _mem_kv(memb, mem_norm[l], w_mem_kv[l], mem_k_norm[l])
            mem_out = _mem_attn(proj, mem_col, mem_q_norm[l], mk, mv)
            xb = _out_proj(xb, mix, mem_out, w_out[l])
            xb = _ffn(xb, ffn2_norm[l], ffn2_w1[l], ffn2_w3[l], ffn2_w2[l])
            if l == n_a - 1:
                shared = _shared_kv(xb, kv_norm, w_kv, kv_k_norm)
        outs.append(xb)
    return jnp.stack(outs)
```

```python
import functools
import math

import jax
import jax.numpy as jnp
from jax import lax
from jax.experimental import pallas as pl
from jax.experimental.pallas import tpu as pltpu

F32 = jnp.float32
BF16 = jnp.bfloat16
HI = lax.Precision.HIGHEST

HEAD = 64
LANES = 128
MXU = 256
CHUNK = 64
MOBA_BLOCK = 256
MOBA_TOPK = 3
ALIBI_MAX = 8.0
NORM_EPS = 1e-6
GN_EPS = 64e-5
VMEM_LIMIT = 56 * 1024 * 1024
NEG_INF = float("-inf")
LOG2E = 1.4426950408889634
DEAD_EXPONENT = 160.0
EXP_HEADROOM = 100.0
BF16_SLACK = 1.02


def _cparams(*sem):
    return pltpu.CompilerParams(dimension_semantics=sem, vmem_limit_bytes=VMEM_LIMIT)


def _const_spec(shape):
    return pl.BlockSpec(shape, lambda *_: (0,) * len(shape), pipeline_mode=pl.Buffered(1))


def _rms(x, g):
    return x * lax.rsqrt(jnp.mean(x * x, axis=-1, keepdims=True) + NORM_EPS) * g


def _b16(x):
    return x.astype(BF16)


def _split_bf16(x):
    hi = x.astype(BF16)
    return hi, (x - hi.astype(F32)).astype(BF16)


def _dot(a, b, precision=None):
    return jnp.dot(a, b, preferred_element_type=F32, precision=precision)


def _dot_nt(a, b, precision=None):
    return lax.dot_general(a, b, (((1,), (1,)), ((), ())), preferred_element_type=F32, precision=precision)


def _dot_tn(a, b, precision=None):
    return lax.dot_general(a, b, (((0,), (0,)), ((), ())), preferred_element_type=F32, precision=precision)


def _head_sum_matrix(scale=1.0):
    i = lax.broadcasted_iota(jnp.int32, (LANES, LANES), 0) // HEAD
    j = lax.broadcasted_iota(jnp.int32, (LANES, LANES), 1) // HEAD
    return jnp.where(i == j, F32(scale), F32(0.0))


def _head_rms(x, g):
    ms = _dot(x * x, _head_sum_matrix(1.0 / HEAD), HI)
    return x * lax.rsqrt(ms + NORM_EPS) * g


def _ffn_kernel(x_ref, g_ref, w1_ref, w3_ref, w2_ref, o_ref, acc_ref, *, f_chunk):
    x = x_ref[...]
    h = _rms(x, g_ref[...]).astype(BF16)
    n_chunks = w1_ref.shape[1] // f_chunk
    for c in range(n_chunks):
        sl = slice(c * f_chunk, (c + 1) * f_chunk)
        a = _dot(h, w1_ref[:, sl])
        b = _dot(h, w3_ref[:, sl])
        act = (a * jax.nn.sigmoid(a) * b).astype(BF16)
        part = _dot(act, w2_ref[sl, :])
        if c == 0:
            acc_ref[...] = part
        else:
            acc_ref[...] += part
    o_ref[...] = x + 0.5 * acc_ref[...]


def _ffn(x, g, w1, w3, w2, *, tm=512, f_chunk=MXU):
    s, d = x.shape
    f = w1.shape[1]
    fp = -(-f // f_chunk) * f_chunk
    w1 = jnp.pad(w1.astype(BF16), ((0, 0), (0, fp - f)))
    w3 = jnp.pad(w3.astype(BF16), ((0, 0), (0, fp - f)))
    w2 = jnp.pad(w2.astype(BF16), ((0, fp - f), (0, 0)))
    tm = min(tm, s)
    return pl.pallas_call(
        functools.partial(_ffn_kernel, f_chunk=f_chunk),
        grid=(s // tm,),
        in_specs=[pl.BlockSpec((tm, d), lambda i: (i, 0)),
                  _const_spec((1, d)), _const_spec((d, fp)), _const_spec((d, fp)), _const_spec((fp, d))],
        out_specs=pl.BlockSpec((tm, d), lambda i: (i, 0)),
        out_shape=jax.ShapeDtypeStruct((s, d), F32),
        scratch_shapes=[pltpu.VMEM((tm, d), F32)],
        compiler_params=_cparams("parallel"),
        name="ffn",
    )(x, g.reshape(1, d), w1, w3, w2)


def _norm_proj_kernel(x_ref, g_ref, w_ref, o_ref):
    h = _rms(x_ref[...], g_ref[...]).astype(BF16)
    o_ref[...] = _dot(h, w_ref[...])


def _norm_proj(x, g, w, *, tm=512):
    s, d = x.shape
    n = w.shape[1]
    tm = min(tm, s)
    return pl.pallas_call(
        _norm_proj_kernel,
        grid=(s // tm,),
        in_specs=[pl.BlockSpec((tm, d), lambda i: (i, 0)), _const_spec((1, d)), _const_spec((d, n))],
        out_specs=pl.BlockSpec((tm, n), lambda i: (i, 0)),
        out_shape=jax.ShapeDtypeStruct((s, n), F32),
        compiler_params=_cparams("parallel"),
        name="norm_proj",
    )(x, g.reshape(1, d), w.astype(BF16))


def _out_proj_kernel(x_ref, mix_ref, mem_ref, wa_ref, wb_ref, o_ref):
    o_ref[...] = (x_ref[...] + _dot(mix_ref[...].astype(BF16), wa_ref[...])
                  + _dot(mem_ref[...].astype(BF16), wb_ref[...]))


def _out_proj(x, mix, mem_out, w_out, *, tm=512):
    s, d = x.shape
    na, nb = mix.shape[1], mem_out.shape[1]
    tm = min(tm, s)
    w = w_out.astype(BF16)
    return pl.pallas_call(
        _out_proj_kernel,
        grid=(s // tm,),
        in_specs=[pl.BlockSpec((tm, d), lambda i: (i, 0)),
                  pl.BlockSpec((tm, na), lambda i: (i, 0)),
                  pl.BlockSpec((tm, nb), lambda i: (i, 0)),
                  _const_spec((na, d)), _const_spec((nb, d))],
        out_specs=pl.BlockSpec((tm, d), lambda i: (i, 0)),
        out_shape=jax.ShapeDtypeStruct((s, d), F32),
        compiler_params=_cparams("parallel"),
        name="out_proj",
    )(x, mix, mem_out, w[:na], w[na:])


def _mem_kv_kernel(mem_ref, g_ref, w_ref, kn_ref, k_ref, v_ref):
    h = _rms(mem_ref[...], g_ref[...]).astype(BF16)
    kv = _dot(h, w_ref[...])
    width = k_ref.shape[1]
    for p in range(width // LANES):
        sl = slice(p * LANES, (p + 1) * LANES)
        k_ref[:, sl] = _head_rms(kv[:, sl], kn_ref[...])
    v_ref[...] = kv[:, width:]


def _mem_kv(mem, g, w, k_norm):
    m, d = mem.shape
    width = w.shape[1] // 2
    kn = jnp.tile(k_norm, LANES // HEAD).reshape(1, LANES)
    return pl.pallas_call(
        _mem_kv_kernel,
        grid=(1,),
        in_specs=[_const_spec((m, d)), _const_spec((1, d)), _const_spec((d, 2 * width)), _const_spec((1, LANES))],
        out_specs=[pl.BlockSpec((m, width), lambda i: (0, 0)), pl.BlockSpec((m, width), lambda i: (0, 0))],
        out_shape=[jax.ShapeDtypeStruct((m, width), F32)] * 2,
        compiler_params=_cparams("arbitrary"),
        name="mem_kv",
    )(mem, g.reshape(1, d), w.astype(BF16), kn)


def _mem_attn_kernel(q_ref, qn_ref, k_ref, v_ref, o_ref):
    lane = lax.broadcasted_iota(jnp.int32, (1, LANES), 1)
    width = q_ref.shape[1]
    for p in range(width // LANES):
        sl = slice(p * LANES, (p + 1) * LANES)
        q = _head_rms(q_ref[:, sl], qn_ref[...]) * (HEAD ** -0.5)
        k = k_ref[:, sl].astype(BF16)
        v = v_ref[:, sl].astype(BF16)
        out = None
        for h in range(LANES // HEAD):
            in_head = (lane // HEAD) == h
            qh = jnp.where(in_head, q, 0.0).astype(BF16)
            s = _dot_nt(qh, k)
            s = s - jnp.max(s, axis=-1, keepdims=True)
            e = jnp.exp(s)
            pr = e / jnp.sum(e, axis=-1, keepdims=True)
            oh = _dot(pr.astype(BF16), v)
            out = oh if out is None else jnp.where(in_head, oh, out)
        o_ref[:, sl] = out


def _mem_attn(proj, col_block, q_norm, k, v, *, tm=512):
    s = proj.shape[0]
    m, width = k.shape
    tm = min(tm, s)
    qn = jnp.tile(q_norm, LANES // HEAD).reshape(1, LANES)
    return pl.pallas_call(
        _mem_attn_kernel,
        grid=(s // tm,),
        in_specs=[pl.BlockSpec((tm, width), lambda i: (i, col_block)),
                  _const_spec((1, LANES)), _const_spec((m, width)), _const_spec((m, width))],
        out_specs=pl.BlockSpec((tm, width), lambda i: (i, 0)),
        out_shape=jax.ShapeDtypeStruct((s, width), F32),
        compiler_params=_cparams("parallel"),
        name="mem_attn",
    )(proj, qn, k, v)


def _stack_heads(z):
    lane = lax.broadcasted_iota(jnp.int32, z.shape, 2)
    zero = jnp.zeros_like(z)
    return jnp.concatenate([jnp.where(lane < HEAD, z, zero), jnp.where(lane >= HEAD, z, zero)], axis=1)


def _bdot(a, b):
    return lax.dot_general(a, b, (((2,), (1,)), ((0,), (0,))), preferred_element_type=F32)


def _bdot_nt(a, b):
    return lax.dot_general(a, b, (((2,), (2,)), ((0,), (0,))), preferred_element_type=F32)


def _bdot_tn(a, b):
    return lax.dot_general(a, b, (((1,), (1,)), ((0,), (0,))), preferred_element_type=F32)


def _rwkv_kernel(proj_ref, mu_ref, wlh_ref, wll_ref, g2_ref, w0_ref, a0_ref, kk_ref, ka_ref, rk_ref, lnw_ref, lnb_ref,
                 o_ref, last_ref, h_ref, r_s, k_s, v_s, a_s, lw_s, cum_s, g_s):
    c = CHUNK
    n_pairs = r_s.shape[0]
    width = n_pairs * LANES

    @pl.when(pl.program_id(0) == 0)
    def _():
        last_ref[...] = jnp.zeros_like(last_ref)
        h_ref[...] = jnp.zeros_like(h_ref)

    u = proj_ref[...]
    row = lax.broadcasted_iota(jnp.int32, u.shape, 0)
    prev = jnp.where(row == 0, last_ref[...], pltpu.roll(u, 1, 0))
    last_ref[...] = u[c - 1:c, :]
    u = u + (prev - u) * mu_ref[...]

    lora_in = u[:, 3 * width:3 * width + LANES]
    lane = lax.broadcasted_iota(jnp.int32, lora_in.shape, 1)
    lora_in = jnp.where(lane < HEAD, jnp.tanh(lora_in), lora_in)
    x_hi, x_lo = _split_bf16(lora_in)
    lo = _dot(x_hi, wlh_ref[...]) + (_dot(x_lo, wlh_ref[...]) + _dot(x_hi, wll_ref[...]))
    z = -(w0_ref[...] + lo[:, :width])
    softplus = jnp.maximum(z, 0.0) + jnp.log(1.0 + jnp.exp(-jnp.abs(z)))
    lw = -jnp.exp(-softplus - 0.5)
    a = jax.nn.sigmoid(a0_ref[...] + lo[:, width:])
    g = _dot(_b16(jax.nn.sigmoid(u[:, 3 * width + LANES:])), g2_ref[...])
    ti = lax.broadcasted_iota(jnp.int32, (c, c), 0)
    si = lax.broadcasted_iota(jnp.int32, (c, c), 1)
    tri = _b16(jnp.where(si <= ti, F32(1.0), F32(0.0)))
    lw_hi, lw_lo = _split_bf16(lw)
    cum = _dot(tri, lw_hi) + _dot(tri, lw_lo)

    for p in range(n_pairs):
        sl = slice(p * LANES, (p + 1) * LANES)
        r_s[p] = u[:, sl]
        k_s[p] = u[:, width + p * LANES:width + (p + 1) * LANES]
        v_s[p] = u[:, 2 * width + p * LANES:2 * width + (p + 1) * LANES]
        a_s[p] = a[:, sl]
        lw_s[p] = lw[:, sl]
        cum_s[p] = cum[:, sl]
        g_s[p] = g[:, sl]

    np_ = n_pairs
    hsum = _b16(_head_sum_matrix())

    def head_sum(z):
        return _dot(_b16(z).reshape(np_ * c, LANES), hsum).reshape(np_, c, LANES)

    col = lax.broadcasted_iota(jnp.int32, (np_, c, LANES), 2) % c
    trow = lax.broadcasted_iota(jnp.int32, (np_, c, LANES), 1)
    strict = col < trow
    incl = col <= trow
    bi = lax.broadcasted_iota(jnp.int32, (np_, LANES, LANES), 1)
    bj = lax.broadcasted_iota(jnp.int32, (np_, LANES, LANES), 2)
    eye = bi == bj
    same_head = (bi // HEAD) == (bj // HEAD)

    r, k, v, a_, lw_, cum_ = r_s[...], k_s[...], v_s[...], a_s[...], lw_s[...], cum_s[...]
    kk = k * kk_ref[...]
    kk = kk / jnp.maximum(jnp.sqrt(head_sum(kk * kk)), 1e-12)
    kmod = k * (1.0 + (a_ - 1.0) * ka_ref[...])
    b = kk * a_
    bonus = head_sum(r * kmod * rk_ref[...]) * v

    tot = cum_[:, c - 1:c, :]
    at = -kk * jnp.exp(cum_ - lw_)
    rt = r * jnp.exp(cum_)
    einv = jnp.exp(-cum_)
    eend = jnp.exp(tot - cum_)
    bh, kh = _b16(b * einv), _b16(kmod * einv)
    bt, kt = _b16(b * eend), _b16(kmod * eend)
    at_b = _b16(at)

    aa = _bdot_nt(jnp.concatenate([at_b, _b16(rt)], axis=1),
                  jnp.concatenate([_stack_heads(bh), _stack_heads(kh)], axis=1))
    a_ab = jnp.where(strict, aa[:, :c, :LANES], 0.0)
    a_ak = _b16(jnp.where(strict, aa[:, :c, LANES:], 0.0))
    a_rb = _b16(jnp.where(incl, aa[:, c:, :LANES], 0.0))
    a_rk = _b16(jnp.where(incl, aa[:, c:, LANES:], 0.0))

    n_blk = _stack_heads(a_ab)
    t_inv = jnp.where(eye, 1.0, 0.0) + jnp.where((bi == bj + 1) & (bi % 2 == 1), n_blk, 0.0)
    size = 2
    while size < c:
        e_mask = ((bi // (2 * size)) == (bj // (2 * size))) & (bi % (2 * size) >= size) & (bj % (2 * size) < size)
        t_b = _b16(t_inv)
        te = _bdot(t_b, _b16(jnp.where(e_mask, n_blk, 0.0)))
        t_inv = t_inv + _bdot(_b16(te), t_b)
        size *= 2
    t_pair = _b16(t_inv[:, :c] + t_inv[:, c:])

    v_b = _b16(v)
    v_st = _stack_heads(v_b)
    g0 = _bdot(a_ak, v_st)
    uw = _bdot(t_pair, jnp.concatenate([_stack_heads(_b16(g0)), _stack_heads(at_b)], axis=2))
    u0, w = _b16(uw[:, :, :LANES]), _b16(uw[:, :, LANES:])
    rw = rt + _bdot(a_rb, _stack_heads(w))
    y0 = _bdot(a_rb, _stack_heads(u0)) + _bdot(a_rk, v_st)
    m1 = jnp.where(eye, jnp.exp(tot), 0.0) + jnp.where(same_head, _bdot_tn(bt, w), 0.0)
    m0 = jnp.where(same_head, _bdot_tn(bt, u0) + _bdot_tn(kt, v_b), 0.0)

    h0 = _b16(h_ref[...])
    y = _bdot(_b16(rw), h0) + y0
    h_ref[...] = _bdot(_b16(m1), h0) + m0

    mean = head_sum(y) * (1.0 / HEAD)
    yc = y - mean
    var = head_sum(yc * yc) * (1.0 / HEAD)
    yn = yc * lax.rsqrt(var + GN_EPS) * lnw_ref[...] + lnb_ref[...]
    out = (yn + bonus) * g_s[...]
    for p in range(n_pairs):
        o_ref[:, p * LANES:(p + 1) * LANES] = out[p]


def _rwkv(proj, mu, w0, w2, a0, a2, g2, k_k, k_a, r_k, ln_w, ln_b):
    s = proj.shape[0]
    width = w0.shape[0]
    n_pairs = width // LANES
    shift_w = mu.shape[0]
    dl, al = w2.shape[0], a2.shape[0]
    assert dl + al == LANES and g2.shape[0] == LANES and shift_w == 3 * width + 2 * LANES
    wl = jnp.zeros((LANES, 2 * width), F32).at[:dl, :width].set(w2).at[dl:, width:].set(a2)
    wl_hi, wl_lo = _split_bf16(wl)

    def per_pair(t):
        return t.reshape(n_pairs, 1, LANES)

    c = CHUNK
    pair_spec = _const_spec((n_pairs, 1, LANES))
    return pl.pallas_call(
        _rwkv_kernel,
        grid=(s // c,),
        in_specs=[pl.BlockSpec((c, shift_w), lambda i: (i, 0)),
                  _const_spec((1, shift_w)), _const_spec((LANES, 2 * width)), _const_spec((LANES, 2 * width)),
                  _const_spec((LANES, width)),
                  _const_spec((1, width)), _const_spec((1, width)),
                  pair_spec, pair_spec, pair_spec, pair_spec, pair_spec],
        out_specs=pl.BlockSpec((c, width), lambda i: (i, 0)),
        out_shape=jax.ShapeDtypeStruct((s, width), F32),
        scratch_shapes=[pltpu.VMEM((1, shift_w), F32), pltpu.VMEM((n_pairs, LANES, LANES), F32)]
                       + [pltpu.VMEM((n_pairs, c, LANES), F32)] * 7,
        compiler_params=_cparams("arbitrary"),
        name="rwkv7",
    )(proj, mu.reshape(1, shift_w), wl_hi, wl_lo, g2.astype(BF16), w0.reshape(1, width), a0.reshape(1, width),
      per_pair(k_k), per_pair(k_a), per_pair(r_k.reshape(-1)), per_pair(ln_w), per_pair(ln_b))


def _shared_kv_kernel(x_ref, g_ref, wk_ref, wvt_ref, kn_ref, k_ref, vt_ref, km_ref):
    h = _rms(x_ref[...], g_ref[...]).astype(BF16)
    k = _dot(h, wk_ref[...])
    n_pairs = k_ref.shape[0]
    for p in range(n_pairs):
        kp = _head_rms(k[:, p * LANES:(p + 1) * LANES], kn_ref[...])
        k_ref[p, 0] = kp.astype(BF16)
        km_ref[0, :, p * LANES:(p + 1) * LANES] = jnp.mean(kp, axis=0, keepdims=True)
    vt = _dot_nt(wvt_ref[...], h)
    vt_ref[:, 0] = vt.reshape(n_pairs, LANES, vt.shape[1]).astype(BF16)


def _shared_kv(x, g, w_kv, k_norm):
    s, d = x.shape
    width = w_kv.shape[1] // 2
    n_pairs = width // LANES
    blk = MOBA_BLOCK
    nb = s // blk
    w = w_kv.astype(BF16)
    kn = jnp.tile(k_norm, LANES // HEAD).reshape(1, LANES)
    k, vt, km = pl.pallas_call(
        _shared_kv_kernel,
        grid=(nb,),
        in_specs=[pl.BlockSpec((blk, d), lambda i: (i, 0)), _const_spec((1, d)),
                  _const_spec((d, width)), _const_spec((width, d)), _const_spec((1, LANES))],
        out_specs=[pl.BlockSpec((n_pairs, 1, blk, LANES), lambda i: (0, i, 0, 0)),
                   pl.BlockSpec((n_pairs, 1, LANES, blk), lambda i: (0, i, 0, 0)),
                   pl.BlockSpec((1, 1, width), lambda i: (i, 0, 0))],
        out_shape=[jax.ShapeDtypeStruct((n_pairs, nb, blk, LANES), BF16),
                   jax.ShapeDtypeStruct((n_pairs, nb, LANES, blk), BF16),
                   jax.ShapeDtypeStruct((nb, 1, width), F32)],
        compiler_params=_cparams("parallel"),
        name="shared_kv",
    )(x, g.reshape(1, d), w[:, :width], w[:, width:].T, kn)
    k_bound = (HEAD ** 0.5) * jnp.max(jnp.abs(k_norm)).reshape(1)
    return k, vt, km.reshape(nb, width), k_bound


def _moba_kernel(slopes_ref, depth_ref, kbound_ref, q_ref, qn_ref, km_ref, k_ref, vt_ref, o_ref,
                 sel_ref, fac_ref, bias_ref, base_ref, sa_ref, sb_ref):
    p = pl.program_id(0)
    i = pl.program_id(1)
    blk = MOBA_BLOCK
    nb = km_ref.shape[0]
    q = _head_rms(q_ref[...], qn_ref[...]) * (HEAD ** -0.5)
    lane = lax.broadcasted_iota(jnp.int32, (1, LANES), 1)
    q2 = jnp.concatenate([jnp.where(lane < HEAD, q, 0.0), jnp.where(lane >= HEAD, q, 0.0)], axis=0)
    qb = (q2 * LOG2E).astype(BF16)
    col = lax.broadcasted_iota(jnp.int32, (1, 2 * blk), 1)
    slope = jnp.where(col < blk, slopes_ref[2 * p], slopes_ref[2 * p + 1]) * LOG2E

    n_idx = lax.broadcasted_iota(jnp.int32, (nb, 2 * blk), 0)
    gate = jnp.where(n_idx < i, _dot_nt(km_ref[...], q2, HI), NEG_INF)
    sel_bias = jnp.full(gate.shape, NEG_INF, F32)
    for _ in range(MOBA_TOPK):
        mx = jnp.max(gate, axis=0, keepdims=True)
        first = jnp.min(jnp.where(gate == mx, n_idx, nb), axis=0, keepdims=True)
        pick = (n_idx == first) & (mx > NEG_INF)
        sel_bias = jnp.where(pick, 0.0, sel_bias)
        gate = jnp.where(pick, NEG_INF, gate)
    sel_ref[...] = sel_bias - slope * ((i - n_idx) * blk).astype(F32)

    key_pos = lax.broadcasted_iota(jnp.int32, (blk, 2 * blk), 0)
    qry_pos = lax.broadcasted_iota(jnp.int32, (blk, 2 * blk), 1) % blk
    dist = (qry_pos - key_pos).astype(F32)
    bias_ref[...] = -slope * dist

    def weighted_values(n, e):
        eb = e.astype(BF16)
        return jnp.concatenate([_dot(vt_ref[n, :HEAD, :], eb[:, :blk]), _dot(vt_ref[n, HEAD:, :], eb[:, blk:])],
                               axis=1)

    s = jnp.where(dist >= 0.0, _dot_nt(k_ref[i], qb) + bias_ref[...], NEG_INF)
    m_own = jnp.max(s, axis=0, keepdims=True)
    e = jnp.exp2(s - m_own)
    l_own = jnp.sum(e, axis=0, keepdims=True)
    acc_own = weighted_values(i, e)

    pad = slope * (blk - 1)
    base_ref[...] = bias_ref[...] - pad - m_own
    fac_ref[...] = jnp.exp2(sel_ref[...] + pad)

    def block_of(g, j):
        return jnp.minimum(2 * g + j, nb - 1)

    def fast_scores(g, s_ref):
        for j in range(2):
            s_ref[j] = _dot_nt(k_ref[block_of(g, j)], qb)

    def fast_consume(g, s_ref, l, acc):
        for j in range(2):
            n = block_of(g, j)
            e = jnp.exp2(s_ref[j] + base_ref[...])
            f = fac_ref[pl.ds(n, 1), :]
            l = l + f * jnp.sum(e, axis=0, keepdims=True)
            acc = acc + f * weighted_values(n, e)
        return l, acc

    def fast_two_groups(u, carry):
        fast_scores(2 * u + 1, sb_ref)
        carry = fast_consume(2 * u, sa_ref, *carry)
        fast_scores(2 * u + 2, sa_ref)
        return fast_consume(2 * u + 1, sb_ref, *carry)

    q_len = jnp.sqrt(_dot_nt(jnp.ones((8, LANES), F32), q2 * q2, HI)[0:1])
    score_bound = q_len * (kbound_ref[0] * (LOG2E * BF16_SLACK))
    can_skip = jnp.max(score_bound - m_own) < EXP_HEADROOM
    first = jnp.where(can_skip, jnp.maximum(i - depth_ref[p] + 1, 0), 0)
    u_lo = first // 4

    fast_scores(2 * u_lo, sa_ref)
    l, acc = lax.fori_loop(u_lo, (i + 3) // 4, fast_two_groups, (l_own, acc_own))

    def emit(l, acc):
        out = acc / l
        o_ref[...] = jnp.concatenate([out[:, :blk], out[:, blk:]], axis=0).T

    emit(l, acc)
    not_finite = jnp.maximum(jnp.max(jnp.where(jnp.abs(l) < jnp.inf, 0.0, 1.0)),
                             jnp.max(jnp.where(jnp.abs(acc) < jnp.inf, 0.0, 1.0)))

    def score_group(g, s_ref):
        col_max = []
        for j in range(2):
            n = block_of(g, j)
            s = _dot_nt(k_ref[n], qb) + bias_ref[...]
            s_ref[j] = s
            col_max.append(jnp.max(s, axis=0, keepdims=True) + sel_ref[pl.ds(n, 1), :])
        return col_max

    def consume_group(g, s_ref, col_max, m, l, acc):
        n0, n1 = block_of(g, 0), block_of(g, 1)
        m_new = jnp.maximum(m, jnp.maximum(col_max[0], col_max[1]))
        alpha = jnp.exp2(m - m_new)
        l = alpha * l
        acc = alpha * acc
        for j, n in enumerate((n0, n1)):
            e = jnp.exp2(s_ref[j] - (m_new - sel_ref[pl.ds(n, 1), :]))
            l = l + jnp.sum(e, axis=0, keepdims=True)
            acc = acc + weighted_values(n, e)
        return m_new, l, acc

    @pl.when(not_finite > 0.0)
    def _():
        max_a = score_group(0, sa_ref)

        def two_groups(u, carry):
            max_a0, max_a1, m, l, acc = carry
            max_b = score_group(2 * u + 1, sb_ref)
            m, l, acc = consume_group(2 * u, sa_ref, (max_a0, max_a1), m, l, acc)
            max_a = score_group(2 * u + 2, sa_ref)
            m, l, acc = consume_group(2 * u + 1, sb_ref, max_b, m, l, acc)
            return max_a[0], max_a[1], m, l, acc

        init = (max_a[0], max_a[1], m_own, l_own, acc_own)
        _, _, _, l, acc = lax.fori_loop(0, (i + 3) // 4, two_groups, init)
        emit(l, acc)


def _moba(proj, q_norm, k, vt, km, k_bound):
    s = proj.shape[0]
    n_pairs, nb, blk, _ = k.shape
    heads = n_pairs * (LANES // HEAD)
    slopes = jnp.exp2(-ALIBI_MAX * jnp.arange(1, heads + 1, dtype=F32) / heads)
    min_slope = [2.0 ** (-ALIBI_MAX * (2 * p + 2) / heads) for p in range(n_pairs)]
    depth = jnp.asarray([math.ceil((DEAD_EXPONENT / (sl * LOG2E) + blk - 1) / blk) for sl in min_slope], jnp.int32)
    qn = jnp.tile(q_norm, LANES // HEAD).reshape(1, LANES)
    smem = pl.BlockSpec(memory_space=pltpu.SMEM)
    return pl.pallas_call(
        _moba_kernel,
        grid=(n_pairs, nb),
        in_specs=[smem, smem, smem,
                  pl.BlockSpec((blk, LANES), lambda p, i: (i, p)),
                  pl.BlockSpec((1, LANES), lambda p, i: (0, 0)),
                  pl.BlockSpec((nb, LANES), lambda p, i: (0, p)),
                  pl.BlockSpec((None, nb, blk, LANES), lambda p, i: (p, 0, 0, 0)),
                  pl.BlockSpec((None, nb, LANES, blk), lambda p, i: (p, 0, 0, 0))],
        out_specs=pl.BlockSpec((blk, LANES), lambda p, i: (i, p)),
        out_shape=jax.ShapeDtypeStruct((s, n_pairs * LANES), F32),
        scratch_shapes=[pltpu.VMEM((nb, 2 * blk), F32), pltpu.VMEM((nb, 2 * blk), F32),
                        pltpu.VMEM((blk, 2 * blk), F32), pltpu.VMEM((blk, 2 * blk), F32),
                        pltpu.VMEM((2, blk, 2 * blk), F32), pltpu.VMEM((2, blk, 2 * blk), F32)],
        compiler_params=_cparams("parallel", "arbitrary"),
        name="moba",
    )(slopes, depth, k_bound, proj, qn, km, k, vt)


def kernel(x, mem, ffn1_norm, ffn1_w1, ffn1_w3, ffn1_w2, mix_norm, w_out, mem_norm, w_mem_kv, mem_q_norm, mem_k_norm, ffn2_norm, ffn2_w1, ffn2_w3, ffn2_w2, rwkv_w_in, rwkv_mu, rwkv_w0, rwkv_w2, rwkv_a0, rwkv_a2, rwkv_g2, rwkv_k_k, rwkv_k_a, rwkv_r_k, rwkv_ln_w, rwkv_ln_b, kv_norm, w_kv, kv_k_norm, moba_w_in, moba_q_norm):
    batch = x.shape[0]
    depth = ffn1_norm.shape[0]
    n_a = rwkv_w_in.shape[0]
    mix_w = rwkv_w0.shape[1]
    mem_w = w_mem_kv.shape[2] // 2
    outs = []
    for bi in range(batch):
        xb, memb = x[bi], mem[bi]
        shared = None
        for l in range(depth):
            xb = _ffn(xb, ffn1_norm[l], ffn1_w1[l], ffn1_w3[l], ffn1_w2[l])
            if l < n_a:
                proj = _norm_proj(xb, mix_norm[l], rwkv_w_in[l])
                mix = _rwkv(proj, rwkv_mu[l], rwkv_w0[l], rwkv_w2[l], rwkv_a0[l], rwkv_a2[l], rwkv_g2[l],
                            rwkv_k_k[l], rwkv_k_a[l], rwkv_r_k[l], rwkv_ln_w[l], rwkv_ln_b[l])
                mem_col = rwkv_mu.shape[1] // mem_w
            else:
                j = l - n_a
                proj = _norm_proj(xb, mix_norm[l], moba_w_in[j])
                mix = _moba(proj, moba_q_norm[j], *shared)
                mem_col = mix_w // mem_w
            mk, mv = _mem_kv(memb, mem_norm[l], w_mem_kv[l], mem_k_norm[l])
            mem_out = _mem_attn(proj, mem_col, mem_q_norm[l], mk, mv)
            xb = _out_proj(xb, mix, mem_out, w_out[l])
            xb = _ffn(xb, ffn2_norm[l], ffn2_w1[l], ffn2_w3[l], ffn2_w2[l])
            if l == n_a - 1:
                shared = _shared_kv(xb, kv_norm, w_kv, kv_k_norm)
        outs.append(xb)
    return jnp.stack(outs)
```

```python
import functools
import math

import jax
import jax.numpy as jnp
from jax import lax
from jax.experimental import pallas as pl
from jax.experimental.pallas import tpu as pltpu

F32 = jnp.float32
BF16 = jnp.bfloat16
HI = lax.Precision.HIGHEST

HEAD = 64
LANES = 128
MXU = 256
CHUNK = 64
MOBA_BLOCK = 256
MOBA_TOPK = 3
ALIBI_MAX = 8.0
NORM_EPS = 1e-6
GN_EPS = 64e-5
VMEM_LIMIT = 56 * 1024 * 1024
NEG_INF = float("-inf")
LOG2E = 1.4426950408889634
DEAD_EXPONENT = 160.0
EXP_HEADROOM = 100.0
BF16_SLACK = 1.02


def _cparams(*sem):
    return pltpu.CompilerParams(dimension_semantics=sem, vmem_limit_bytes=VMEM_LIMIT)


def _const_spec(shape):
    return pl.BlockSpec(shape, lambda *_: (0,) * len(shape), pipeline_mode=pl.Buffered(1))


def _rms(x, g):
    return x * lax.rsqrt(jnp.mean(x * x, axis=-1, keepdims=True) + NORM_EPS) * g


def _b16(x):
    return x.astype(BF16)


def _split_bf16(x):
    hi = x.astype(BF16)
    return hi, (x - hi.astype(F32)).astype(BF16)


def _dot(a, b, precision=None):
    return jnp.dot(a, b, preferred_element_type=F32, precision=precision)


def _dot_nt(a, b, precision=None):
    return lax.dot_general(a, b, (((1,), (1,)), ((), ())), preferred_element_type=F32, precision=precision)


def _dot_tn(a, b, precision=None):
    return lax.dot_general(a, b, (((0,), (0,)), ((), ())), preferred_element_type=F32, precision=precision)


def _head_sum_matrix(scale=1.0):
    i = lax.broadcasted_iota(jnp.int32, (LANES, LANES), 0) // HEAD
    j = lax.broadcasted_iota(jnp.int32, (LANES, LANES), 1) // HEAD
    return jnp.where(i == j, F32(scale), F32(0.0))


def _head_rms(x, g):
    hi, lo = _split_bf16(x * x)
    mean_matrix = _b16(_head_sum_matrix(1.0 / HEAD))
    ms = _dot(hi, mean_matrix) + _dot(lo, mean_matrix)
    return x * lax.rsqrt(ms + NORM_EPS) * g


def _ffn_kernel(x_ref, g_ref, w1_ref, w3_ref, w2_ref, o_ref, acc_ref, *, f_chunk):
    x = x_ref[...]
    h = _rms(x, g_ref[...]).astype(BF16)
    n_chunks = w1_ref.shape[1] // f_chunk
    for c in range(n_chunks):
        sl = slice(c * f_chunk, (c + 1) * f_chunk)
        a = _dot(h, w1_ref[:, sl])
        b = _dot(h, w3_ref[:, sl])
        act = (a * jax.nn.sigmoid(a) * b).astype(BF16)
        part = _dot(act, w2_ref[sl, :])
        if c == 0:
            acc_ref[...] = part
        else:
            acc_ref[...] += part
    o_ref[...] = x + 0.5 * acc_ref[...]


def _ffn(x, g, w1, w3, w2, *, tm=512, f_chunk=MXU):
    s, d = x.shape
    f = w1.shape[1]
    fp = -(-f // f_chunk) * f_chunk
    w1 = jnp.pad(w1.astype(BF16), ((0, 0), (0, fp - f)))
    w3 = jnp.pad(w3.astype(BF16), ((0, 0), (0, fp - f)))
    w2 = jnp.pad(w2.astype(BF16), ((0, fp - f), (0, 0)))
    tm = min(tm, s)
    return pl.pallas_call(
        functools.partial(_ffn_kernel, f_chunk=f_chunk),
        grid=(s // tm,),
        in_specs=[pl.BlockSpec((tm, d), lambda i: (i, 0)),
                  _const_spec((1, d)), _const_spec((d, fp)), _const_spec((d, fp)), _const_spec((fp, d))],
        out_specs=pl.BlockSpec((tm, d), lambda i: (i, 0)),
        out_shape=jax.ShapeDtypeStruct((s, d), F32),
        scratch_shapes=[pltpu.VMEM((tm, d), F32)],
        compiler_params=_cparams("parallel"),
        name="ffn",
    )(x, g.reshape(1, d), w1, w3, w2)


def _norm_proj_kernel(x_ref, g_ref, w_ref, o_ref):
    h = _rms(x_ref[...], g_ref[...]).astype(BF16)
    o_ref[...] = _dot(h, w_ref[...])


def _norm_proj(x, g, w, *, tm=512):
    s, d = x.shape
    n = w.shape[1]
    tm = min(tm, s)
    return pl.pallas_call(
        _norm_proj_kernel,
        grid=(s // tm,),
        in_specs=[pl.BlockSpec((tm, d), lambda i: (i, 0)), _const_spec((1, d)), _const_spec((d, n))],
        out_specs=pl.BlockSpec((tm, n), lambda i: (i, 0)),
        out_shape=jax.ShapeDtypeStruct((s, n), F32),
        compiler_params=_cparams("parallel"),
        name="norm_proj",
    )(x, g.reshape(1, d), w.astype(BF16))


def _out_proj_kernel(x_ref, mix_ref, mem_ref, wa_ref, wb_ref, o_ref):
    o_ref[...] = (x_ref[...] + _dot(mix_ref[...].astype(BF16), wa_ref[...])
                  + _dot(mem_ref[...].astype(BF16), wb_ref[...]))


def _out_proj(x, mix, mem_out, w_out, *, tm=512):
    s, d = x.shape
    na, nb = mix.shape[1], mem_out.shape[1]
    tm = min(tm, s)
    w = w_out.astype(BF16)
    return pl.pallas_call(
        _out_proj_kernel,
        grid=(s // tm,),
        in_specs=[pl.BlockSpec((tm, d), lambda i: (i, 0)),
                  pl.BlockSpec((tm, na), lambda i: (i, 0)),
                  pl.BlockSpec((tm, nb), lambda i: (i, 0)),
                  _const_spec((na, d)), _const_spec((nb, d))],
        out_specs=pl.BlockSpec((tm, d), lambda i: (i, 0)),
        out_shape=jax.ShapeDtypeStruct((s, d), F32),
        compiler_params=_cparams("parallel"),
        name="out_proj",
    )(x, mix, mem_out, w[:na], w[na:])


def _mem_kv_kernel(mem_ref, g_ref, w_ref, kn_ref, k_ref, v_ref):
    h = _rms(mem_ref[...], g_ref[...]).astype(BF16)
    kv = _dot(h, w_ref[...])
    width = k_ref.shape[1]
    for p in range(width // LANES):
        sl = slice(p * LANES, (p + 1) * LANES)
        k_ref[:, sl] = _head_rms(kv[:, sl], kn_ref[...])
    v_ref[...] = kv[:, width:]


def _mem_kv(mem, g, w, k_norm):
    m, d = mem.shape
    width = w.shape[1] // 2
    kn = jnp.tile(k_norm, LANES // HEAD).reshape(1, LANES)
    return pl.pallas_call(
        _mem_kv_kernel,
        grid=(1,),
        in_specs=[_const_spec((m, d)), _const_spec((1, d)), _const_spec((d, 2 * width)), _const_spec((1, LANES))],
        out_specs=[pl.BlockSpec((m, width), lambda i: (0, 0)), pl.BlockSpec((m, width), lambda i: (0, 0))],
        out_shape=[jax.ShapeDtypeStruct((m, width), F32)] * 2,
        compiler_params=_cparams("arbitrary"),
        name="mem_kv",
    )(mem, g.reshape(1, d), w.astype(BF16), kn)


def _mem_attn_kernel(q_ref, qn_ref, k_ref, v_ref, o_ref):
    lane = lax.broadcasted_iota(jnp.int32, (1, LANES), 1)
    width = q_ref.shape[1]
    for p in range(width // LANES):
        sl = slice(p * LANES, (p + 1) * LANES)
        q = _head_rms(q_ref[:, sl], qn_ref[...]) * (HEAD ** -0.5)
        k = k_ref[:, sl].astype(BF16)
        v = v_ref[:, sl].astype(BF16)
        out = None
        for h in range(LANES // HEAD):
            in_head = (lane // HEAD) == h
            qh = jnp.where(in_head, q, 0.0).astype(BF16)
            s = _dot_nt(qh, k)
            s = s - jnp.max(s, axis=-1, keepdims=True)
            e = jnp.exp(s)
            pr = e / jnp.sum(e, axis=-1, keepdims=True)
            oh = _dot(pr.astype(BF16), v)
            out = oh if out is None else jnp.where(in_head, oh, out)
        o_ref[:, sl] = out


def _mem_attn(proj, col_block, q_norm, k, v, *, tm=512):
    s = proj.shape[0]
    m, width = k.shape
    tm = min(tm, s)
    qn = jnp.tile(q_norm, LANES // HEAD).reshape(1, LANES)
    return pl.pallas_call(
        _mem_attn_kernel,
        grid=(s // tm,),
        in_specs=[pl.BlockSpec((tm, width), lambda i: (i, col_block)),
                  _const_spec((1, LANES)), _const_spec((m, width)), _const_spec((m, width))],
        out_specs=pl.BlockSpec((tm, width), lambda i: (i, 0)),
        out_shape=jax.ShapeDtypeStruct((s, width), F32),
        compiler_params=_cparams("parallel"),
        name="mem_attn",
    )(proj, qn, k, v)


def _stack_heads(z):
    lane = lax.broadcasted_iota(jnp.int32, z.shape, 2)
    zero = jnp.zeros_like(z)
    return jnp.concatenate([jnp.where(lane < HEAD, z, zero), jnp.where(lane >= HEAD, z, zero)], axis=1)


def _bdot(a, b):
    return lax.dot_general(a, b, (((2,), (1,)), ((0,), (0,))), preferred_element_type=F32)


def _bdot_nt(a, b):
    return lax.dot_general(a, b, (((2,), (2,)), ((0,), (0,))), preferred_element_type=F32)


def _bdot_tn(a, b):
    return lax.dot_general(a, b, (((1,), (1,)), ((0,), (0,))), preferred_element_type=F32)


def _rwkv_kernel(proj_ref, mu_ref, wlh_ref, wll_ref, g2_ref, w0_ref, a0_ref, kk_ref, ka_ref, rk_ref, lnw_ref, lnb_ref,
                 o_ref, last_ref, h_ref, r_s, k_s, v_s, a_s, lw_s, cum_s, g_s):
    c = CHUNK
    n_pairs = r_s.shape[0]
    width = n_pairs * LANES

    @pl.when(pl.program_id(0) == 0)
    def _():
        last_ref[...] = jnp.zeros_like(last_ref)
        h_ref[...] = jnp.zeros_like(h_ref)

    u = proj_ref[...]
    row = lax.broadcasted_iota(jnp.int32, u.shape, 0)
    prev = jnp.where(row == 0, last_ref[...], pltpu.roll(u, 1, 0))
    last_ref[...] = u[c - 1:c, :]
    u = u + (prev - u) * mu_ref[...]

    lora_in = u[:, 3 * width:3 * width + LANES]
    lane = lax.broadcasted_iota(jnp.int32, lora_in.shape, 1)
    lora_in = jnp.where(lane < HEAD, jnp.tanh(lora_in), lora_in)
    x_hi, x_lo = _split_bf16(lora_in)
    lo = _dot(x_hi, wlh_ref[...]) + (_dot(x_lo, wlh_ref[...]) + _dot(x_hi, wll_ref[...]))
    z = -(w0_ref[...] + lo[:, :width])
    softplus = jnp.maximum(z, 0.0) + jnp.log(1.0 + jnp.exp(-jnp.abs(z)))
    lw = -jnp.exp(-softplus - 0.5)
    a = jax.nn.sigmoid(a0_ref[...] + lo[:, width:])
    g = _dot(_b16(jax.nn.sigmoid(u[:, 3 * width + LANES:])), g2_ref[...])
    ti = lax.broadcasted_iota(jnp.int32, (c, c), 0)
    si = lax.broadcasted_iota(jnp.int32, (c, c), 1)
    tri = _b16(jnp.where(si <= ti, F32(1.0), F32(0.0)))
    lw_hi, lw_lo = _split_bf16(lw)
    cum = _dot(tri, lw_hi) + _dot(tri, lw_lo)

    for p in range(n_pairs):
        sl = slice(p * LANES, (p + 1) * LANES)
        r_s[p] = u[:, sl]
        k_s[p] = u[:, width + p * LANES:width + (p + 1) * LANES]
        v_s[p] = u[:, 2 * width + p * LANES:2 * width + (p + 1) * LANES]
        a_s[p] = a[:, sl]
        lw_s[p] = lw[:, sl]
        cum_s[p] = cum[:, sl]
        g_s[p] = g[:, sl]

    np_ = n_pairs
    hsum = _b16(_head_sum_matrix())

    def head_sum(z):
        return _dot(_b16(z).reshape(np_ * c, LANES), hsum).reshape(np_, c, LANES)

    col = lax.broadcasted_iota(jnp.int32, (np_, c, LANES), 2) % c
    trow = lax.broadcasted_iota(jnp.int32, (np_, c, LANES), 1)
    strict = col < trow
    incl = col <= trow
    bi = lax.broadcasted_iota(jnp.int32, (np_, LANES, LANES), 1)
    bj = lax.broadcasted_iota(jnp.int32, (np_, LANES, LANES), 2)
    eye = bi == bj
    same_head = (bi // HEAD) == (bj // HEAD)

    r, k, v, a_, lw_, cum_ = r_s[...], k_s[...], v_s[...], a_s[...], lw_s[...], cum_s[...]
    kk = k * kk_ref[...]
    kk = kk / jnp.maximum(jnp.sqrt(head_sum(kk * kk)), 1e-12)
    kmod = k * (1.0 + (a_ - 1.0) * ka_ref[...])
    b = kk * a_
    bonus = head_sum(r * kmod * rk_ref[...]) * v

    tot = cum_[:, c - 1:c, :]
    at = -kk * jnp.exp(cum_ - lw_)
    rt = r * jnp.exp(cum_)
    einv = jnp.exp(-cum_)
    eend = jnp.exp(tot - cum_)
    bh, kh = _b16(b * einv), _b16(kmod * einv)
    bt, kt = _b16(b * eend), _b16(kmod * eend)
    at_b = _b16(at)

    aa = _bdot_nt(jnp.concatenate([at_b, _b16(rt)], axis=1),
                  jnp.concatenate([_stack_heads(bh), _stack_heads(kh)], axis=1))
    a_ab = jnp.where(strict, aa[:, :c, :LANES], 0.0)
    a_ak = _b16(jnp.where(strict, aa[:, :c, LANES:], 0.0))
    a_rb = _b16(jnp.where(incl, aa[:, c:, :LANES], 0.0))
    a_rk = _b16(jnp.where(incl, aa[:, c:, LANES:], 0.0))

    n_blk = _stack_heads(a_ab)
    t_inv = jnp.where(eye, 1.0, 0.0) + jnp.where((bi == bj + 1) & (bi % 2 == 1), n_blk, 0.0)
    size = 2
    while size < c:
        e_mask = ((bi // (2 * size)) == (bj // (2 * size))) & (bi % (2 * size) >= size) & (bj % (2 * size) < size)
        t_b = _b16(t_inv)
        te = _bdot(t_b, _b16(jnp.where(e_mask, n_blk, 0.0)))
        t_inv = t_inv + _bdot(_b16(te), t_b)
        size *= 2
    t_pair = _b16(t_inv[:, :c] + t_inv[:, c:])

    v_b = _b16(v)
    v_st = _stack_heads(v_b)
    g0 = _bdot(a_ak, v_st)
    uw = _bdot(t_pair, jnp.concatenate([_stack_heads(_b16(g0)), _stack_heads(at_b)], axis=2))
    u0, w = _b16(uw[:, :, :LANES]), _b16(uw[:, :, LANES:])
    rw = rt + _bdot(a_rb, _stack_heads(w))
    y0 = _bdot(a_rb, _stack_heads(u0)) + _bdot(a_rk, v_st)
    m1 = jnp.where(eye, jnp.exp(tot), 0.0) + jnp.where(same_head, _bdot_tn(bt, w), 0.0)
    m0 = jnp.where(same_head, _bdot_tn(bt, u0) + _bdot_tn(kt, v_b), 0.0)

    h0 = _b16(h_ref[...])
    y = _bdot(_b16(rw), h0) + y0
    h_ref[...] = _bdot(_b16(m1), h0) + m0

    mean = head_sum(y) * (1.0 / HEAD)
    yc = y - mean
    var = head_sum(yc * yc) * (1.0 / HEAD)
    yn = yc * lax.rsqrt(var + GN_EPS) * lnw_ref[...] + lnb_ref[...]
    out = (yn + bonus) * g_s[...]
    for p in range(n_pairs):
        o_ref[:, p * LANES:(p + 1) * LANES] = out[p]


def _rwkv(proj, mu, w0, w2, a0, a2, g2, k_k, k_a, r_k, ln_w, ln_b):
    s = proj.shape[0]
    width = w0.shape[0]
    n_pairs = width // LANES
    shift_w = mu.shape[0]
    dl, al = w2.shape[0], a2.shape[0]
    assert dl + al == LANES and g2.shape[0] == LANES and shift_w == 3 * width + 2 * LANES
    wl = jnp.zeros((LANES, 2 * width), F32).at[:dl, :width].set(w2).at[dl:, width:].set(a2)
    wl_hi, wl_lo = _split_bf16(wl)

    def per_pair(t):
        return t.reshape(n_pairs, 1, LANES)

    c = CHUNK
    pair_spec = _const_spec((n_pairs, 1, LANES))
    return pl.pallas_call(
        _rwkv_kernel,
        grid=(s // c,),
        in_specs=[pl.BlockSpec((c, shift_w), lambda i: (i, 0)),
                  _const_spec((1, shift_w)), _const_spec((LANES, 2 * width)), _const_spec((LANES, 2 * width)),
                  _const_spec((LANES, width)),
                  _const_spec((1, width)), _const_spec((1, width)),
                  pair_spec, pair_spec, pair_spec, pair_spec, pair_spec],
        out_specs=pl.BlockSpec((c, width), lambda i: (i, 0)),
        out_shape=jax.ShapeDtypeStruct((s, width), F32),
        scratch_shapes=[pltpu.VMEM((1, shift_w), F32), pltpu.VMEM((n_pairs, LANES, LANES), F32)]
                       + [pltpu.VMEM((n_pairs, c, LANES), F32)] * 7,
        compiler_params=_cparams("arbitrary"),
        name="rwkv7",
    )(proj, mu.reshape(1, shift_w), wl_hi, wl_lo, g2.astype(BF16), w0.reshape(1, width), a0.reshape(1, width),
      per_pair(k_k), per_pair(k_a), per_pair(r_k.reshape(-1)), per_pair(ln_w), per_pair(ln_b))


def _shared_kv_kernel(x_ref, g_ref, wk_ref, wvt_ref, kn_ref, k_ref, vt_ref, km_ref):
    h = _rms(x_ref[...], g_ref[...]).astype(BF16)
    k = _dot(h, wk_ref[...])
    n_pairs = k_ref.shape[0]
    for p in range(n_pairs):
        kp = _head_rms(k[:, p * LANES:(p + 1) * LANES], kn_ref[...])
        k_ref[p, 0] = kp.astype(BF16)
        km_ref[0, :, p * LANES:(p + 1) * LANES] = jnp.mean(kp, axis=0, keepdims=True)
    vt = _dot_nt(wvt_ref[...], h)
    vt_ref[:, 0] = vt.reshape(n_pairs, LANES, vt.shape[1]).astype(BF16)


def _shared_kv(x, g, w_kv, k_norm):
    s, d = x.shape
    width = w_kv.shape[1] // 2
    n_pairs = width // LANES
    blk = MOBA_BLOCK
    nb = s // blk
    w = w_kv.astype(BF16)
    kn = jnp.tile(k_norm, LANES // HEAD).reshape(1, LANES)
    k, vt, km = pl.pallas_call(
        _shared_kv_kernel,
        grid=(nb,),
        in_specs=[pl.BlockSpec((blk, d), lambda i: (i, 0)), _const_spec((1, d)),
                  _const_spec((d, width)), _const_spec((width, d)), _const_spec((1, LANES))],
        out_specs=[pl.BlockSpec((n_pairs, 1, blk, LANES), lambda i: (0, i, 0, 0)),
                   pl.BlockSpec((n_pairs, 1, LANES, blk), lambda i: (0, i, 0, 0)),
                   pl.BlockSpec((1, 1, width), lambda i: (i, 0, 0))],
        out_shape=[jax.ShapeDtypeStruct((n_pairs, nb, blk, LANES), BF16),
                   jax.ShapeDtypeStruct((n_pairs, nb, LANES, blk), BF16),
                   jax.ShapeDtypeStruct((nb, 1, width), F32)],
        compiler_params=_cparams("parallel"),
        name="shared_kv",
    )(x, g.reshape(1, d), w[:, :width], w[:, width:].T, kn)
    k_bound = (HEAD ** 0.5) * jnp.max(jnp.abs(k_norm)).reshape(1)
    return k, vt, km.reshape(nb, width), k_bound


def _moba_kernel(slopes_ref, depth_ref, kbound_ref, q_ref, qn_ref, km_ref, k_ref, vt_ref, o_ref,
                 sel_ref, fac_ref, bias_ref, base_ref, sa_ref, sb_ref):
    p = pl.program_id(0)
    i = pl.program_id(1)
    blk = MOBA_BLOCK
    nb = km_ref.shape[0]
    q = _head_rms(q_ref[...], qn_ref[...]) * (HEAD ** -0.5)
    lane = lax.broadcasted_iota(jnp.int32, (1, LANES), 1)
    q2 = jnp.concatenate([jnp.where(lane < HEAD, q, 0.0), jnp.where(lane >= HEAD, q, 0.0)], axis=0)
    qb = (q2 * LOG2E).astype(BF16)
    col = lax.broadcasted_iota(jnp.int32, (1, 2 * blk), 1)
    slope = jnp.where(col < blk, slopes_ref[2 * p], slopes_ref[2 * p + 1]) * LOG2E

    start = jnp.maximum(i - depth_ref[p] + 1, 0)
    n_iter = (i - start + 3) // 4

    def block_of(g, j):
        return jnp.minimum(start + 2 * g + j, nb - 1)

    def fast_scores(g, s_ref):
        for j in range(2):
            s_ref[j] = _dot_nt(k_ref[block_of(g, j)], qb)

    fast_scores(0, sa_ref)

    n_idx = lax.broadcasted_iota(jnp.int32, (nb, 2 * blk), 0)
    q_hi, q_lo = _split_bf16(q2)
    km_hi, km_lo = _split_bf16(km_ref[...])
    gate = _dot_nt(km_hi, q_hi) + (_dot_nt(km_lo, q_hi) + _dot_nt(km_hi, q_lo))
    gate = jnp.where(n_idx < i, gate, NEG_INF)
    sel_bias = jnp.full(gate.shape, NEG_INF, F32)
    for _ in range(MOBA_TOPK):
        mx = jnp.max(gate, axis=0, keepdims=True)
        first = jnp.min(jnp.where(gate == mx, n_idx, nb), axis=0, keepdims=True)
        pick = (n_idx == first) & (mx > NEG_INF)
        sel_bias = jnp.where(pick, 0.0, sel_bias)
        gate = jnp.where(pick, NEG_INF, gate)
    sel_ref[...] = sel_bias - slope * ((i - n_idx) * blk).astype(F32)

    key_pos = lax.broadcasted_iota(jnp.int32, (blk, 2 * blk), 0)
    qry_pos = lax.broadcasted_iota(jnp.int32, (blk, 2 * blk), 1) % blk
    dist = (qry_pos - key_pos).astype(F32)
    bias_ref[...] = -slope * dist

    def weighted_values(n, e):
        eb = e.astype(BF16)
        return jnp.concatenate([_dot(vt_ref[n, :HEAD, :], eb[:, :blk]), _dot(vt_ref[n, HEAD:, :], eb[:, blk:])],
                               axis=1)

    s = jnp.where(dist >= 0.0, _dot_nt(k_ref[i], qb) + bias_ref[...], NEG_INF)
    m_own = jnp.max(s, axis=0, keepdims=True)
    e = jnp.exp2(s - m_own)
    l_own = jnp.sum(e, axis=0, keepdims=True)
    acc_own = weighted_values(i, e)

    pad = slope * (blk - 1)
    base_ref[...] = bias_ref[...] - pad - m_own
    fac_ref[...] = jnp.exp2(sel_ref[...] + pad)

    def fast_consume(g, s_ref, l, acc):
        for j in range(2):
            n = block_of(g, j)
            e = jnp.exp2(s_ref[j] + base_ref[...])
            f = fac_ref[pl.ds(n, 1), :]
            l = l + f * jnp.sum(e, axis=0, keepdims=True)
            acc = acc + f * weighted_values(n, e)
        return l, acc

    def fast_two_groups(u, carry):
        fast_scores(2 * u + 1, sb_ref)
        carry = fast_consume(2 * u, sa_ref, *carry)
        fast_scores(2 * u + 2, sa_ref)
        return fast_consume(2 * u + 1, sb_ref, *carry)

    l, acc = lax.fori_loop(0, n_iter, fast_two_groups, (l_own, acc_own))

    def emit(l, acc):
        out = acc / l
        o_ref[...] = jnp.concatenate([out[:, :blk], out[:, blk:]], axis=0).T

    emit(l, acc)

    q_len = jnp.sqrt(_dot_nt(jnp.ones((8, LANES), BF16), _b16(q2 * q2))[0:1])
    score_bound = q_len * (kbound_ref[0] * (LOG2E * BF16_SLACK))
    redo = jnp.maximum(jnp.max(jnp.where(jnp.abs(l) < jnp.inf, 0.0, 1.0)),
                       jnp.max(jnp.where(jnp.abs(acc) < jnp.inf, 0.0, 1.0)))
    redo = jnp.maximum(redo, jnp.max(jnp.where(score_bound - m_own < EXP_HEADROOM, 0.0, 1.0)))

    def any_block_of(g, j):
        return jnp.minimum(2 * g + j, nb - 1)

    def score_group(g, s_ref):
        col_max = []
        for j in range(2):
            n = any_block_of(g, j)
            s = _dot_nt(k_ref[n], qb) + bias_ref[...]
            s_ref[j] = s
            col_max.append(jnp.max(s, axis=0, keepdims=True) + sel_ref[pl.ds(n, 1), :])
        return col_max

    def consume_group(g, s_ref, col_max, m, l, acc):
        n0, n1 = any_block_of(g, 0), any_block_of(g, 1)
        m_new = jnp.maximum(m, jnp.maximum(col_max[0], col_max[1]))
        alpha = jnp.exp2(m - m_new)
        l = alpha * l
        acc = alpha * acc
        for j, n in enumerate((n0, n1)):
            e = jnp.exp2(s_ref[j] - (m_new - sel_ref[pl.ds(n, 1), :]))
            l = l + jnp.sum(e, axis=0, keepdims=True)
            acc = acc + weighted_values(n, e)
        return m_new, l, acc

    @pl.when(redo > 0.0)
    def _():
        max_a = score_group(0, sa_ref)

        def two_groups(u, carry):
            max_a0, max_a1, m, l, acc = carry
            max_b = score_group(2 * u + 1, sb_ref)
            m, l, acc = consume_group(2 * u, sa_ref, (max_a0, max_a1), m, l, acc)
            max_a = score_group(2 * u + 2, sa_ref)
            m, l, acc = consume_group(2 * u + 1, sb_ref, max_b, m, l, acc)
            return max_a[0], max_a[1], m, l, acc

        init = (max_a[0], max_a[1], m_own, l_own, acc_own)
        _, _, _, l, acc = lax.fori_loop(0, (i + 3) // 4, two_groups, init)
        emit(l, acc)


def _moba(proj, q_norm, k, vt, km, k_bound):
    s = proj.shape[0]
    n_pairs, nb, blk, _ = k.shape
    heads = n_pairs * (LANES // HEAD)
    slopes = jnp.exp2(-ALIBI_MAX * jnp.arange(1, heads + 1, dtype=F32) / heads)
    min_slope = [2.0 ** (-ALIBI_MAX * (2 * p + 2) / heads) for p in range(n_pairs)]
    depth = jnp.asarray([math.ceil((DEAD_EXPONENT / (sl * LOG2E) + blk - 1) / blk) for sl in min_slope], jnp.int32)
    qn = jnp.tile(q_norm, LANES // HEAD).reshape(1, LANES)
    smem = pl.BlockSpec(memory_space=pltpu.SMEM)
    return pl.pallas_call(
        _moba_kernel,
        grid=(n_pairs, nb),
        in_specs=[smem, smem, smem,
                  pl.BlockSpec((blk, LANES), lambda p, i: (i, p)),
                  pl.BlockSpec((1, LANES), lambda p, i: (0, 0)),
                  pl.BlockSpec((nb, LANES), lambda p, i: (0, p)),
                  pl.BlockSpec((None, nb, blk, LANES), lambda p, i: (p, 0, 0, 0)),
                  pl.BlockSpec((None, nb, LANES, blk), lambda p, i: (p, 0, 0, 0))],
        out_specs=pl.BlockSpec((blk, LANES), lambda p, i: (i, p)),
        out_shape=jax.ShapeDtypeStruct((s, n_pairs * LANES), F32),
        scratch_shapes=[pltpu.VMEM((nb, 2 * blk), F32), pltpu.VMEM((nb, 2 * blk), F32),
                        pltpu.VMEM((blk, 2 * blk), F32), pltpu.VMEM((blk, 2 * blk), F32),
                        pltpu.VMEM((2, blk, 2 * blk), F32), pltpu.VMEM((2, blk, 2 * blk), F32)],
        compiler_params=_cparams("parallel", "arbitrary"),
        name="moba",
    )(slopes, depth, k_bound, proj, qn, km, k, vt)


def kernel(x, mem, ffn1_norm, ffn1_w1, ffn1_w3, ffn1_w2, mix_norm, w_out, mem_norm, w_mem_kv, mem_q_norm, mem_k_norm, ffn2_norm, ffn2_w1, ffn2_w3, ffn2_w2, rwkv_w_in, rwkv_mu, rwkv_w0, rwkv_w2, rwkv_a0, rwkv_a2, rwkv_g2, rwkv_k_k, rwkv_k_a, rwkv_r_k, rwkv_ln_w, rwkv_ln_b, kv_norm, w_kv, kv_k_norm, moba_w_in, moba_q_norm):
    batch = x.shape[0]
    depth = ffn1_norm.shape[0]
    n_a = rwkv_w_in.shape[0]
    mix_w = rwkv_w0.shape[1]
    mem_w = w_mem_kv.shape[2] // 2
    outs = []
    for bi in range(batch):
        xb, memb = x[bi], mem[bi]
        shared = None
        for l in range(depth):
            xb = _ffn(xb, ffn1_norm[l], ffn1_w1[l], ffn1_w3[l], ffn1_w2[l])
            if l < n_a:
                proj = _norm_proj(xb, mix_norm[l], rwkv_w_in[l])
                mix = _rwkv(proj, rwkv_mu[l], rwkv_w0[l], rwkv_w2[l], rwkv_a0[l], rwkv_a2[l], rwkv_g2[l],
                            rwkv_k_k[l], rwkv_k_a[l], rwkv_r_k[l], rwkv_ln_w[l], rwkv_ln_b[l])
                mem_col = rwkv_mu.shape[1] // mem_w
            else:
                j = l - n_a
                proj = _norm_proj(xb, mix_norm[l], moba_w_in[j])
                mix = _moba(proj, moba_q_norm[j], *shared)
                mem_col = mix_w // mem_w
            mk, mv = _mem_kv(memb, mem_norm[l], w_mem_kv[l], mem_k_norm[l])
            mem_out = _mem_attn(proj, mem_col, mem_q_norm[l], mk, mv)
            xb = _out_proj(xb, mix, mem_out, w_out[l])
            xb = _ffn(xb, ffn2_norm[l], ffn2_w1[l], ffn2_w3[l], ffn2_w2[l])
            if l == n_a - 1:
                shared = _shared_kv(xb, kv_norm, w_kv, kv_k_norm)
        outs.append(xb)
    return jnp.stack(outs)
```

```python
import functools
import math

import jax
import jax.numpy as jnp
from jax import lax
from jax.experimental import pallas as pl
from jax.experimental.pallas import tpu as pltpu

F32 = jnp.float32
BF16 = jnp.bfloat16
HI = lax.Precision.HIGHEST

HEAD = 64
LANES = 128
MXU = 256
CHUNK = 64
MOBA_BLOCK = 256
MOBA_TOPK = 3
ALIBI_MAX = 8.0
NORM_EPS = 1e-6
GN_EPS = 64e-5
VMEM_LIMIT = 56 * 1024 * 1024
NEG_INF = float("-inf")
LOG2E = 1.4426950408889634
DEAD_EXPONENT = 160.0
EXP_HEADROOM = 80.0
BF16_SLACK = 1.02


def _cparams(*sem):
    return pltpu.CompilerParams(dimension_semantics=sem, vmem_limit_bytes=VMEM_LIMIT)


def _const_spec(shape):
    return pl.BlockSpec(shape, lambda *_: (0,) * len(shape), pipeline_mode=pl.Buffered(1))


def _rms(x, g):
    return x * lax.rsqrt(jnp.mean(x * x, axis=-1, keepdims=True) + NORM_EPS) * g


def _b16(x):
    return x.astype(BF16)


def _split_bf16(x):
    hi = x.astype(BF16)
    return hi, (x - hi.astype(F32)).astype(BF16)


def _dot(a, b, precision=None):
    return jnp.dot(a, b, preferred_element_type=F32, precision=precision)


def _dot_nt(a, b, precision=None):
    return lax.dot_general(a, b, (((1,), (1,)), ((), ())), preferred_element_type=F32, precision=precision)


def _dot_tn(a, b, precision=None):
    return lax.dot_general(a, b, (((0,), (0,)), ((), ())), preferred_element_type=F32, precision=precision)


def _head_sum_matrix(scale=1.0):
    i = lax.broadcasted_iota(jnp.int32, (LANES, LANES), 0) // HEAD
    j = lax.broadcasted_iota(jnp.int32, (LANES, LANES), 1) // HEAD
    return jnp.where(i == j, F32(scale), F32(0.0))


def _head_rms(x, g):
    hi, lo = _split_bf16(x * x)
    mean_matrix = _b16(_head_sum_matrix(1.0 / HEAD))
    ms = _dot(hi, mean_matrix) + _dot(lo, mean_matrix)
    return x * lax.rsqrt(ms + NORM_EPS) * g


def _ffn_kernel(x_ref, g_ref, w1_ref, w3_ref, w2_ref, o_ref, acc_ref, *, f_chunk):
    x = x_ref[...]
    h = _rms(x, g_ref[...]).astype(BF16)
    n_chunks = w1_ref.shape[1] // f_chunk
    for c in range(n_chunks):
        sl = slice(c * f_chunk, (c + 1) * f_chunk)
        a = _dot(h, w1_ref[:, sl])
        b = _dot(h, w3_ref[:, sl])
        act = (a * jax.nn.sigmoid(a) * b).astype(BF16)
        part = _dot(act, w2_ref[sl, :])
        if c == 0:
            acc_ref[...] = part
        else:
            acc_ref[...] += part
    o_ref[...] = x + 0.5 * acc_ref[...]


def _ffn(x, g, w1, w3, w2, *, tm=512, f_chunk=MXU):
    s, d = x.shape
    f = w1.shape[1]
    fp = -(-f // f_chunk) * f_chunk
    w1 = jnp.pad(w1.astype(BF16), ((0, 0), (0, fp - f)))
    w3 = jnp.pad(w3.astype(BF16), ((0, 0), (0, fp - f)))
    w2 = jnp.pad(w2.astype(BF16), ((0, fp - f), (0, 0)))
    tm = min(tm, s)
    return pl.pallas_call(
        functools.partial(_ffn_kernel, f_chunk=f_chunk),
        grid=(s // tm,),
        in_specs=[pl.BlockSpec((tm, d), lambda i: (i, 0)),
                  _const_spec((1, d)), _const_spec((d, fp)), _const_spec((d, fp)), _const_spec((fp, d))],
        out_specs=pl.BlockSpec((tm, d), lambda i: (i, 0)),
        out_shape=jax.ShapeDtypeStruct((s, d), F32),
        scratch_shapes=[pltpu.VMEM((tm, d), F32)],
        compiler_params=_cparams("parallel"),
        name="ffn",
    )(x, g.reshape(1, d), w1, w3, w2)


def _norm_proj_kernel(x_ref, g_ref, w_ref, o_ref):
    h = _rms(x_ref[...], g_ref[...]).astype(BF16)
    o_ref[...] = _dot(h, w_ref[...])


def _norm_proj(x, g, w, *, tm=512):
    s, d = x.shape
    n = w.shape[1]
    tm = min(tm, s)
    return pl.pallas_call(
        _norm_proj_kernel,
        grid=(s // tm,),
        in_specs=[pl.BlockSpec((tm, d), lambda i: (i, 0)), _const_spec((1, d)), _const_spec((d, n))],
        out_specs=pl.BlockSpec((tm, n), lambda i: (i, 0)),
        out_shape=jax.ShapeDtypeStruct((s, n), F32),
        compiler_params=_cparams("parallel"),
        name="norm_proj",
    )(x, g.reshape(1, d), w.astype(BF16))


def _out_proj_kernel(x_ref, mix_ref, mem_ref, wa_ref, wb_ref, o_ref):
    o_ref[...] = (x_ref[...] + _dot(mix_ref[...].astype(BF16), wa_ref[...])
                  + _dot(mem_ref[...].astype(BF16), wb_ref[...]))


def _out_proj(x, mix, mem_out, w_out, *, tm=512):
    s, d = x.shape
    na, nb = mix.shape[1], mem_out.shape[1]
    tm = min(tm, s)
    w = w_out.astype(BF16)
    return pl.pallas_call(
        _out_proj_kernel,
        grid=(s // tm,),
        in_specs=[pl.BlockSpec((tm, d), lambda i: (i, 0)),
                  pl.BlockSpec((tm, na), lambda i: (i, 0)),
                  pl.BlockSpec((tm, nb), lambda i: (i, 0)),
                  _const_spec((na, d)), _const_spec((nb, d))],
        out_specs=pl.BlockSpec((tm, d), lambda i: (i, 0)),
        out_shape=jax.ShapeDtypeStruct((s, d), F32),
        compiler_params=_cparams("parallel"),
        name="out_proj",
    )(x, mix, mem_out, w[:na], w[na:])


def _mem_kv_kernel(mem_ref, g_ref, w_ref, kn_ref, k_ref, v_ref):
    h = _rms(mem_ref[...], g_ref[...]).astype(BF16)
    kv = _dot(h, w_ref[...])
    width = k_ref.shape[1]
    for p in range(width // LANES):
        sl = slice(p * LANES, (p + 1) * LANES)
        k_ref[:, sl] = _head_rms(kv[:, sl], kn_ref[...])
    v_ref[...] = kv[:, width:]


def _mem_kv(mem, g, w, k_norm):
    m, d = mem.shape
    width = w.shape[1] // 2
    kn = jnp.tile(k_norm, LANES // HEAD).reshape(1, LANES)
    return pl.pallas_call(
        _mem_kv_kernel,
        grid=(1,),
        in_specs=[_const_spec((m, d)), _const_spec((1, d)), _const_spec((d, 2 * width)), _const_spec((1, LANES))],
        out_specs=[pl.BlockSpec((m, width), lambda i: (0, 0)), pl.BlockSpec((m, width), lambda i: (0, 0))],
        out_shape=[jax.ShapeDtypeStruct((m, width), F32)] * 2,
        compiler_params=_cparams("arbitrary"),
        name="mem_kv",
    )(mem, g.reshape(1, d), w.astype(BF16), kn)


def _mem_attn_kernel(q_ref, qn_ref, k_ref, v_ref, o_ref):
    lane = lax.broadcasted_iota(jnp.int32, (1, LANES), 1)
    width = q_ref.shape[1]
    for p in range(width // LANES):
        sl = slice(p * LANES, (p + 1) * LANES)
        q = _head_rms(q_ref[:, sl], qn_ref[...]) * (HEAD ** -0.5)
        k = k_ref[:, sl].astype(BF16)
        v = v_ref[:, sl].astype(BF16)
        out = None
        for h in range(LANES // HEAD):
            in_head = (lane // HEAD) == h
            qh = jnp.where(in_head, q, 0.0).astype(BF16)
            s = _dot_nt(qh, k)
            s = s - jnp.max(s, axis=-1, keepdims=True)
            e = jnp.exp(s)
            pr = e / jnp.sum(e, axis=-1, keepdims=True)
            oh = _dot(pr.astype(BF16), v)
            out = oh if out is None else jnp.where(in_head, oh, out)
        o_ref[:, sl] = out


def _mem_attn(proj, col_block, q_norm, k, v, *, tm=512):
    s = proj.shape[0]
    m, width = k.shape
    tm = min(tm, s)
    qn = jnp.tile(q_norm, LANES // HEAD).reshape(1, LANES)
    return pl.pallas_call(
        _mem_attn_kernel,
        grid=(s // tm,),
        in_specs=[pl.BlockSpec((tm, width), lambda i: (i, col_block)),
                  _const_spec((1, LANES)), _const_spec((m, width)), _const_spec((m, width))],
        out_specs=pl.BlockSpec((tm, width), lambda i: (i, 0)),
        out_shape=jax.ShapeDtypeStruct((s, width), F32),
        compiler_params=_cparams("parallel"),
        name="mem_attn",
    )(proj, qn, k, v)


def _stack_heads(z):
    lane = lax.broadcasted_iota(jnp.int32, z.shape, 2)
    zero = jnp.zeros_like(z)
    return jnp.concatenate([jnp.where(lane < HEAD, z, zero), jnp.where(lane >= HEAD, z, zero)], axis=1)


def _bdot(a, b):
    return lax.dot_general(a, b, (((2,), (1,)), ((0,), (0,))), preferred_element_type=F32)


def _bdot_nt(a, b):
    return lax.dot_general(a, b, (((2,), (2,)), ((0,), (0,))), preferred_element_type=F32)


def _bdot_tn(a, b):
    return lax.dot_general(a, b, (((1,), (1,)), ((0,), (0,))), preferred_element_type=F32)


def _rwkv_kernel(proj_ref, mu_ref, wlh_ref, wll_ref, g2_ref, w0_ref, a0_ref, kk_ref, ka_ref, rk_ref, lnw_ref, lnb_ref,
                 o_ref, last_ref, h_ref, r_s, k_s, v_s, a_s, lw_s, cum_s, g_s):
    c = CHUNK
    n_pairs = r_s.shape[0]
    width = n_pairs * LANES

    @pl.when(pl.program_id(0) == 0)
    def _():
        last_ref[...] = jnp.zeros_like(last_ref)
        h_ref[...] = jnp.zeros_like(h_ref)

    u = proj_ref[...]
    row = lax.broadcasted_iota(jnp.int32, u.shape, 0)
    prev = jnp.where(row == 0, last_ref[...], pltpu.roll(u, 1, 0))
    last_ref[...] = u[c - 1:c, :]
    u = u + (prev - u) * mu_ref[...]

    lora_in = u[:, 3 * width:3 * width + LANES]
    lane = lax.broadcasted_iota(jnp.int32, lora_in.shape, 1)
    lora_in = jnp.where(lane < HEAD, jnp.tanh(lora_in), lora_in)
    x_hi, x_lo = _split_bf16(lora_in)
    lo = _dot(x_hi, wlh_ref[...]) + (_dot(x_lo, wlh_ref[...]) + _dot(x_hi, wll_ref[...]))
    z = -(w0_ref[...] + lo[:, :width])
    softplus = jnp.maximum(z, 0.0) + jnp.log(1.0 + jnp.exp(-jnp.abs(z)))
    lw = -jnp.exp(-softplus - 0.5)
    a = jax.nn.sigmoid(a0_ref[...] + lo[:, width:])
    g = _dot(_b16(jax.nn.sigmoid(u[:, 3 * width + LANES:])), g2_ref[...])
    ti = lax.broadcasted_iota(jnp.int32, (c, c), 0)
    si = lax.broadcasted_iota(jnp.int32, (c, c), 1)
    tri = _b16(jnp.where(si <= ti, F32(1.0), F32(0.0)))
    lw_hi, lw_lo = _split_bf16(lw)
    cum = _dot(tri, lw_hi) + _dot(tri, lw_lo)

    for p in range(n_pairs):
        sl = slice(p * LANES, (p + 1) * LANES)
        r_s[p] = u[:, sl]
        k_s[p] = u[:, width + p * LANES:width + (p + 1) * LANES]
        v_s[p] = u[:, 2 * width + p * LANES:2 * width + (p + 1) * LANES]
        a_s[p] = a[:, sl]
        lw_s[p] = lw[:, sl]
        cum_s[p] = cum[:, sl]
        g_s[p] = g[:, sl]

    np_ = n_pairs
    hsum = _b16(_head_sum_matrix())

    def head_sum(z):
        return _dot(_b16(z).reshape(np_ * c, LANES), hsum).reshape(np_, c, LANES)

    col = lax.broadcasted_iota(jnp.int32, (np_, c, LANES), 2) % c
    trow = lax.broadcasted_iota(jnp.int32, (np_, c, LANES), 1)
    strict = col < trow
    incl = col <= trow
    bi = lax.broadcasted_iota(jnp.int32, (np_, LANES, LANES), 1)
    bj = lax.broadcasted_iota(jnp.int32, (np_, LANES, LANES), 2)
    eye = bi == bj
    same_head = (bi // HEAD) == (bj // HEAD)

    r, k, v, a_, lw_, cum_ = r_s[...], k_s[...], v_s[...], a_s[...], lw_s[...], cum_s[...]
    kk = k * kk_ref[...]
    kk = kk / jnp.maximum(jnp.sqrt(head_sum(kk * kk)), 1e-12)
    kmod = k * (1.0 + (a_ - 1.0) * ka_ref[...])
    b = kk * a_
    bonus = head_sum(r * kmod * rk_ref[...]) * v

    tot = cum_[:, c - 1:c, :]
    at = -kk * jnp.exp(cum_ - lw_)
    rt = r * jnp.exp(cum_)
    einv = jnp.exp(-cum_)
    eend = jnp.exp(tot - cum_)
    bh, kh = _b16(b * einv), _b16(kmod * einv)
    bt, kt = _b16(b * eend), _b16(kmod * eend)
    at_b = _b16(at)

    aa = _bdot_nt(jnp.concatenate([at_b, _b16(rt)], axis=1),
                  jnp.concatenate([_stack_heads(bh), _stack_heads(kh)], axis=1))
    a_ab = jnp.where(strict, aa[:, :c, :LANES], 0.0)
    a_ak = _b16(jnp.where(strict, aa[:, :c, LANES:], 0.0))
    a_rb = _b16(jnp.where(incl, aa[:, c:, :LANES], 0.0))
    a_rk = _b16(jnp.where(incl, aa[:, c:, LANES:], 0.0))

    t_side = jnp.where(col == trow, 1.0, 0.0) + jnp.where((trow == col + 1) & (trow % 2 == 1), a_ab, 0.0)
    size = 2
    while size < c:
        e_mask = ((trow // (2 * size)) == (col // (2 * size))) & (trow % (2 * size) >= size) & (col % (2 * size) < size)
        t_b = _b16(t_side)
        t_blk = _stack_heads(t_b)
        te = _bdot(t_b, _stack_heads(_b16(jnp.where(e_mask, a_ab, 0.0))))
        t_side = t_side + _bdot(_b16(te), t_blk)
        size *= 2
    t_pair = _b16(t_side)

    v_b = _b16(v)
    v_st = _stack_heads(v_b)
    g0 = _bdot(a_ak, v_st)
    uw = _bdot(t_pair, jnp.concatenate([_stack_heads(_b16(g0)), _stack_heads(at_b)], axis=2))
    u0, w = _b16(uw[:, :, :LANES]), _b16(uw[:, :, LANES:])
    rw = rt + _bdot(a_rb, _stack_heads(w))
    y0 = _bdot(a_rb, _stack_heads(u0)) + _bdot(a_rk, v_st)
    m1 = jnp.where(eye, jnp.exp(tot), 0.0) + jnp.where(same_head, _bdot_tn(bt, w), 0.0)
    m0 = jnp.where(same_head, _bdot_tn(bt, u0) + _bdot_tn(kt, v_b), 0.0)

    h0 = _b16(h_ref[...])
    y = _bdot(_b16(rw), h0) + y0
    h_ref[...] = _bdot(_b16(m1), h0) + m0

    mean = head_sum(y) * (1.0 / HEAD)
    yc = y - mean
    var = head_sum(yc * yc) * (1.0 / HEAD)
    yn = yc * lax.rsqrt(var + GN_EPS) * lnw_ref[...] + lnb_ref[...]
    out = (yn + bonus) * g_s[...]
    for p in range(n_pairs):
        o_ref[:, p * LANES:(p + 1) * LANES] = out[p]


def _rwkv(proj, mu, w0, w2, a0, a2, g2, k_k, k_a, r_k, ln_w, ln_b):
    s = proj.shape[0]
    width = w0.shape[0]
    n_pairs = width // LANES
    shift_w = mu.shape[0]
    dl, al = w2.shape[0], a2.shape[0]
    assert dl + al == LANES and g2.shape[0] == LANES and shift_w == 3 * width + 2 * LANES
    wl = jnp.zeros((LANES, 2 * width), F32).at[:dl, :width].set(w2).at[dl:, width:].set(a2)
    wl_hi, wl_lo = _split_bf16(wl)

    def per_pair(t):
        return t.reshape(n_pairs, 1, LANES)

    c = CHUNK
    pair_spec = _const_spec((n_pairs, 1, LANES))
    return pl.pallas_call(
        _rwkv_kernel,
        grid=(s // c,),
        in_specs=[pl.BlockSpec((c, shift_w), lambda i: (i, 0)),
                  _const_spec((1, shift_w)), _const_spec((LANES, 2 * width)), _const_spec((LANES, 2 * width)),
                  _const_spec((LANES, width)),
                  _const_spec((1, width)), _const_spec((1, width)),
                  pair_spec, pair_spec, pair_spec, pair_spec, pair_spec],
        out_specs=pl.BlockSpec((c, width), lambda i: (i, 0)),
        out_shape=jax.ShapeDtypeStruct((s, width), F32),
        scratch_shapes=[pltpu.VMEM((1, shift_w), F32), pltpu.VMEM((n_pairs, LANES, LANES), F32)]
                       + [pltpu.VMEM((n_pairs, c, LANES), F32)] * 7,
        compiler_params=_cparams("arbitrary"),
        name="rwkv7",
    )(proj, mu.reshape(1, shift_w), wl_hi, wl_lo, g2.astype(BF16), w0.reshape(1, width), a0.reshape(1, width),
      per_pair(k_k), per_pair(k_a), per_pair(r_k.reshape(-1)), per_pair(ln_w), per_pair(ln_b))


def _shared_kv_kernel(x_ref, g_ref, wk_ref, wvt_ref, kn_ref, k_ref, vt_ref, km_ref):
    h = _rms(x_ref[...], g_ref[...]).astype(BF16)
    k = _dot(h, wk_ref[...])
    n_pairs = k_ref.shape[0]
    for p in range(n_pairs):
        kp = _head_rms(k[:, p * LANES:(p + 1) * LANES], kn_ref[...])
        k_ref[p, 0] = kp.astype(BF16)
        km_ref[0, :, p * LANES:(p + 1) * LANES] = jnp.mean(kp, axis=0, keepdims=True)
    vt = _dot_nt(wvt_ref[...], h)
    vt_ref[:, 0] = vt.reshape(n_pairs, LANES, vt.shape[1]).astype(BF16)


def _shared_kv(x, g, w_kv, k_norm):
    s, d = x.shape
    width = w_kv.shape[1] // 2
    n_pairs = width // LANES
    blk = MOBA_BLOCK
    nb = s // blk
    w = w_kv.astype(BF16)
    kn = jnp.tile(k_norm, LANES // HEAD).reshape(1, LANES)
    k, vt, km = pl.pallas_call(
        _shared_kv_kernel,
        grid=(nb,),
        in_specs=[pl.BlockSpec((blk, d), lambda i: (i, 0)), _const_spec((1, d)),
                  _const_spec((d, width)), _const_spec((width, d)), _const_spec((1, LANES))],
        out_specs=[pl.BlockSpec((n_pairs, 1, blk, LANES), lambda i: (0, i, 0, 0)),
                   pl.BlockSpec((n_pairs, 1, LANES, blk), lambda i: (0, i, 0, 0)),
                   pl.BlockSpec((1, 1, width), lambda i: (i, 0, 0))],
        out_shape=[jax.ShapeDtypeStruct((n_pairs, nb, blk, LANES), BF16),
                   jax.ShapeDtypeStruct((n_pairs, nb, LANES, blk), BF16),
                   jax.ShapeDtypeStruct((nb, 1, width), F32)],
        compiler_params=_cparams("parallel"),
        name="shared_kv",
    )(x, g.reshape(1, d), w[:, :width], w[:, width:].T, kn)
    k_bound = (HEAD ** 0.5) * jnp.max(jnp.abs(k_norm)).reshape(1)
    return k, vt, km.reshape(nb, width), k_bound


def _moba_kernel(slopes_ref, depth_ref, kbound_ref, q_ref, qn_ref, km_ref, k_ref, vt_ref, o_ref,
                 sel_ref, fac_ref, ownb_ref, base_ref, sa_ref, sb_ref):
    p = pl.program_id(0)
    i = pl.program_id(1)
    blk = MOBA_BLOCK
    nb = km_ref.shape[0]
    q = _head_rms(q_ref[...], qn_ref[...]) * (HEAD ** -0.5)
    lane = lax.broadcasted_iota(jnp.int32, (1, LANES), 1)
    q2 = jnp.concatenate([jnp.where(lane < HEAD, q, 0.0), jnp.where(lane >= HEAD, q, 0.0)], axis=0)
    qb = (q2 * LOG2E).astype(BF16)
    col = lax.broadcasted_iota(jnp.int32, (1, 2 * blk), 1)
    slope = jnp.where(col < blk, slopes_ref[2 * p], slopes_ref[2 * p + 1]) * LOG2E

    start = jnp.maximum(i - depth_ref[p] + 1, 0)
    n_iter = (i - start + 3) // 4

    def block_of(g, j):
        return jnp.minimum(start + 2 * g + j, nb - 1)

    def fast_scores(g, s_ref):
        for j in range(2):
            s_ref[j] = _dot_nt(k_ref[block_of(g, j)], qb)

    fast_scores(0, sa_ref)

    n_idx = lax.broadcasted_iota(jnp.int32, (nb, 2 * blk), 0)
    q_hi, q_lo = _split_bf16(q2)
    km_hi, km_lo = _split_bf16(km_ref[...])
    gate = _dot_nt(km_hi, q_hi) + (_dot_nt(km_lo, q_hi) + _dot_nt(km_hi, q_lo))
    gate = jnp.where(n_idx < i, gate, NEG_INF)
    sel_bias = jnp.full(gate.shape, NEG_INF, F32)
    for _ in range(MOBA_TOPK):
        mx = jnp.max(gate, axis=0, keepdims=True)
        first = jnp.min(jnp.where(gate == mx, n_idx, nb), axis=0, keepdims=True)
        pick = (n_idx == first) & (mx > NEG_INF)
        sel_bias = jnp.where(pick, 0.0, sel_bias)
        gate = jnp.where(pick, NEG_INF, gate)
    sel_ref[...] = sel_bias - slope * ((i - n_idx) * blk).astype(F32)

    pad = slope * (blk - 1)
    fac_ref[...] = jnp.exp2(sel_ref[...] + pad)

    @pl.when(i == 0)
    def _():
        key_pos = lax.broadcasted_iota(jnp.int32, (blk, 2 * blk), 0)
        qry_pos = lax.broadcasted_iota(jnp.int32, (blk, 2 * blk), 1) % blk
        dist = (qry_pos - key_pos).astype(F32)
        ownb_ref[...] = jnp.where(dist >= 0.0, -slope * dist, NEG_INF)
        base_ref[...] = -slope * dist - pad

    def weighted_values(n, e):
        eb = e.astype(BF16)
        return jnp.concatenate([_dot(vt_ref[n, :HEAD, :], eb[:, :blk]), _dot(vt_ref[n, HEAD:, :], eb[:, blk:])],
                               axis=1)

    e = jnp.exp2(_dot_nt(k_ref[i], qb) + ownb_ref[...])
    l_own = jnp.sum(e, axis=0, keepdims=True)
    acc_own = weighted_values(i, e)

    def fast_consume(g, s_ref, l, acc):
        for j in range(2):
            n = block_of(g, j)
            e = jnp.exp2(s_ref[j] + base_ref[...])
            f = fac_ref[pl.ds(n, 1), :]
            l = l + f * jnp.sum(e, axis=0, keepdims=True)
            acc = acc + f * weighted_values(n, e)
        return l, acc

    def fast_two_groups(u, carry):
        fast_scores(2 * u + 1, sb_ref)
        carry = fast_consume(2 * u, sa_ref, *carry)
        fast_scores(2 * u + 2, sa_ref)
        return fast_consume(2 * u + 1, sb_ref, *carry)

    l, acc = lax.fori_loop(0, n_iter, fast_two_groups, (l_own, acc_own))

    def emit(l, acc):
        out = acc / l
        o_ref[...] = jnp.concatenate([out[:, :blk], out[:, blk:]], axis=0).T

    emit(l, acc)

    q_len = jnp.sqrt(_dot_nt(jnp.ones((8, LANES), BF16), _b16(q2 * q2))[0:1])
    score_bound = q_len * (kbound_ref[0] * (LOG2E * BF16_SLACK))
    redo = jnp.max(jnp.where(score_bound < EXP_HEADROOM, 0.0, 1.0))
    redo = jnp.maximum(redo, jnp.maximum(jnp.max(jnp.where(jnp.abs(l) < jnp.inf, 0.0, 1.0)),
                                         jnp.max(jnp.where(jnp.abs(acc) < jnp.inf, 0.0, 1.0))))

    def any_block_of(g, j):
        return jnp.minimum(2 * g + j, nb - 1)

    def score_group(g, s_ref):
        col_max = []
        for j in range(2):
            n = any_block_of(g, j)
            s = _dot_nt(k_ref[n], qb) + (base_ref[...] + pad)
            s_ref[j] = s
            col_max.append(jnp.max(s, axis=0, keepdims=True) + sel_ref[pl.ds(n, 1), :])
        return col_max

    def consume_group(g, s_ref, col_max, m, l, acc):
        n0, n1 = any_block_of(g, 0), any_block_of(g, 1)
        m_new = jnp.maximum(m, jnp.maximum(col_max[0], col_max[1]))
        alpha = jnp.exp2(m - m_new)
        l = alpha * l
        acc = alpha * acc
        for j, n in enumerate((n0, n1)):
            e = jnp.exp2(s_ref[j] - (m_new - sel_ref[pl.ds(n, 1), :]))
            l = l + jnp.sum(e, axis=0, keepdims=True)
            acc = acc + weighted_values(n, e)
        return m_new, l, acc

    @pl.when(redo > 0.0)
    def _():
        max_a = score_group(0, sa_ref)
        s = _dot_nt(k_ref[i], qb) + ownb_ref[...]
        m_own = jnp.max(s, axis=0, keepdims=True)
        e = jnp.exp2(s - m_own)
        l_own = jnp.sum(e, axis=0, keepdims=True)
        acc_own = weighted_values(i, e)

        def two_groups(u, carry):
            max_a0, max_a1, m, l, acc = carry
            max_b = score_group(2 * u + 1, sb_ref)
            m, l, acc = consume_group(2 * u, sa_ref, (max_a0, max_a1), m, l, acc)
            max_a = score_group(2 * u + 2, sa_ref)
            m, l, acc = consume_group(2 * u + 1, sb_ref, max_b, m, l, acc)
            return max_a[0], max_a[1], m, l, acc

        init = (max_a[0], max_a[1], m_own, l_own, acc_own)
        _, _, _, l, acc = lax.fori_loop(0, (i + 3) // 4, two_groups, init)
        emit(l, acc)


def _moba(proj, q_norm, k, vt, km, k_bound):
    s = proj.shape[0]
    n_pairs, nb, blk, _ = k.shape
    heads = n_pairs * (LANES // HEAD)
    slopes = jnp.exp2(-ALIBI_MAX * jnp.arange(1, heads + 1, dtype=F32) / heads)
    min_slope = [2.0 ** (-ALIBI_MAX * (2 * p + 2) / heads) for p in range(n_pairs)]
    depth = jnp.asarray([math.ceil((DEAD_EXPONENT / (sl * LOG2E) + blk - 1) / blk) for sl in min_slope], jnp.int32)
    qn = jnp.tile(q_norm, LANES // HEAD).reshape(1, LANES)
    smem = pl.BlockSpec(memory_space=pltpu.SMEM)
    return pl.pallas_call(
        _moba_kernel,
        grid=(n_pairs, nb),
        in_specs=[smem, smem, smem,
                  pl.BlockSpec((blk, LANES), lambda p, i: (i, p)),
                  pl.BlockSpec((1, LANES), lambda p, i: (0, 0)),
                  pl.BlockSpec((nb, LANES), lambda p, i: (0, p)),
                  pl.BlockSpec((None, nb, blk, LANES), lambda p, i: (p, 0, 0, 0)),
                  pl.BlockSpec((None, nb, LANES, blk), lambda p, i: (p, 0, 0, 0))],
        out_specs=pl.BlockSpec((blk, LANES), lambda p, i: (i, p)),
        out_shape=jax.ShapeDtypeStruct((s, n_pairs * LANES), F32),
        scratch_shapes=[pltpu.VMEM((nb, 2 * blk), F32), pltpu.VMEM((nb, 2 * blk), F32),
                        pltpu.VMEM((blk, 2 * blk), F32), pltpu.VMEM((blk, 2 * blk), F32),
                        pltpu.VMEM((2, blk, 2 * blk), F32), pltpu.VMEM((2, blk, 2 * blk), F32)],
        compiler_params=_cparams("parallel", "arbitrary"),
        name="moba",
    )(slopes, depth, k_bound, proj, qn, km, k, vt)


def kernel(x, mem, ffn1_norm, ffn1_w1, ffn1_w3, ffn1_w2, mix_norm, w_out, mem_norm, w_mem_kv, mem_q_norm, mem_k_norm, ffn2_norm, ffn2_w1, ffn2_w3, ffn2_w2, rwkv_w_in, rwkv_mu, rwkv_w0, rwkv_w2, rwkv_a0, rwkv_a2, rwkv_g2, rwkv_k_k, rwkv_k_a, rwkv_r_k, rwkv_ln_w, rwkv_ln_b, kv_norm, w_kv, kv_k_norm, moba_w_in, moba_q_norm):
    batch = x.shape[0]
    depth = ffn1_norm.shape[0]
    n_a = rwkv_w_in.shape[0]
    mix_w = rwkv_w0.shape[1]
    mem_w = w_mem_kv.shape[2] // 2
    outs = []
    for bi in range(batch):
        xb, memb = x[bi], mem[bi]
        shared = None
        for l in range(depth):
            xb = _ffn(xb, ffn1_norm[l], ffn1_w1[l], ffn1_w3[l], ffn1_w2[l])
            if l < n_a:
                proj = _norm_proj(xb, mix_norm[l], rwkv_w_in[l])
                mix = _rwkv(proj, rwkv_mu[l], rwkv_w0[l], rwkv_w2[l], rwkv_a0[l], rwkv_a2[l], rwkv_g2[l],
                            rwkv_k_k[l], rwkv_k_a[l], rwkv_r_k[l], rwkv_ln_w[l], rwkv_ln_b[l])
                mem_col = rwkv_mu.shape[1] // mem_w
            else:
                j = l - n_a
                proj = _norm_proj(xb, mix_norm[l], moba_w_in[j])
                mix = _moba(proj, moba_q_norm[j], *shared)
                mem_col = mix_w // mem_w
            mk, mv = _mem_kv(memb, mem_norm[l], w_mem_kv[l], mem_k_norm[l])
            mem_out = _mem_attn(proj, mem_col, mem_q_norm[l], mk, mv)
            xb = _out_proj(xb, mix, mem_out, w_out[l])
            xb = _ffn(xb, ffn2_norm[l], ffn2_w1[l], ffn2_w3[l], ffn2_w2[l])
            if l == n_a - 1:
                shared = _shared_kv(xb, kv_norm, w_kv, kv_k_norm)
        outs.append(xb)
    return jnp.stack(outs)
```

```python
import functools
import math

import jax
import jax.numpy as jnp
from jax import lax
from jax.experimental import pallas as pl
from jax.experimental.pallas import tpu as pltpu

F32 = jnp.float32
BF16 = jnp.bfloat16
HI = lax.Precision.HIGHEST

HEAD = 64
LANES = 128
MXU = 256
CHUNK = 64
MOBA_BLOCK = 256
MOBA_TOPK = 3
ALIBI_MAX = 8.0
NORM_EPS = 1e-6
GN_EPS = 64e-5
VMEM_LIMIT = 56 * 1024 * 1024
NEG_INF = float("-inf")
LOG2E = 1.4426950408889634
DEAD_EXPONENT = 160.0
EXP_HEADROOM = 80.0
BF16_SLACK = 1.02


def _cparams(*sem):
    return pltpu.CompilerParams(dimension_semantics=sem, vmem_limit_bytes=VMEM_LIMIT)


def _const_spec(shape):
    return pl.BlockSpec(shape, lambda *_: (0,) * len(shape), pipeline_mode=pl.Buffered(1))


def _rms(x, g):
    return x * lax.rsqrt(jnp.mean(x * x, axis=-1, keepdims=True) + NORM_EPS) * g


def _b16(x):
    return x.astype(BF16)


def _split_bf16(x):
    hi = x.astype(BF16)
    return hi, (x - hi.astype(F32)).astype(BF16)


def _dot(a, b, precision=None):
    return jnp.dot(a, b, preferred_element_type=F32, precision=precision)


def _dot_nt(a, b, precision=None):
    return lax.dot_general(a, b, (((1,), (1,)), ((), ())), preferred_element_type=F32, precision=precision)


def _dot_tn(a, b, precision=None):
    return lax.dot_general(a, b, (((0,), (0,)), ((), ())), preferred_element_type=F32, precision=precision)


def _head_sum_matrix(scale=1.0):
    i = lax.broadcasted_iota(jnp.int32, (LANES, LANES), 0) // HEAD
    j = lax.broadcasted_iota(jnp.int32, (LANES, LANES), 1) // HEAD
    return jnp.where(i == j, F32(scale), F32(0.0))


def _head_rms(x, g):
    hi, lo = _split_bf16(x * x)
    mean_matrix = _b16(_head_sum_matrix(1.0 / HEAD))
    ms = _dot(hi, mean_matrix) + _dot(lo, mean_matrix)
    return x * lax.rsqrt(ms + NORM_EPS) * g


def _ffn_kernel(x_ref, g_ref, w1_ref, w3_ref, w2_ref, o_ref, acc_ref, *, f_chunk):
    x = x_ref[...]
    h = _rms(x, g_ref[...]).astype(BF16)
    n_chunks = w1_ref.shape[1] // f_chunk
    for c in range(n_chunks):
        sl = slice(c * f_chunk, (c + 1) * f_chunk)
        a = _dot(h, w1_ref[:, sl])
        b = _dot(h, w3_ref[:, sl])
        act = (a * jax.nn.sigmoid(a) * b).astype(BF16)
        part = _dot(act, w2_ref[sl, :])
        if c == 0:
            acc_ref[...] = part
        else:
            acc_ref[...] += part
    o_ref[...] = x + 0.5 * acc_ref[...]


def _ffn(x, g, w1, w3, w2, *, tm=512, f_chunk=MXU):
    s, d = x.shape
    f = w1.shape[1]
    fp = -(-f // f_chunk) * f_chunk
    w1 = jnp.pad(w1.astype(BF16), ((0, 0), (0, fp - f)))
    w3 = jnp.pad(w3.astype(BF16), ((0, 0), (0, fp - f)))
    w2 = jnp.pad(w2.astype(BF16), ((0, fp - f), (0, 0)))
    tm = min(tm, s)
    return pl.pallas_call(
        functools.partial(_ffn_kernel, f_chunk=f_chunk),
        grid=(s // tm,),
        in_specs=[pl.BlockSpec((tm, d), lambda i: (i, 0)),
                  _const_spec((1, d)), _const_spec((d, fp)), _const_spec((d, fp)), _const_spec((fp, d))],
        out_specs=pl.BlockSpec((tm, d), lambda i: (i, 0)),
        out_shape=jax.ShapeDtypeStruct((s, d), F32),
        scratch_shapes=[pltpu.VMEM((tm, d), F32)],
        compiler_params=_cparams("parallel"),
        name="ffn",
    )(x, g.reshape(1, d), w1, w3, w2)


def _norm_proj_kernel(x_ref, g_ref, w_ref, o_ref):
    h = _rms(x_ref[...], g_ref[...]).astype(BF16)
    o_ref[...] = _dot(h, w_ref[...])


def _norm_proj(x, g, w, *, tm=512):
    s, d = x.shape
    n = w.shape[1]
    tm = min(tm, s)
    return pl.pallas_call(
        _norm_proj_kernel,
        grid=(s // tm,),
        in_specs=[pl.BlockSpec((tm, d), lambda i: (i, 0)), _const_spec((1, d)), _const_spec((d, n))],
        out_specs=pl.BlockSpec((tm, n), lambda i: (i, 0)),
        out_shape=jax.ShapeDtypeStruct((s, n), F32),
        compiler_params=_cparams("parallel"),
        name="norm_proj",
    )(x, g.reshape(1, d), w.astype(BF16))


def _out_proj_kernel(x_ref, mix_ref, mem_ref, wa_ref, wb_ref, o_ref):
    o_ref[...] = (x_ref[...] + _dot(mix_ref[...].astype(BF16), wa_ref[...])
                  + _dot(mem_ref[...].astype(BF16), wb_ref[...]))


def _out_proj(x, mix, mem_out, w_out, *, tm=512):
    s, d = x.shape
    na, nb = mix.shape[1], mem_out.shape[1]
    tm = min(tm, s)
    w = w_out.astype(BF16)
    return pl.pallas_call(
        _out_proj_kernel,
        grid=(s // tm,),
        in_specs=[pl.BlockSpec((tm, d), lambda i: (i, 0)),
                  pl.BlockSpec((tm, na), lambda i: (i, 0)),
                  pl.BlockSpec((tm, nb), lambda i: (i, 0)),
                  _const_spec((na, d)), _const_spec((nb, d))],
        out_specs=pl.BlockSpec((tm, d), lambda i: (i, 0)),
        out_shape=jax.ShapeDtypeStruct((s, d), F32),
        compiler_params=_cparams("parallel"),
        name="out_proj",
    )(x, mix, mem_out, w[:na], w[na:])


def _mem_kv_kernel(mem_ref, g_ref, w_ref, kn_ref, k_ref, v_ref):
    h = _rms(mem_ref[...], g_ref[...]).astype(BF16)
    kv = _dot(h, w_ref[...])
    width = k_ref.shape[1]
    for p in range(width // LANES):
        sl = slice(p * LANES, (p + 1) * LANES)
        k_ref[:, sl] = _head_rms(kv[:, sl], kn_ref[...])
    v_ref[...] = kv[:, width:]


def _mem_kv(mem, g, w, k_norm):
    m, d = mem.shape
    width = w.shape[1] // 2
    kn = jnp.tile(k_norm, LANES // HEAD).reshape(1, LANES)
    return pl.pallas_call(
        _mem_kv_kernel,
        grid=(1,),
        in_specs=[_const_spec((m, d)), _const_spec((1, d)), _const_spec((d, 2 * width)), _const_spec((1, LANES))],
        out_specs=[pl.BlockSpec((m, width), lambda i: (0, 0)), pl.BlockSpec((m, width), lambda i: (0, 0))],
        out_shape=[jax.ShapeDtypeStruct((m, width), F32)] * 2,
        compiler_params=_cparams("arbitrary"),
        name="mem_kv",
    )(mem, g.reshape(1, d), w.astype(BF16), kn)


def _mem_attn_kernel(q_ref, qn_ref, k_ref, v_ref, o_ref):
    lane = lax.broadcasted_iota(jnp.int32, (1, LANES), 1)
    width = q_ref.shape[1]
    for p in range(width // LANES):
        sl = slice(p * LANES, (p + 1) * LANES)
        q = _head_rms(q_ref[:, sl], qn_ref[...]) * (HEAD ** -0.5)
        k = k_ref[:, sl].astype(BF16)
        v = v_ref[:, sl].astype(BF16)
        out = None
        for h in range(LANES // HEAD):
            in_head = (lane // HEAD) == h
            qh = jnp.where(in_head, q, 0.0).astype(BF16)
            s = _dot_nt(qh, k)
            s = s - jnp.max(s, axis=-1, keepdims=True)
            e = jnp.exp(s)
            pr = e / jnp.sum(e, axis=-1, keepdims=True)
            oh = _dot(pr.astype(BF16), v)
            out = oh if out is None else jnp.where(in_head, oh, out)
        o_ref[:, sl] = out


def _mem_attn(proj, col_block, q_norm, k, v, *, tm=512):
    s = proj.shape[0]
    m, width = k.shape
    tm = min(tm, s)
    qn = jnp.tile(q_norm, LANES // HEAD).reshape(1, LANES)
    return pl.pallas_call(
        _mem_attn_kernel,
        grid=(s // tm,),
        in_specs=[pl.BlockSpec((tm, width), lambda i: (i, col_block)),
                  _const_spec((1, LANES)), _const_spec((m, width)), _const_spec((m, width))],
        out_specs=pl.BlockSpec((tm, width), lambda i: (i, 0)),
        out_shape=jax.ShapeDtypeStruct((s, width), F32),
        compiler_params=_cparams("parallel"),
        name="mem_attn",
    )(proj, qn, k, v)


def _stack_heads(z):
    lane = lax.broadcasted_iota(jnp.int32, z.shape, 2)
    zero = jnp.zeros_like(z)
    return jnp.concatenate([jnp.where(lane < HEAD, z, zero), jnp.where(lane >= HEAD, z, zero)], axis=1)


def _bdot(a, b):
    return lax.dot_general(a, b, (((2,), (1,)), ((0,), (0,))), preferred_element_type=F32)


def _bdot_nt(a, b):
    return lax.dot_general(a, b, (((2,), (2,)), ((0,), (0,))), preferred_element_type=F32)


def _bdot_tn(a, b):
    return lax.dot_general(a, b, (((1,), (1,)), ((0,), (0,))), preferred_element_type=F32)


def _rwkv_kernel(proj_ref, mu_ref, wlh_ref, wll_ref, g2_ref, w0_ref, a0_ref, kk_ref, ka_ref, rk_ref, lnw_ref, lnb_ref,
                 o_ref, last_ref, h_ref, r_s, k_s, v_s, a_s, lw_s, cum_s, g_s):
    c = CHUNK
    n_pairs = r_s.shape[0]
    width = n_pairs * LANES

    @pl.when(pl.program_id(0) == 0)
    def _():
        last_ref[...] = jnp.zeros_like(last_ref)
        h_ref[...] = jnp.zeros_like(h_ref)

    u = proj_ref[...]
    row = lax.broadcasted_iota(jnp.int32, u.shape, 0)
    prev = jnp.where(row == 0, last_ref[...], pltpu.roll(u, 1, 0))
    last_ref[...] = u[c - 1:c, :]
    u = u + (prev - u) * mu_ref[...]

    lora_in = u[:, 3 * width:3 * width + LANES]
    lane = lax.broadcasted_iota(jnp.int32, lora_in.shape, 1)
    lora_in = jnp.where(lane < HEAD, jnp.tanh(lora_in), lora_in)
    x_hi, x_lo = _split_bf16(lora_in)
    lo = _dot(x_hi, wlh_ref[...]) + (_dot(x_lo, wlh_ref[...]) + _dot(x_hi, wll_ref[...]))
    z = -(w0_ref[...] + lo[:, :width])
    softplus = jnp.maximum(z, 0.0) + jnp.log(1.0 + jnp.exp(-jnp.abs(z)))
    lw = -jnp.exp(-softplus - 0.5)
    a = jax.nn.sigmoid(a0_ref[...] + lo[:, width:])
    g = _dot(_b16(jax.nn.sigmoid(u[:, 3 * width + LANES:])), g2_ref[...])
    ti = lax.broadcasted_iota(jnp.int32, (c, c), 0)
    si = lax.broadcasted_iota(jnp.int32, (c, c), 1)
    tri = _b16(jnp.where(si <= ti, F32(1.0), F32(0.0)))
    lw_hi, lw_lo = _split_bf16(lw)
    cum = _dot(tri, lw_hi) + _dot(tri, lw_lo)

    for p in range(n_pairs):
        sl = slice(p * LANES, (p + 1) * LANES)
        r_s[p] = u[:, sl]
        k_s[p] = u[:, width + p * LANES:width + (p + 1) * LANES]
        v_s[p] = u[:, 2 * width + p * LANES:2 * width + (p + 1) * LANES]
        a_s[p] = a[:, sl]
        lw_s[p] = lw[:, sl]
        cum_s[p] = cum[:, sl]
        g_s[p] = g[:, sl]

    np_ = n_pairs
    hsum = _b16(_head_sum_matrix())

    def head_sum(z):
        return _dot(_b16(z).reshape(np_ * c, LANES), hsum).reshape(np_, c, LANES)

    col = lax.broadcasted_iota(jnp.int32, (np_, c, LANES), 2) % c
    trow = lax.broadcasted_iota(jnp.int32, (np_, c, LANES), 1)
    strict = col < trow
    incl = col <= trow
    bi = lax.broadcasted_iota(jnp.int32, (np_, LANES, LANES), 1)
    bj = lax.broadcasted_iota(jnp.int32, (np_, LANES, LANES), 2)
    eye = bi == bj
    same_head = (bi // HEAD) == (bj // HEAD)

    r, k, v, a_, lw_, cum_ = r_s[...], k_s[...], v_s[...], a_s[...], lw_s[...], cum_s[...]
    kk = k * kk_ref[...]
    kk = kk / jnp.maximum(jnp.sqrt(head_sum(kk * kk)), 1e-12)
    kmod = k * (1.0 + (a_ - 1.0) * ka_ref[...])
    b = kk * a_
    bonus = head_sum(r * kmod * rk_ref[...]) * v

    tot = cum_[:, c - 1:c, :]
    at = -kk * jnp.exp(cum_ - lw_)
    rt = r * jnp.exp(cum_)
    einv = jnp.exp(-cum_)
    eend = jnp.exp(tot - cum_)
    bh, kh = _b16(b * einv), _b16(kmod * einv)
    bt, kt = _b16(b * eend), _b16(kmod * eend)
    at_b = _b16(at)

    aa = _bdot_nt(jnp.concatenate([at_b, _b16(rt)], axis=1),
                  jnp.concatenate([_stack_heads(bh), _stack_heads(kh)], axis=1))
    a_ab = jnp.where(strict, aa[:, :c, :LANES], 0.0)
    a_ak = _b16(jnp.where(strict, aa[:, :c, LANES:], 0.0))
    a_rb = _b16(jnp.where(incl, aa[:, c:, :LANES], 0.0))
    a_rk = _b16(jnp.where(incl, aa[:, c:, LANES:], 0.0))

    t_side = jnp.where(col == trow, 1.0, 0.0) + jnp.where((trow == col + 1) & (trow % 2 == 1), a_ab, 0.0)
    size = 2
    while size < c:
        e_mask = ((trow // (2 * size)) == (col // (2 * size))) & (trow % (2 * size) >= size) & (col % (2 * size) < size)
        t_b = _b16(t_side)
        t_blk = _stack_heads(t_b)
        te = _bdot(t_b, _stack_heads(_b16(jnp.where(e_mask, a_ab, 0.0))))
        t_side = t_side + _bdot(_b16(te), t_blk)
        size *= 2
    t_pair = _b16(t_side)

    v_b = _b16(v)
    v_st = _stack_heads(v_b)
    g0 = _bdot(a_ak, v_st)
    uw = _bdot(t_pair, jnp.concatenate([_stack_heads(_b16(g0)), _stack_heads(at_b)], axis=2))
    u0, w = _b16(uw[:, :, :LANES]), _b16(uw[:, :, LANES:])
    rw = rt + _bdot(a_rb, _stack_heads(w))
    y0 = _bdot(a_rb, _stack_heads(u0)) + _bdot(a_rk, v_st)
    m1 = jnp.where(eye, jnp.exp(tot), 0.0) + jnp.where(same_head, _bdot_tn(bt, w), 0.0)
    m0 = jnp.where(same_head, _bdot_tn(bt, u0) + _bdot_tn(kt, v_b), 0.0)

    h0 = _b16(h_ref[...])
    y = _bdot(_b16(rw), h0) + y0
    h_ref[...] = _bdot(_b16(m1), h0) + m0

    mean = head_sum(y) * (1.0 / HEAD)
    yc = y - mean
    var = head_sum(yc * yc) * (1.0 / HEAD)
    yn = yc * lax.rsqrt(var + GN_EPS) * lnw_ref[...] + lnb_ref[...]
    out = (yn + bonus) * g_s[...]
    for p in range(n_pairs):
        o_ref[:, p * LANES:(p + 1) * LANES] = out[p]


def _rwkv(proj, mu, w0, w2, a0, a2, g2, k_k, k_a, r_k, ln_w, ln_b):
    s = proj.shape[0]
    width = w0.shape[0]
    n_pairs = width // LANES
    shift_w = mu.shape[0]
    dl, al = w2.shape[0], a2.shape[0]
    assert dl + al == LANES and g2.shape[0] == LANES and shift_w == 3 * width + 2 * LANES
    wl = jnp.zeros((LANES, 2 * width), F32).at[:dl, :width].set(w2).at[dl:, width:].set(a2)
    wl_hi, wl_lo = _split_bf16(wl)

    def per_pair(t):
        return t.reshape(n_pairs, 1, LANES)

    c = CHUNK
    pair_spec = _const_spec((n_pairs, 1, LANES))
    return pl.pallas_call(
        _rwkv_kernel,
        grid=(s // c,),
        in_specs=[pl.BlockSpec((c, shift_w), lambda i: (i, 0)),
                  _const_spec((1, shift_w)), _const_spec((LANES, 2 * width)), _const_spec((LANES, 2 * width)),
                  _const_spec((LANES, width)),
                  _const_spec((1, width)), _const_spec((1, width)),
                  pair_spec, pair_spec, pair_spec, pair_spec, pair_spec],
        out_specs=pl.BlockSpec((c, width), lambda i: (i, 0)),
        out_shape=jax.ShapeDtypeStruct((s, width), F32),
        scratch_shapes=[pltpu.VMEM((1, shift_w), F32), pltpu.VMEM((n_pairs, LANES, LANES), F32)]
                       + [pltpu.VMEM((n_pairs, c, LANES), F32)] * 7,
        compiler_params=_cparams("arbitrary"),
        name="rwkv7",
    )(proj, mu.reshape(1, shift_w), wl_hi, wl_lo, g2.astype(BF16), w0.reshape(1, width), a0.reshape(1, width),
      per_pair(k_k), per_pair(k_a), per_pair(r_k.reshape(-1)), per_pair(ln_w), per_pair(ln_b))


def _shared_kv_kernel(x_ref, g_ref, wk_ref, wvt_ref, kn_ref, k_ref, vt_ref, km_ref):
    h = _rms(x_ref[...], g_ref[...]).astype(BF16)
    k = _dot(h, wk_ref[...])
    n_pairs = k_ref.shape[0]
    for p in range(n_pairs):
        kp = _head_rms(k[:, p * LANES:(p + 1) * LANES], kn_ref[...])
        k_ref[p, 0] = kp.astype(BF16)
        km_ref[0, :, p * LANES:(p + 1) * LANES] = jnp.mean(kp, axis=0, keepdims=True)
    vt = _dot_nt(wvt_ref[...], h)
    vt_ref[:, 0] = vt.reshape(n_pairs, LANES, vt.shape[1]).astype(BF16)


def _shared_kv(x, g, w_kv, k_norm):
    s, d = x.shape
    width = w_kv.shape[1] // 2
    n_pairs = width // LANES
    blk = MOBA_BLOCK
    nb = s // blk
    w = w_kv.astype(BF16)
    kn = jnp.tile(k_norm, LANES // HEAD).reshape(1, LANES)
    k, vt, km = pl.pallas_call(
        _shared_kv_kernel,
        grid=(nb,),
        in_specs=[pl.BlockSpec((blk, d), lambda i: (i, 0)), _const_spec((1, d)),
                  _const_spec((d, width)), _const_spec((width, d)), _const_spec((1, LANES))],
        out_specs=[pl.BlockSpec((n_pairs, 1, blk, LANES), lambda i: (0, i, 0, 0)),
                   pl.BlockSpec((n_pairs, 1, LANES, blk), lambda i: (0, i, 0, 0)),
                   pl.BlockSpec((1, 1, width), lambda i: (i, 0, 0))],
        out_shape=[jax.ShapeDtypeStruct((n_pairs, nb, blk, LANES), BF16),
                   jax.ShapeDtypeStruct((n_pairs, nb, LANES, blk), BF16),
                   jax.ShapeDtypeStruct((nb, 1, width), F32)],
        compiler_params=_cparams("parallel"),
        name="shared_kv",
    )(x, g.reshape(1, d), w[:, :width], w[:, width:].T, kn)
    k_bound = (HEAD ** 0.5) * jnp.max(jnp.abs(k_norm)).reshape(1)
    return k, vt, km.reshape(nb, width), k_bound


def _moba_kernel(slopes_ref, depth_ref, kbound_ref, q_ref, qn_ref, km_ref, k_ref, vt_ref, o_ref,
                 sel_ref, fac_ref, ownb_ref, base_ref, sa_ref, sb_ref):
    p = pl.program_id(0)
    i = pl.program_id(1)
    blk = MOBA_BLOCK
    nb = km_ref.shape[0]
    col = lax.broadcasted_iota(jnp.int32, (1, 2 * blk), 1)
    slope = jnp.where(col < blk, slopes_ref[2 * p], slopes_ref[2 * p + 1]) * LOG2E
    pad = slope * (blk - 1)

    @pl.when(i == 0)
    def _():
        key_pos = lax.broadcasted_iota(jnp.int32, (blk, 2 * blk), 0)
        qry_pos = lax.broadcasted_iota(jnp.int32, (blk, 2 * blk), 1) % blk
        dist = (qry_pos - key_pos).astype(F32)
        ownb_ref[...] = jnp.where(dist >= 0.0, -slope * dist, NEG_INF)
        base_ref[...] = -slope * dist - pad

    q = _head_rms(q_ref[...], qn_ref[...]) * (HEAD ** -0.5)
    lane = lax.broadcasted_iota(jnp.int32, (1, LANES), 1)
    q2 = jnp.concatenate([jnp.where(lane < HEAD, q, 0.0), jnp.where(lane >= HEAD, q, 0.0)], axis=0)
    qb = (q2 * LOG2E).astype(BF16)

    start = jnp.maximum(i - depth_ref[p] + 1, 0)
    n_iter = (i - start + 3) // 4

    def block_of(g, j):
        return jnp.minimum(start + 2 * g + j, nb - 1)

    def fast_scores(g, s_ref):
        for j in range(2):
            s_ref[j] = _dot_nt(k_ref[block_of(g, j)], qb)

    fast_scores(0, sa_ref)

    n_idx = lax.broadcasted_iota(jnp.int32, (nb, 2 * blk), 0)
    q_hi, q_lo = _split_bf16(q2)
    km_hi, km_lo = _split_bf16(km_ref[...])
    gate = _dot_nt(km_hi, q_hi) + (_dot_nt(km_lo, q_hi) + _dot_nt(km_hi, q_lo))
    gate = jnp.where(n_idx < i, gate, NEG_INF)
    sel_bias = jnp.full(gate.shape, NEG_INF, F32)
    for _ in range(MOBA_TOPK):
        mx = jnp.max(gate, axis=0, keepdims=True)
        first = jnp.min(jnp.where(gate == mx, n_idx, nb), axis=0, keepdims=True)
        pick = (n_idx == first) & (mx > NEG_INF)
        sel_bias = jnp.where(pick, 0.0, sel_bias)
        gate = jnp.where(pick, NEG_INF, gate)
    sel_ref[...] = sel_bias - slope * ((i - n_idx) * blk).astype(F32)

    fac_ref[...] = jnp.exp2(sel_ref[...] + pad)

    def weighted_values(n, e):
        eb = e.astype(BF16)
        return jnp.concatenate([_dot(vt_ref[n, :HEAD, :], eb[:, :blk]), _dot(vt_ref[n, HEAD:, :], eb[:, blk:])],
                               axis=1)

    e = jnp.exp2(_dot_nt(k_ref[i], qb) + ownb_ref[...])
    l_own = jnp.sum(e, axis=0, keepdims=True)
    acc_own = weighted_values(i, e)

    def fast_consume(g, s_ref, l, acc):
        for j in range(2):
            n = block_of(g, j)
            e = jnp.exp2(s_ref[j] + base_ref[...])
            f = fac_ref[pl.ds(n, 1), :]
            l = l + f * jnp.sum(e, axis=0, keepdims=True)
            acc = acc + f * weighted_values(n, e)
        return l, acc

    def fast_two_groups(u, carry):
        fast_scores(2 * u + 1, sb_ref)
        carry = fast_consume(2 * u, sa_ref, *carry)
        fast_scores(2 * u + 2, sa_ref)
        return fast_consume(2 * u + 1, sb_ref, *carry)

    l, acc = lax.fori_loop(0, n_iter, fast_two_groups, (l_own, acc_own))

    def emit(l, acc):
        out = acc / l
        o_ref[...] = jnp.concatenate([out[:, :blk], out[:, blk:]], axis=0).T

    emit(l, acc)

    q_len = jnp.sqrt(_dot_nt(jnp.ones((8, LANES), BF16), _b16(q2 * q2))[0:1])
    score_bound = q_len * (kbound_ref[0] * (LOG2E * BF16_SLACK))
    redo = jnp.max(jnp.where(score_bound < EXP_HEADROOM, 0.0, 1.0))
    redo = jnp.maximum(redo, jnp.maximum(jnp.max(jnp.where(jnp.abs(l) < jnp.inf, 0.0, 1.0)),
                                         jnp.max(jnp.where(jnp.abs(acc) < jnp.inf, 0.0, 1.0))))

    def any_block_of(g, j):
        return jnp.minimum(2 * g + j, nb - 1)

    def score_group(g, s_ref):
        col_max = []
        for j in range(2):
            n = any_block_of(g, j)
            s = _dot_nt(k_ref[n], qb) + (base_ref[...] + pad)
            s_ref[j] = s
            col_max.append(jnp.max(s, axis=0, keepdims=True) + sel_ref[pl.ds(n, 1), :])
        return col_max

    def consume_group(g, s_ref, col_max, m, l, acc):
        n0, n1 = any_block_of(g, 0), any_block_of(g, 1)
        m_new = jnp.maximum(m, jnp.maximum(col_max[0], col_max[1]))
        alpha = jnp.exp2(m - m_new)
        l = alpha * l
        acc = alpha * acc
        for j, n in enumerate((n0, n1)):
            e = jnp.exp2(s_ref[j] - (m_new - sel_ref[pl.ds(n, 1), :]))
            l = l + jnp.sum(e, axis=0, keepdims=True)
            acc = acc + weighted_values(n, e)
        return m_new, l, acc

    @pl.when(redo > 0.0)
    def _():
        max_a = score_group(0, sa_ref)
        s = _dot_nt(k_ref[i], qb) + ownb_ref[...]
        m_own = jnp.max(s, axis=0, keepdims=True)
        e = jnp.exp2(s - m_own)
        l_own = jnp.sum(e, axis=0, keepdims=True)
        acc_own = weighted_values(i, e)

        def two_groups(u, carry):
            max_a0, max_a1, m, l, acc = carry
            max_b = score_group(2 * u + 1, sb_ref)
            m, l, acc = consume_group(2 * u, sa_ref, (max_a0, max_a1), m, l, acc)
            max_a = score_group(2 * u + 2, sa_ref)
            m, l, acc = consume_group(2 * u + 1, sb_ref, max_b, m, l, acc)
            return max_a[0], max_a[1], m, l, acc

        init = (max_a[0], max_a[1], m_own, l_own, acc_own)
        _, _, _, l, acc = lax.fori_loop(0, (i + 3) // 4, two_groups, init)
        emit(l, acc)


def _moba(proj, q_norm, k, vt, km, k_bound):
    s = proj.shape[0]
    n_pairs, nb, blk, _ = k.shape
    heads = n_pairs * (LANES // HEAD)
    slopes = jnp.exp2(-ALIBI_MAX * jnp.arange(1, heads + 1, dtype=F32) / heads)
    min_slope = [2.0 ** (-ALIBI_MAX * (2 * p + 2) / heads) for p in range(n_pairs)]
    depth = jnp.asarray([math.ceil((DEAD_EXPONENT / (sl * LOG2E) + blk - 1) / blk) for sl in min_slope], jnp.int32)
    qn = jnp.tile(q_norm, LANES // HEAD).reshape(1, LANES)
    smem = pl.BlockSpec(memory_space=pltpu.SMEM)
    return pl.pallas_call(
        _moba_kernel,
        grid=(n_pairs, nb),
        in_specs=[smem, smem, smem,
                  pl.BlockSpec((blk, LANES), lambda p, i: (i, p)),
                  pl.BlockSpec((1, LANES), lambda p, i: (0, 0)),
                  pl.BlockSpec((nb, LANES), lambda p, i: (0, p)),
                  pl.BlockSpec((None, nb, blk, LANES), lambda p, i: (p, 0, 0, 0)),
                  pl.BlockSpec((None, nb, LANES, blk), lambda p, i: (p, 0, 0, 0))],
        out_specs=pl.BlockSpec((blk, LANES), lambda p, i: (i, p)),
        out_shape=jax.ShapeDtypeStruct((s, n_pairs * LANES), F32),
        scratch_shapes=[pltpu.VMEM((nb, 2 * blk), F32), pltpu.VMEM((nb, 2 * blk), F32),
                        pltpu.VMEM((blk, 2 * blk), F32), pltpu.VMEM((blk, 2 * blk), F32),
                        pltpu.VMEM((2, blk, 2 * blk), F32), pltpu.VMEM((2, blk, 2 * blk), F32)],
        compiler_params=_cparams("parallel", "arbitrary"),
        name="moba",
    )(slopes, depth, k_bound, proj, qn, km, k, vt)


def kernel(x, mem, ffn1_norm, ffn1_w1, ffn1_w3, ffn1_w2, mix_norm, w_out, mem_norm, w_mem_kv, mem_q_norm, mem_k_norm, ffn2_norm, ffn2_w1, ffn2_w3, ffn2_w2, rwkv_w_in, rwkv_mu, rwkv_w0, rwkv_w2, rwkv_a0, rwkv_a2, rwkv_g2, rwkv_k_k, rwkv_k_a, rwkv_r_k, rwkv_ln_w, rwkv_ln_b, kv_norm, w_kv, kv_k_norm, moba_w_in, moba_q_norm):
    batch = x.shape[0]
    depth = ffn1_norm.shape[0]
    n_a = rwkv_w_in.shape[0]
    mix_w = rwkv_w0.shape[1]
    mem_w = w_mem_kv.shape[2] // 2
    outs = []
    for bi in range(batch):
        xb, memb = x[bi], mem[bi]
        shared = None
        for l in range(depth):
            xb = _ffn(xb, ffn1_norm[l], ffn1_w1[l], ffn1_w3[l], ffn1_w2[l])
            if l < n_a:
                proj = _norm_proj(xb, mix_norm[l], rwkv_w_in[l])
                mix = _rwkv(proj, rwkv_mu[l], rwkv_w0[l], rwkv_w2[l], rwkv_a0[l], rwkv_a2[l], rwkv_g2[l],
                            rwkv_k_k[l], rwkv_k_a[l], rwkv_r_k[l], rwkv_ln_w[l], rwkv_ln_b[l])
                mem_col = rwkv_mu.shape[1] // mem_w
            else:
                j = l - n_a
                proj = _norm_proj(xb, mix_norm[l], moba_w_in[j])
                mix = _moba(proj, moba_q_norm[j], *shared)
                mem_col = mix_w // mem_w
            mk, mv = _mem_kv(memb, mem_norm[l], w_mem_kv[l], mem_k_norm[l])
            mem_out = _mem_attn(proj, mem_col, mem_q_norm[l], mk, mv)
            xb = _out_proj(xb, mix, mem_out, w_out[l])
            xb = _ffn(xb, ffn2_norm[l], ffn2_w1[l], ffn2_w3[l], ffn2_w2[l])
            if l == n_a - 1:
                shared = _shared_kv(xb, kv_norm, w_kv, kv_k_norm)
        outs.append(xb)
    return jnp.stack(outs)
```

```python
import functools
import math

import jax
import jax.numpy as jnp
from jax import lax
from jax.experimental import pallas as pl
from jax.experimental.pallas import tpu as pltpu

F32 = jnp.float32
BF16 = jnp.bfloat16
HI = lax.Precision.HIGHEST

HEAD = 64
LANES = 128
MXU = 256
CHUNK = 64
RWKV_CHUNKS_PER_STEP = 4
MOBA_BLOCK = 256
MOBA_TOPK = 3
ALIBI_MAX = 8.0
NORM_EPS = 1e-6
GN_EPS = 64e-5
VMEM_LIMIT = 56 * 1024 * 1024
NEG_INF = float("-inf")
LOG2E = 1.4426950408889634
DEAD_EXPONENT = 160.0
EXP_HEADROOM = 80.0
BF16_SLACK = 1.02


def _cparams(*sem):
    return pltpu.CompilerParams(dimension_semantics=sem, vmem_limit_bytes=VMEM_LIMIT)


def _const_spec(shape):
    return pl.BlockSpec(shape, lambda *_: (0,) * len(shape), pipeline_mode=pl.Buffered(1))


def _rms(x, g):
    return x * lax.rsqrt(jnp.mean(x * x, axis=-1, keepdims=True) + NORM_EPS) * g


def _b16(x):
    return x.astype(BF16)


def _split_bf16(x):
    hi = x.astype(BF16)
    return hi, (x - hi.astype(F32)).astype(BF16)


def _dot(a, b, precision=None):
    return jnp.dot(a, b, preferred_element_type=F32, precision=precision)


def _dot_nt(a, b, precision=None):
    return lax.dot_general(a, b, (((1,), (1,)), ((), ())), preferred_element_type=F32, precision=precision)


def _dot_tn(a, b, precision=None):
    return lax.dot_general(a, b, (((0,), (0,)), ((), ())), preferred_element_type=F32, precision=precision)


def _head_sum_matrix(scale=1.0):
    i = lax.broadcasted_iota(jnp.int32, (LANES, LANES), 0) // HEAD
    j = lax.broadcasted_iota(jnp.int32, (LANES, LANES), 1) // HEAD
    return jnp.where(i == j, F32(scale), F32(0.0))


def _head_rms(x, g):
    hi, lo = _split_bf16(x * x)
    mean_matrix = _b16(_head_sum_matrix(1.0 / HEAD))
    ms = _dot(hi, mean_matrix) + _dot(lo, mean_matrix)
    return x * lax.rsqrt(ms + NORM_EPS) * g


def _ffn_kernel(x_ref, g_ref, w1_ref, w3_ref, w2_ref, o_ref, acc_ref, *, f_chunk):
    x = x_ref[...]
    h = _rms(x, g_ref[...]).astype(BF16)
    n_chunks = w1_ref.shape[1] // f_chunk
    for c in range(n_chunks):
        sl = slice(c * f_chunk, (c + 1) * f_chunk)
        a = _dot(h, w1_ref[:, sl])
        b = _dot(h, w3_ref[:, sl])
        act = (a * jax.nn.sigmoid(a) * b).astype(BF16)
        part = _dot(act, w2_ref[sl, :])
        if c == 0:
            acc_ref[...] = part
        else:
            acc_ref[...] += part
    o_ref[...] = x + 0.5 * acc_ref[...]


def _ffn(x, g, w1, w3, w2, *, tm=512, f_chunk=MXU):
    s, d = x.shape
    f = w1.shape[1]
    fp = -(-f // f_chunk) * f_chunk
    w1 = jnp.pad(w1.astype(BF16), ((0, 0), (0, fp - f)))
    w3 = jnp.pad(w3.astype(BF16), ((0, 0), (0, fp - f)))
    w2 = jnp.pad(w2.astype(BF16), ((0, fp - f), (0, 0)))
    tm = min(tm, s)
    return pl.pallas_call(
        functools.partial(_ffn_kernel, f_chunk=f_chunk),
        grid=(s // tm,),
        in_specs=[pl.BlockSpec((tm, d), lambda i: (i, 0)),
                  _const_spec((1, d)), _const_spec((d, fp)), _const_spec((d, fp)), _const_spec((fp, d))],
        out_specs=pl.BlockSpec((tm, d), lambda i: (i, 0)),
        out_shape=jax.ShapeDtypeStruct((s, d), F32),
        scratch_shapes=[pltpu.VMEM((tm, d), F32)],
        compiler_params=_cparams("parallel"),
        name="ffn",
    )(x, g.reshape(1, d), w1, w3, w2)


def _norm_proj_kernel(x_ref, g_ref, w_ref, o_ref):
    h = _rms(x_ref[...], g_ref[...]).astype(BF16)
    o_ref[...] = _dot(h, w_ref[...])


def _norm_proj(x, g, w, *, tm=512):
    s, d = x.shape
    n = w.shape[1]
    tm = min(tm, s)
    return pl.pallas_call(
        _norm_proj_kernel,
        grid=(s // tm,),
        in_specs=[pl.BlockSpec((tm, d), lambda i: (i, 0)), _const_spec((1, d)), _const_spec((d, n))],
        out_specs=pl.BlockSpec((tm, n), lambda i: (i, 0)),
        out_shape=jax.ShapeDtypeStruct((s, n), F32),
        compiler_params=_cparams("parallel"),
        name="norm_proj",
    )(x, g.reshape(1, d), w.astype(BF16))


def _out_proj_kernel(x_ref, mix_ref, mem_ref, wa_ref, wb_ref, o_ref):
    o_ref[...] = (x_ref[...] + _dot(mix_ref[...].astype(BF16), wa_ref[...])
                  + _dot(mem_ref[...].astype(BF16), wb_ref[...]))


def _out_proj(x, mix, mem_out, w_out, *, tm=512):
    s, d = x.shape
    na, nb = mix.shape[1], mem_out.shape[1]
    tm = min(tm, s)
    w = w_out.astype(BF16)
    return pl.pallas_call(
        _out_proj_kernel,
        grid=(s // tm,),
        in_specs=[pl.BlockSpec((tm, d), lambda i: (i, 0)),
                  pl.BlockSpec((tm, na), lambda i: (i, 0)),
                  pl.BlockSpec((tm, nb), lambda i: (i, 0)),
                  _const_spec((na, d)), _const_spec((nb, d))],
        out_specs=pl.BlockSpec((tm, d), lambda i: (i, 0)),
        out_shape=jax.ShapeDtypeStruct((s, d), F32),
        compiler_params=_cparams("parallel"),
        name="out_proj",
    )(x, mix, mem_out, w[:na], w[na:])


def _mem_kv_kernel(mem_ref, g_ref, w_ref, kn_ref, k_ref, v_ref):
    h = _rms(mem_ref[...], g_ref[...]).astype(BF16)
    kv = _dot(h, w_ref[...])
    width = k_ref.shape[1]
    for p in range(width // LANES):
        sl = slice(p * LANES, (p + 1) * LANES)
        k_ref[:, sl] = _head_rms(kv[:, sl], kn_ref[...])
    v_ref[...] = kv[:, width:]


def _mem_kv(mem, g, w, k_norm):
    m, d = mem.shape
    width = w.shape[1] // 2
    kn = jnp.tile(k_norm, LANES // HEAD).reshape(1, LANES)
    return pl.pallas_call(
        _mem_kv_kernel,
        grid=(1,),
        in_specs=[_const_spec((m, d)), _const_spec((1, d)), _const_spec((d, 2 * width)), _const_spec((1, LANES))],
        out_specs=[pl.BlockSpec((m, width), lambda i: (0, 0)), pl.BlockSpec((m, width), lambda i: (0, 0))],
        out_shape=[jax.ShapeDtypeStruct((m, width), F32)] * 2,
        compiler_params=_cparams("arbitrary"),
        name="mem_kv",
    )(mem, g.reshape(1, d), w.astype(BF16), kn)


def _mem_attn_kernel(q_ref, qn_ref, k_ref, v_ref, o_ref):
    lane = lax.broadcasted_iota(jnp.int32, (1, LANES), 1)
    width = q_ref.shape[1]
    for p in range(width // LANES):
        sl = slice(p * LANES, (p + 1) * LANES)
        q = _head_rms(q_ref[:, sl], qn_ref[...]) * (HEAD ** -0.5)
        k = k_ref[:, sl].astype(BF16)
        v = v_ref[:, sl].astype(BF16)
        out = None
        for h in range(LANES // HEAD):
            in_head = (lane // HEAD) == h
            qh = jnp.where(in_head, q, 0.0).astype(BF16)
            s = _dot_nt(qh, k)
            s = s - jnp.max(s, axis=-1, keepdims=True)
            e = jnp.exp(s)
            pr = e / jnp.sum(e, axis=-1, keepdims=True)
            oh = _dot(pr.astype(BF16), v)
            out = oh if out is None else jnp.where(in_head, oh, out)
        o_ref[:, sl] = out


def _mem_attn(proj, col_block, q_norm, k, v, *, tm=512):
    s = proj.shape[0]
    m, width = k.shape
    tm = min(tm, s)
    qn = jnp.tile(q_norm, LANES // HEAD).reshape(1, LANES)
    return pl.pallas_call(
        _mem_attn_kernel,
        grid=(s // tm,),
        in_specs=[pl.BlockSpec((tm, width), lambda i: (i, col_block)),
                  _const_spec((1, LANES)), _const_spec((m, width)), _const_spec((m, width))],
        out_specs=pl.BlockSpec((tm, width), lambda i: (i, 0)),
        out_shape=jax.ShapeDtypeStruct((s, width), F32),
        compiler_params=_cparams("parallel"),
        name="mem_attn",
    )(proj, qn, k, v)


def _stack_heads(z):
    lane = lax.broadcasted_iota(jnp.int32, z.shape, 2)
    zero = jnp.zeros_like(z)
    return jnp.concatenate([jnp.where(lane < HEAD, z, zero), jnp.where(lane >= HEAD, z, zero)], axis=1)


def _bdot(a, b):
    return lax.dot_general(a, b, (((2,), (1,)), ((0,), (0,))), preferred_element_type=F32)


def _bdot_nt(a, b):
    return lax.dot_general(a, b, (((2,), (2,)), ((0,), (0,))), preferred_element_type=F32)


def _bdot_tn(a, b):
    return lax.dot_general(a, b, (((1,), (1,)), ((0,), (0,))), preferred_element_type=F32)


def _rwkv_kernel(proj_ref, mu_ref, wlh_ref, wll_ref, g2_ref, w0_ref, a0_ref, kk_ref, ka_ref, rk_ref, lnw_ref, lnb_ref,
                 o_ref, last_ref, h_ref, r_s, k_s, v_s, a_s, lw_s, cum_s, g_s):
    c = CHUNK
    rows = proj_ref.shape[0]
    n_sub = rows // c
    n_pairs = h_ref.shape[0]
    width = n_pairs * LANES

    @pl.when(pl.program_id(0) == 0)
    def _():
        last_ref[...] = jnp.zeros_like(last_ref)
        h_ref[...] = jnp.zeros_like(h_ref)

    u = proj_ref[...]
    row = lax.broadcasted_iota(jnp.int32, u.shape, 0)
    prev = jnp.where(row == 0, last_ref[...], pltpu.roll(u, 1, 0))
    last_ref[...] = u[rows - 1:rows, :]
    u = u + (prev - u) * mu_ref[...]

    lora_in = u[:, 3 * width:3 * width + LANES]
    lane = lax.broadcasted_iota(jnp.int32, lora_in.shape, 1)
    lora_in = jnp.where(lane < HEAD, jnp.tanh(lora_in), lora_in)
    x_hi, x_lo = _split_bf16(lora_in)
    lo = _dot(x_hi, wlh_ref[...]) + (_dot(x_lo, wlh_ref[...]) + _dot(x_hi, wll_ref[...]))
    z = -(w0_ref[...] + lo[:, :width])
    softplus = jnp.maximum(z, 0.0) + jnp.log(1.0 + jnp.exp(-jnp.abs(z)))
    lw = -jnp.exp(-softplus - 0.5)
    a = jax.nn.sigmoid(a0_ref[...] + lo[:, width:])
    g = _dot(_b16(jax.nn.sigmoid(u[:, 3 * width + LANES:])), g2_ref[...])
    ti = lax.broadcasted_iota(jnp.int32, (rows, rows), 0)
    si = lax.broadcasted_iota(jnp.int32, (rows, rows), 1)
    tri = _b16(jnp.where((si <= ti) & (si // c == ti // c), F32(1.0), F32(0.0)))
    lw_hi, lw_lo = _split_bf16(lw)
    cum = _dot(tri, lw_hi) + _dot(tri, lw_lo)

    for sub in range(n_sub):
        rs = slice(sub * c, (sub + 1) * c)
        for p in range(n_pairs):
            b_ = sub * n_pairs + p
            sl = slice(p * LANES, (p + 1) * LANES)
            r_s[b_] = u[rs, sl]
            k_s[b_] = u[rs, width + p * LANES:width + (p + 1) * LANES]
            v_s[b_] = u[rs, 2 * width + p * LANES:2 * width + (p + 1) * LANES]
            a_s[b_] = a[rs, sl]
            lw_s[b_] = lw[rs, sl]
            cum_s[b_] = cum[rs, sl]
            g_s[b_] = g[rs, sl]

    np_ = n_sub * n_pairs

    def per_pair(ref):
        return jnp.concatenate([ref[...]] * n_sub, axis=0)
    hsum = _b16(_head_sum_matrix())

    def head_sum(z):
        return _dot(_b16(z).reshape(np_ * c, LANES), hsum).reshape(np_, c, LANES)

    col = lax.broadcasted_iota(jnp.int32, (np_, c, LANES), 2) % c
    trow = lax.broadcasted_iota(jnp.int32, (np_, c, LANES), 1)
    strict = col < trow
    incl = col <= trow
    bi = lax.broadcasted_iota(jnp.int32, (np_, LANES, LANES), 1)
    bj = lax.broadcasted_iota(jnp.int32, (np_, LANES, LANES), 2)
    eye = bi == bj
    same_head = (bi // HEAD) == (bj // HEAD)

    r, k, v, a_, lw_, cum_ = r_s[...], k_s[...], v_s[...], a_s[...], lw_s[...], cum_s[...]
    kk = k * per_pair(kk_ref)
    kk = kk / jnp.maximum(jnp.sqrt(head_sum(kk * kk)), 1e-12)
    kmod = k * (1.0 + (a_ - 1.0) * per_pair(ka_ref))
    b = kk * a_
    bonus = head_sum(r * kmod * per_pair(rk_ref)) * v

    tot = cum_[:, c - 1:c, :]
    at = -kk * jnp.exp(cum_ - lw_)
    rt = r * jnp.exp(cum_)
    einv = jnp.exp(-cum_)
    eend = jnp.exp(tot - cum_)
    bh, kh = _b16(b * einv), _b16(kmod * einv)
    bt, kt = _b16(b * eend), _b16(kmod * eend)
    at_b = _b16(at)

    aa = _bdot_nt(jnp.concatenate([at_b, _b16(rt)], axis=1),
                  jnp.concatenate([_stack_heads(bh), _stack_heads(kh)], axis=1))
    a_ab = jnp.where(strict, aa[:, :c, :LANES], 0.0)
    a_ak = _b16(jnp.where(strict, aa[:, :c, LANES:], 0.0))
    a_rb = _b16(jnp.where(incl, aa[:, c:, :LANES], 0.0))
    a_rk = _b16(jnp.where(incl, aa[:, c:, LANES:], 0.0))

    t_side = jnp.where(col == trow, 1.0, 0.0) + jnp.where((trow == col + 1) & (trow % 2 == 1), a_ab, 0.0)
    size = 2
    while size < c:
        e_mask = ((trow // (2 * size)) == (col // (2 * size))) & (trow % (2 * size) >= size) & (col % (2 * size) < size)
        t_b = _b16(t_side)
        t_blk = _stack_heads(t_b)
        te = _bdot(t_b, _stack_heads(_b16(jnp.where(e_mask, a_ab, 0.0))))
        t_side = t_side + _bdot(_b16(te), t_blk)
        size *= 2
    t_pair = _b16(t_side)

    v_b = _b16(v)
    v_st = _stack_heads(v_b)
    g0 = _bdot(a_ak, v_st)
    uw = _bdot(t_pair, jnp.concatenate([_stack_heads(_b16(g0)), _stack_heads(at_b)], axis=2))
    u0, w = _b16(uw[:, :, :LANES]), _b16(uw[:, :, LANES:])
    rw = rt + _bdot(a_rb, _stack_heads(w))
    y0 = _bdot(a_rb, _stack_heads(u0)) + _bdot(a_rk, v_st)
    m1 = jnp.where(eye, jnp.exp(tot), 0.0) + jnp.where(same_head, _bdot_tn(bt, w), 0.0)
    m0 = jnp.where(same_head, _bdot_tn(bt, u0) + _bdot_tn(kt, v_b), 0.0)

    rw_b, m1_b = _b16(rw), _b16(m1)
    h = h_ref[...]
    ys = []
    for sub in range(n_sub):
        sl = slice(sub * n_pairs, (sub + 1) * n_pairs)
        h_b = _b16(h)
        ys.append(_bdot(rw_b[sl], h_b) + y0[sl])
        h = _bdot(m1_b[sl], h_b) + m0[sl]
    h_ref[...] = h
    y = jnp.concatenate(ys, axis=0)

    mean = head_sum(y) * (1.0 / HEAD)
    yc = y - mean
    var = head_sum(yc * yc) * (1.0 / HEAD)
    yn = yc * lax.rsqrt(var + GN_EPS) * per_pair(lnw_ref) + per_pair(lnb_ref)
    out = (yn + bonus) * g_s[...]
    for sub in range(n_sub):
        for p in range(n_pairs):
            o_ref[sub * c:(sub + 1) * c, p * LANES:(p + 1) * LANES] = out[sub * n_pairs + p]


def _rwkv(proj, mu, w0, w2, a0, a2, g2, k_k, k_a, r_k, ln_w, ln_b):
    s = proj.shape[0]
    width = w0.shape[0]
    n_pairs = width // LANES
    shift_w = mu.shape[0]
    dl, al = w2.shape[0], a2.shape[0]
    assert dl + al == LANES and g2.shape[0] == LANES and shift_w == 3 * width + 2 * LANES
    wl = jnp.zeros((LANES, 2 * width), F32).at[:dl, :width].set(w2).at[dl:, width:].set(a2)
    wl_hi, wl_lo = _split_bf16(wl)

    def per_pair(t):
        return t.reshape(n_pairs, 1, LANES)

    c = CHUNK
    rows = c * RWKV_CHUNKS_PER_STEP
    pair_spec = _const_spec((n_pairs, 1, LANES))
    return pl.pallas_call(
        _rwkv_kernel,
        grid=(s // rows,),
        in_specs=[pl.BlockSpec((rows, shift_w), lambda i: (i, 0)),
                  _const_spec((1, shift_w)), _const_spec((LANES, 2 * width)), _const_spec((LANES, 2 * width)),
                  _const_spec((LANES, width)),
                  _const_spec((1, width)), _const_spec((1, width)),
                  pair_spec, pair_spec, pair_spec, pair_spec, pair_spec],
        out_specs=pl.BlockSpec((rows, width), lambda i: (i, 0)),
        out_shape=jax.ShapeDtypeStruct((s, width), F32),
        scratch_shapes=[pltpu.VMEM((1, shift_w), F32), pltpu.VMEM((n_pairs, LANES, LANES), F32)]
                       + [pltpu.VMEM((RWKV_CHUNKS_PER_STEP * n_pairs, c, LANES), F32)] * 7,
        compiler_params=_cparams("arbitrary"),
        name="rwkv7",
    )(proj, mu.reshape(1, shift_w), wl_hi, wl_lo, g2.astype(BF16), w0.reshape(1, width), a0.reshape(1, width),
      per_pair(k_k), per_pair(k_a), per_pair(r_k.reshape(-1)), per_pair(ln_w), per_pair(ln_b))


def _shared_kv_kernel(x_ref, g_ref, wk_ref, wvt_ref, kn_ref, k_ref, vt_ref, km_ref):
    h = _rms(x_ref[...], g_ref[...]).astype(BF16)
    k = _dot(h, wk_ref[...])
    n_pairs = k_ref.shape[0]
    for p in range(n_pairs):
        kp = _head_rms(k[:, p * LANES:(p + 1) * LANES], kn_ref[...])
        k_ref[p, 0] = kp.astype(BF16)
        km_ref[0, :, p * LANES:(p + 1) * LANES] = jnp.mean(kp, axis=0, keepdims=True)
    vt = _dot_nt(wvt_ref[...], h)
    vt_ref[:, 0] = vt.reshape(n_pairs, LANES, vt.shape[1]).astype(BF16)


def _shared_kv(x, g, w_kv, k_norm):
    s, d = x.shape
    width = w_kv.shape[1] // 2
    n_pairs = width // LANES
    blk = MOBA_BLOCK
    nb = s // blk
    w = w_kv.astype(BF16)
    kn = jnp.tile(k_norm, LANES // HEAD).reshape(1, LANES)
    k, vt, km = pl.pallas_call(
        _shared_kv_kernel,
        grid=(nb,),
        in_specs=[pl.BlockSpec((blk, d), lambda i: (i, 0)), _const_spec((1, d)),
                  _const_spec((d, width)), _const_spec((width, d)), _const_spec((1, LANES))],
        out_specs=[pl.BlockSpec((n_pairs, 1, blk, LANES), lambda i: (0, i, 0, 0)),
                   pl.BlockSpec((n_pairs, 1, LANES, blk), lambda i: (0, i, 0, 0)),
                   pl.BlockSpec((1, 1, width), lambda i: (i, 0, 0))],
        out_shape=[jax.ShapeDtypeStruct((n_pairs, nb, blk, LANES), BF16),
                   jax.ShapeDtypeStruct((n_pairs, nb, LANES, blk), BF16),
                   jax.ShapeDtypeStruct((nb, 1, width), F32)],
        compiler_params=_cparams("parallel"),
        name="shared_kv",
    )(x, g.reshape(1, d), w[:, :width], w[:, width:].T, kn)
    k_bound = (HEAD ** 0.5) * jnp.max(jnp.abs(k_norm)).reshape(1)
    return k, vt, km.reshape(nb, width), k_bound


def _moba_kernel(slopes_ref, depth_ref, kbound_ref, q_ref, qn_ref, km_ref, k_ref, vt_ref, o_ref,
                 sel_ref, fac_ref, ownb_ref, base_ref, sa_ref, sb_ref):
    p = pl.program_id(0)
    i = pl.program_id(1)
    blk = MOBA_BLOCK
    nb = km_ref.shape[0]
    col = lax.broadcasted_iota(jnp.int32, (1, 2 * blk), 1)
    slope = jnp.where(col < blk, slopes_ref[2 * p], slopes_ref[2 * p + 1]) * LOG2E
    pad = slope * (blk - 1)

    @pl.when(i == 0)
    def _():
        key_pos = lax.broadcasted_iota(jnp.int32, (blk, 2 * blk), 0)
        qry_pos = lax.broadcasted_iota(jnp.int32, (blk, 2 * blk), 1) % blk
        dist = (qry_pos - key_pos).astype(F32)
        ownb_ref[...] = jnp.where(dist >= 0.0, -slope * dist, NEG_INF)
        base_ref[...] = -slope * dist - pad

    q = _head_rms(q_ref[...], qn_ref[...]) * (HEAD ** -0.5)
    lane = lax.broadcasted_iota(jnp.int32, (1, LANES), 1)
    q2 = jnp.concatenate([jnp.where(lane < HEAD, q, 0.0), jnp.where(lane >= HEAD, q, 0.0)], axis=0)
    qb = (q2 * LOG2E).astype(BF16)

    start = jnp.maximum(i - depth_ref[p] + 1, 0)
    n_iter = (i - start + 3) // 4

    def block_of(g, j):
        return jnp.minimum(start + 2 * g + j, nb - 1)

    def fast_scores(g, s_ref):
        for j in range(2):
            s_ref[j] = _dot_nt(k_ref[block_of(g, j)], qb)

    fast_scores(0, sa_ref)

    n_idx = lax.broadcasted_iota(jnp.int32, (nb, 2 * blk), 0)
    q_hi, q_lo = _split_bf16(q2)
    km_hi, km_lo = _split_bf16(km_ref[...])
    gate = _dot_nt(km_hi, q_hi) + (_dot_nt(km_lo, q_hi) + _dot_nt(km_hi, q_lo))
    gate = jnp.where(n_idx < i, gate, NEG_INF)
    sel_bias = jnp.full(gate.shape, NEG_INF, F32)
    for _ in range(MOBA_TOPK):
        mx = jnp.max(gate, axis=0, keepdims=True)
        first = jnp.min(jnp.where(gate == mx, n_idx, nb), axis=0, keepdims=True)
        pick = (n_idx == first) & (mx > NEG_INF)
        sel_bias = jnp.where(pick, 0.0, sel_bias)
        gate = jnp.where(pick, NEG_INF, gate)
    sel_ref[...] = sel_bias - slope * ((i - n_idx) * blk).astype(F32)

    fac_ref[...] = jnp.exp2(sel_ref[...] + pad)

    def weighted_values(n, e):
        eb = e.astype(BF16)
        return jnp.concatenate([_dot(vt_ref[n, :HEAD, :], eb[:, :blk]), _dot(vt_ref[n, HEAD:, :], eb[:, blk:])],
                               axis=1)

    e = jnp.exp2(_dot_nt(k_ref[i], qb) + ownb_ref[...])
    l_own = jnp.sum(e, axis=0, keepdims=True)
    acc_own = weighted_values(i, e)

    def fast_consume(g, s_ref, l, acc):
        for j in range(2):
            n = block_of(g, j)
            e = jnp.exp2(s_ref[j] + base_ref[...])
            f = fac_ref[pl.ds(n, 1), :]
            l = l + f * jnp.sum(e, axis=0, keepdims=True)
            acc = acc + f * weighted_values(n, e)
        return l, acc

    def fast_two_groups(u, carry):
        fast_scores(2 * u + 1, sb_ref)
        carry = fast_consume(2 * u, sa_ref, *carry)
        fast_scores(2 * u + 2, sa_ref)
        return fast_consume(2 * u + 1, sb_ref, *carry)

    l, acc = lax.fori_loop(0, n_iter, fast_two_groups, (l_own, acc_own))

    def emit(l, acc):
        out = acc / l
        o_ref[...] = jnp.concatenate([out[:, :blk], out[:, blk:]], axis=0).T

    emit(l, acc)

    q_len = jnp.sqrt(_dot_nt(jnp.ones((8, LANES), BF16), _b16(q2 * q2))[0:1])
    score_bound = q_len * (kbound_ref[0] * (LOG2E * BF16_SLACK))
    redo = jnp.max(jnp.where(score_bound < EXP_HEADROOM, 0.0, 1.0))
    redo = jnp.maximum(redo, jnp.maximum(jnp.max(jnp.where(jnp.abs(l) < jnp.inf, 0.0, 1.0)),
                                         jnp.max(jnp.where(jnp.abs(acc) < jnp.inf, 0.0, 1.0))))

    def any_block_of(g, j):
        return jnp.minimum(2 * g + j, nb - 1)

    def score_group(g, s_ref):
        col_max = []
        for j in range(2):
            n = any_block_of(g, j)
            s = _dot_nt(k_ref[n], qb) + (base_ref[...] + pad)
            s_ref[j] = s
            col_max.append(jnp.max(s, axis=0, keepdims=True) + sel_ref[pl.ds(n, 1), :])
        return col_max

    def consume_group(g, s_ref, col_max, m, l, acc):
        n0, n1 = any_block_of(g, 0), any_block_of(g, 1)
        m_new = jnp.maximum(m, jnp.maximum(col_max[0], col_max[1]))
        alpha = jnp.exp2(m - m_new)
        l = alpha * l
        acc = alpha * acc
        for j, n in enumerate((n0, n1)):
            e = jnp.exp2(s_ref[j] - (m_new - sel_ref[pl.ds(n, 1), :]))
            l = l + jnp.sum(e, axis=0, keepdims=True)
            acc = acc + weighted_values(n, e)
        return m_new, l, acc

    @pl.when(redo > 0.0)
    def _():
        max_a = score_group(0, sa_ref)
        s = _dot_nt(k_ref[i], qb) + ownb_ref[...]
        m_own = jnp.max(s, axis=0, keepdims=True)
        e = jnp.exp2(s - m_own)
        l_own = jnp.sum(e, axis=0, keepdims=True)
        acc_own = weighted_values(i, e)

        def two_groups(u, carry):
            max_a0, max_a1, m, l, acc = carry
            max_b = score_group(2 * u + 1, sb_ref)
            m, l, acc = consume_group(2 * u, sa_ref, (max_a0, max_a1), m, l, acc)
            max_a = score_group(2 * u + 2, sa_ref)
            m, l, acc = consume_group(2 * u + 1, sb_ref, max_b, m, l, acc)
            return max_a[0], max_a[1], m, l, acc

        init = (max_a[0], max_a[1], m_own, l_own, acc_own)
        _, _, _, l, acc = lax.fori_loop(0, (i + 3) // 4, two_groups, init)
        emit(l, acc)


def _moba(proj, q_norm, k, vt, km, k_bound):
    s = proj.shape[0]
    n_pairs, nb, blk, _ = k.shape
    heads = n_pairs * (LANES // HEAD)
    slopes = jnp.exp2(-ALIBI_MAX * jnp.arange(1, heads + 1, dtype=F32) / heads)
    min_slope = [2.0 ** (-ALIBI_MAX * (2 * p + 2) / heads) for p in range(n_pairs)]
    depth = jnp.asarray([math.ceil((DEAD_EXPONENT / (sl * LOG2E) + blk - 1) / blk) for sl in min_slope], jnp.int32)
    qn = jnp.tile(q_norm, LANES // HEAD).reshape(1, LANES)
    smem = pl.BlockSpec(memory_space=pltpu.SMEM)
    return pl.pallas_call(
        _moba_kernel,
        grid=(n_pairs, nb),
        in_specs=[smem, smem, smem,
                  pl.BlockSpec((blk, LANES), lambda p, i: (i, p)),
                  pl.BlockSpec((1, LANES), lambda p, i: (0, 0)),
                  pl.BlockSpec((nb, LANES), lambda p, i: (0, p)),
                  pl.BlockSpec((None, nb, blk, LANES), lambda p, i: (p, 0, 0, 0)),
                  pl.BlockSpec((None, nb, LANES, blk), lambda p, i: (p, 0, 0, 0))],
        out_specs=pl.BlockSpec((blk, LANES), lambda p, i: (i, p)),
        out_shape=jax.ShapeDtypeStruct((s, n_pairs * LANES), F32),
        scratch_shapes=[pltpu.VMEM((nb, 2 * blk), F32), pltpu.VMEM((nb, 2 * blk), F32),
                        pltpu.VMEM((blk, 2 * blk), F32), pltpu.VMEM((blk, 2 * blk), F32),
                        pltpu.VMEM((2, blk, 2 * blk), F32), pltpu.VMEM((2, blk, 2 * blk), F32)],
        compiler_params=_cparams("parallel", "arbitrary"),
        name="moba",
    )(slopes, depth, k_bound, proj, qn, km, k, vt)


def kernel(x, mem, ffn1_norm, ffn1_w1, ffn1_w3, ffn1_w2, mix_norm, w_out, mem_norm, w_mem_kv, mem_q_norm, mem_k_norm, ffn2_norm, ffn2_w1, ffn2_w3, ffn2_w2, rwkv_w_in, rwkv_mu, rwkv_w0, rwkv_w2, rwkv_a0, rwkv_a2, rwkv_g2, rwkv_k_k, rwkv_k_a, rwkv_r_k, rwkv_ln_w, rwkv_ln_b, kv_norm, w_kv, kv_k_norm, moba_w_in, moba_q_norm):
    batch = x.shape[0]
    depth = ffn1_norm.shape[0]
    n_a = rwkv_w_in.shape[0]
    mix_w = rwkv_w0.shape[1]
    mem_w = w_mem_kv.shape[2] // 2
    outs = []
    for bi in range(batch):
        xb, memb = x[bi], mem[bi]
        shared = None
        for l in range(depth):
            xb = _ffn(xb, ffn1_norm[l], ffn1_w1[l], ffn1_w3[l], ffn1_w2[l])
            if l < n_a:
                proj = _norm_proj(xb, mix_norm[l], rwkv_w_in[l])
                mix = _rwkv(proj, rwkv_mu[l], rwkv_w0[l], rwkv_w2[l], rwkv_a0[l], rwkv_a2[l], rwkv_g2[l],
                            rwkv_k_k[l], rwkv_k_a[l], rwkv_r_k[l], rwkv_ln_w[l], rwkv_ln_b[l])
                mem_col = rwkv_mu.shape[1] // mem_w
            else:
                j = l - n_a
                proj = _norm_proj(xb, mix_norm[l], moba_w_in[j])
                mix = _moba(proj, moba_q_norm[j], *shared)
                mem_col = mix_w // mem_w
            mk, mv = _mem_kv(memb, mem_norm[l], w_mem_kv[l], mem_k_norm[l])
            mem_out = _mem_attn(proj, mem_col, mem_q_norm[l], mk, mv)
            xb = _out_proj(xb, mix, mem_out, w_out[l])
            xb = _ffn(xb, ffn2_norm[l], ffn2_w1[l], ffn2_w3[l], ffn2_w2[l])
            if l == n_a - 1:
                shared = _shared_kv(xb, kv_norm, w_kv, kv_k_norm)
        outs.append(xb)
    return jnp.stack(outs)
```

```python
import functools
import math

import jax
import jax.numpy as jnp
from jax import lax
from jax.experimental import pallas as pl
from jax.experimental.pallas import tpu as pltpu

F32 = jnp.float32
BF16 = jnp.bfloat16
HI = lax.Precision.HIGHEST

HEAD = 64
LANES = 128
MXU = 256
CHUNK = 64
RWKV_CHUNKS_PER_STEP = 4
MOBA_BLOCK = 256
MOBA_TOPK = 3
ALIBI_MAX = 8.0
NORM_EPS = 1e-6
GN_EPS = 64e-5
VMEM_LIMIT = 56 * 1024 * 1024
NEG_INF = float("-inf")
LOG2E = 1.4426950408889634
DEAD_EXPONENT = 160.0
EXP_HEADROOM = 80.0
BF16_SLACK = 1.02


def _cparams(*sem):
    return pltpu.CompilerParams(dimension_semantics=sem, vmem_limit_bytes=VMEM_LIMIT)


def _const_spec(shape):
    return pl.BlockSpec(shape, lambda *_: (0,) * len(shape), pipeline_mode=pl.Buffered(1))


def _rms(x, g):
    return x * lax.rsqrt(jnp.mean(x * x, axis=-1, keepdims=True) + NORM_EPS) * g


def _b16(x):
    return x.astype(BF16)


def _split_bf16(x):
    hi = x.astype(BF16)
    return hi, (x - hi.astype(F32)).astype(BF16)


def _dot(a, b, precision=None):
    return jnp.dot(a, b, preferred_element_type=F32, precision=precision)


def _dot_nt(a, b, precision=None):
    return lax.dot_general(a, b, (((1,), (1,)), ((), ())), preferred_element_type=F32, precision=precision)


def _dot_tn(a, b, precision=None):
    return lax.dot_general(a, b, (((0,), (0,)), ((), ())), preferred_element_type=F32, precision=precision)


def _head_sum_matrix(scale=1.0):
    i = lax.broadcasted_iota(jnp.int32, (LANES, LANES), 0) // HEAD
    j = lax.broadcasted_iota(jnp.int32, (LANES, LANES), 1) // HEAD
    return jnp.where(i == j, F32(scale), F32(0.0))


def _head_rms(x, g):
    hi, lo = _split_bf16(x * x)
    mean_matrix = _b16(_head_sum_matrix(1.0 / HEAD))
    ms = _dot(hi, mean_matrix) + _dot(lo, mean_matrix)
    return x * lax.rsqrt(ms + NORM_EPS) * g


def _swiglu_half_step(x, g_ref, w1_ref, w3_ref, w2_ref, acc_ref, f_chunk):
    h = _rms(x, g_ref[...]).astype(BF16)
    n_chunks = w1_ref.shape[1] // f_chunk
    for c in range(n_chunks):
        sl = slice(c * f_chunk, (c + 1) * f_chunk)
        a = _dot(h, w1_ref[:, sl])
        b = _dot(h, w3_ref[:, sl])
        act = (a * jax.nn.sigmoid(a) * b).astype(BF16)
        part = _dot(act, w2_ref[sl, :])
        if c == 0:
            acc_ref[...] = part
        else:
            acc_ref[...] += part
    return x + 0.5 * acc_ref[...]


def _ffn_kernel(x_ref, g_ref, w1_ref, w3_ref, w2_ref, o_ref, acc_ref, *, f_chunk):
    o_ref[...] = _swiglu_half_step(x_ref[...], g_ref, w1_ref, w3_ref, w2_ref, acc_ref, f_chunk)


def _mix_ffn_kernel(x_ref, mix_ref, mem_ref, wo_ref, g_ref, w1_ref, w3_ref, w2_ref, o_ref, acc_ref, *, f_chunk):
    na = mix_ref.shape[1]
    x = (x_ref[...] + _dot(mix_ref[...].astype(BF16), wo_ref[:na, :])
         + _dot(mem_ref[...].astype(BF16), wo_ref[na:, :]))
    o_ref[...] = _swiglu_half_step(x, g_ref, w1_ref, w3_ref, w2_ref, acc_ref, f_chunk)


def _ffn(x, g, w1, w3, w2, mixer=None, *, tm=512, f_chunk=MXU):
    s, d = x.shape
    f = w1.shape[1]
    fp = -(-f // f_chunk) * f_chunk
    w1 = jnp.pad(w1, ((0, 0), (0, fp - f))).astype(BF16)
    w3 = jnp.pad(w3, ((0, 0), (0, fp - f))).astype(BF16)
    w2 = jnp.pad(w2, ((0, fp - f), (0, 0))).astype(BF16)
    tm = min(tm, s)
    row_spec = pl.BlockSpec((tm, d), lambda i: (i, 0))
    ffn_specs = [_const_spec((1, d)), _const_spec((d, fp)), _const_spec((d, fp)), _const_spec((fp, d))]
    ffn_args = (g.reshape(1, d), w1, w3, w2)
    if mixer is None:
        body, specs, args = _ffn_kernel, [row_spec] + ffn_specs, (x,) + ffn_args
    else:
        mix, mem_out, w_out = mixer
        na, nb = mix.shape[1], mem_out.shape[1]
        body = _mix_ffn_kernel
        specs = [row_spec, pl.BlockSpec((tm, na), lambda i: (i, 0)), pl.BlockSpec((tm, nb), lambda i: (i, 0)),
                 _const_spec((na + nb, d))] + ffn_specs
        args = (x, mix, mem_out, w_out.astype(BF16)) + ffn_args
    return pl.pallas_call(
        functools.partial(body, f_chunk=f_chunk),
        grid=(s // tm,),
        in_specs=specs,
        out_specs=row_spec,
        out_shape=jax.ShapeDtypeStruct((s, d), F32),
        scratch_shapes=[pltpu.VMEM((tm, d), F32)],
        compiler_params=_cparams("parallel"),
        name="ffn" if mixer is None else "mix_ffn",
    )(*args)


def _norm_proj_kernel(x_ref, g_ref, w_ref, o_ref):
    h = _rms(x_ref[...], g_ref[...]).astype(BF16)
    o_ref[...] = _dot(h, w_ref[...])


def _norm_proj(x, g, w, *, tm=512):
    s, d = x.shape
    n = w.shape[1]
    tm = min(tm, s)
    return pl.pallas_call(
        _norm_proj_kernel,
        grid=(s // tm,),
        in_specs=[pl.BlockSpec((tm, d), lambda i: (i, 0)), _const_spec((1, d)), _const_spec((d, n))],
        out_specs=pl.BlockSpec((tm, n), lambda i: (i, 0)),
        out_shape=jax.ShapeDtypeStruct((s, n), F32),
        compiler_params=_cparams("parallel"),
        name="norm_proj",
    )(x, g.reshape(1, d), w.astype(BF16))


def _mem_kv_kernel(mem_ref, g_ref, w_ref, kn_ref, k_ref, v_ref):
    h = _rms(mem_ref[...], g_ref[...]).astype(BF16)
    kv = _dot(h, w_ref[...])
    width = k_ref.shape[1]
    for p in range(width // LANES):
        sl = slice(p * LANES, (p + 1) * LANES)
        k_ref[:, sl] = _head_rms(kv[:, sl], kn_ref[...])
    v_ref[...] = kv[:, width:]


def _mem_kv(mem, g, w, k_norm):
    m, d = mem.shape
    width = w.shape[1] // 2
    kn = jnp.tile(k_norm, LANES // HEAD).reshape(1, LANES)
    return pl.pallas_call(
        _mem_kv_kernel,
        grid=(1,),
        in_specs=[_const_spec((m, d)), _const_spec((1, d)), _const_spec((d, 2 * width)), _const_spec((1, LANES))],
        out_specs=[pl.BlockSpec((m, width), lambda i: (0, 0)), pl.BlockSpec((m, width), lambda i: (0, 0))],
        out_shape=[jax.ShapeDtypeStruct((m, width), F32)] * 2,
        compiler_params=_cparams("arbitrary"),
        name="mem_kv",
    )(mem, g.reshape(1, d), w.astype(BF16), kn)


def _mem_attn_kernel(q_ref, qn_ref, k_ref, v_ref, o_ref):
    lane = lax.broadcasted_iota(jnp.int32, (1, LANES), 1)
    width = q_ref.shape[1]
    for p in range(width // LANES):
        sl = slice(p * LANES, (p + 1) * LANES)
        q = _head_rms(q_ref[:, sl], qn_ref[...]) * (HEAD ** -0.5)
        k = k_ref[:, sl].astype(BF16)
        v = v_ref[:, sl].astype(BF16)
        out = None
        for h in range(LANES // HEAD):
            in_head = (lane // HEAD) == h
            qh = jnp.where(in_head, q, 0.0).astype(BF16)
            s = _dot_nt(qh, k)
            s = s - jnp.max(s, axis=-1, keepdims=True)
            e = jnp.exp(s)
            pr = e / jnp.sum(e, axis=-1, keepdims=True)
            oh = _dot(pr.astype(BF16), v)
            out = oh if out is None else jnp.where(in_head, oh, out)
        o_ref[:, sl] = out


def _mem_attn(proj, col_block, q_norm, k, v, *, tm=512):
    s = proj.shape[0]
    m, width = k.shape
    tm = min(tm, s)
    qn = jnp.tile(q_norm, LANES // HEAD).reshape(1, LANES)
    return pl.pallas_call(
        _mem_attn_kernel,
        grid=(s // tm,),
        in_specs=[pl.BlockSpec((tm, width), lambda i: (i, col_block)),
                  _const_spec((1, LANES)), _const_spec((m, width)), _const_spec((m, width))],
        out_specs=pl.BlockSpec((tm, width), lambda i: (i, 0)),
        out_shape=jax.ShapeDtypeStruct((s, width), F32),
        compiler_params=_cparams("parallel"),
        name="mem_attn",
    )(proj, qn, k, v)


def _stack_heads(z):
    lane = lax.broadcasted_iota(jnp.int32, z.shape, 2)
    zero = jnp.zeros_like(z)
    return jnp.concatenate([jnp.where(lane < HEAD, z, zero), jnp.where(lane >= HEAD, z, zero)], axis=1)


def _bdot(a, b):
    return lax.dot_general(a, b, (((2,), (1,)), ((0,), (0,))), preferred_element_type=F32)


def _bdot_nt(a, b):
    return lax.dot_general(a, b, (((2,), (2,)), ((0,), (0,))), preferred_element_type=F32)


def _bdot_tn(a, b):
    return lax.dot_general(a, b, (((1,), (1,)), ((0,), (0,))), preferred_element_type=F32)


def _rwkv_kernel(proj_ref, mu_ref, wlh_ref, wll_ref, g2_ref, w0_ref, a0_ref, kk_ref, ka_ref, rk_ref, lnw_ref, lnb_ref,
                 o_ref, last_ref, h_ref, r_s, k_s, v_s, a_s, lw_s, cum_s, g_s):
    c = CHUNK
    rows = proj_ref.shape[0]
    n_sub = rows // c
    n_pairs = h_ref.shape[0]
    width = n_pairs * LANES

    @pl.when(pl.program_id(0) == 0)
    def _():
        last_ref[...] = jnp.zeros_like(last_ref)
        h_ref[...] = jnp.zeros_like(h_ref)

    u = proj_ref[...]
    row = lax.broadcasted_iota(jnp.int32, u.shape, 0)
    prev = jnp.where(row == 0, last_ref[...], pltpu.roll(u, 1, 0))
    last_ref[...] = u[rows - 1:rows, :]
    u = u + (prev - u) * mu_ref[...]

    lora_in = u[:, 3 * width:3 * width + LANES]
    lane = lax.broadcasted_iota(jnp.int32, lora_in.shape, 1)
    lora_in = jnp.where(lane < HEAD, jnp.tanh(lora_in), lora_in)
    x_hi, x_lo = _split_bf16(lora_in)
    lo = _dot(x_hi, wlh_ref[...]) + (_dot(x_lo, wlh_ref[...]) + _dot(x_hi, wll_ref[...]))
    z = -(w0_ref[...] + lo[:, :width])
    softplus = jnp.maximum(z, 0.0) + jnp.log(1.0 + jnp.exp(-jnp.abs(z)))
    lw = -jnp.exp(-softplus - 0.5)
    a = jax.nn.sigmoid(a0_ref[...] + lo[:, width:])
    g = _dot(_b16(jax.nn.sigmoid(u[:, 3 * width + LANES:])), g2_ref[...])
    ti = lax.broadcasted_iota(jnp.int32, (rows, rows), 0)
    si = lax.broadcasted_iota(jnp.int32, (rows, rows), 1)
    tri = _b16(jnp.where((si <= ti) & (si // c == ti // c), F32(1.0), F32(0.0)))
    lw_hi, lw_lo = _split_bf16(lw)
    cum = _dot(tri, lw_hi) + _dot(tri, lw_lo)

    for sub in range(n_sub):
        rs = slice(sub * c, (sub + 1) * c)
        for p in range(n_pairs):
            b_ = sub * n_pairs + p
            sl = slice(p * LANES, (p + 1) * LANES)
            r_s[b_] = u[rs, sl]
            k_s[b_] = u[rs, width + p * LANES:width + (p + 1) * LANES]
            v_s[b_] = u[rs, 2 * width + p * LANES:2 * width + (p + 1) * LANES]
            a_s[b_] = a[rs, sl]
            lw_s[b_] = lw[rs, sl]
            cum_s[b_] = cum[rs, sl]
            g_s[b_] = g[rs, sl]

    np_ = n_sub * n_pairs

    def per_pair(ref):
        return jnp.concatenate([ref[...]] * n_sub, axis=0)
    hsum = _b16(_head_sum_matrix())

    def head_sum(z):
        return _dot(_b16(z).reshape(np_ * c, LANES), hsum).reshape(np_, c, LANES)

    col = lax.broadcasted_iota(jnp.int32, (np_, c, LANES), 2) % c
    trow = lax.broadcasted_iota(jnp.int32, (np_, c, LANES), 1)
    strict = col < trow
    incl = col <= trow
    bi = lax.broadcasted_iota(jnp.int32, (np_, LANES, LANES), 1)
    bj = lax.broadcasted_iota(jnp.int32, (np_, LANES, LANES), 2)
    eye = bi == bj
    same_head = (bi // HEAD) == (bj // HEAD)

    r, k, v, a_, lw_, cum_ = r_s[...], k_s[...], v_s[...], a_s[...], lw_s[...], cum_s[...]
    kk = k * per_pair(kk_ref)
    kk = kk / jnp.maximum(jnp.sqrt(head_sum(kk * kk)), 1e-12)
    kmod = k * (1.0 + (a_ - 1.0) * per_pair(ka_ref))
    b = kk * a_
    bonus = head_sum(r * kmod * per_pair(rk_ref)) * v

    tot = cum_[:, c - 1:c, :]
    at = -kk * jnp.exp(cum_ - lw_)
    rt = r * jnp.exp(cum_)
    einv = jnp.exp(-cum_)
    eend = jnp.exp(tot - cum_)
    bh, kh = _b16(b * einv), _b16(kmod * einv)
    bt, kt = _b16(b * eend), _b16(kmod * eend)
    at_b = _b16(at)

    aa = _bdot_nt(jnp.concatenate([at_b, _b16(rt)], axis=1),
                  jnp.concatenate([_stack_heads(bh), _stack_heads(kh)], axis=1))
    a_ab = jnp.where(strict, aa[:, :c, :LANES], 0.0)
    a_ak = _b16(jnp.where(strict, aa[:, :c, LANES:], 0.0))
    a_rb = _b16(jnp.where(incl, aa[:, c:, :LANES], 0.0))
    a_rk = _b16(jnp.where(incl, aa[:, c:, LANES:], 0.0))

    t_side = jnp.where(col == trow, 1.0, 0.0) + jnp.where((trow == col + 1) & (trow % 2 == 1), a_ab, 0.0)
    size = 2
    while size < c:
        e_mask = ((trow // (2 * size)) == (col // (2 * size))) & (trow % (2 * size) >= size) & (col % (2 * size) < size)
        t_b = _b16(t_side)
        t_blk = _stack_heads(t_b)
        te = _bdot(t_b, _stack_heads(_b16(jnp.where(e_mask, a_ab, 0.0))))
        t_side = t_side + _bdot(_b16(te), t_blk)
        size *= 2
    t_pair = _b16(t_side)

    v_b = _b16(v)
    v_st = _stack_heads(v_b)
    g0 = _bdot(a_ak, v_st)
    uw = _bdot(t_pair, jnp.concatenate([_stack_heads(_b16(g0)), _stack_heads(at_b)], axis=2))
    u0, w = _b16(uw[:, :, :LANES]), _b16(uw[:, :, LANES:])
    rw = rt + _bdot(a_rb, _stack_heads(w))
    y0 = _bdot(a_rb, _stack_heads(u0)) + _bdot(a_rk, v_st)
    m1 = jnp.where(eye, jnp.exp(tot), 0.0) + jnp.where(same_head, _bdot_tn(bt, w), 0.0)
    m0 = jnp.where(same_head, _bdot_tn(bt, u0) + _bdot_tn(kt, v_b), 0.0)

    rw_b, m1_b = _b16(rw), _b16(m1)
    h = h_ref[...]
    ys = []
    for sub in range(n_sub):
        sl = slice(sub * n_pairs, (sub + 1) * n_pairs)
        h_b = _b16(h)
        ys.append(_bdot(rw_b[sl], h_b) + y0[sl])
        h = _bdot(m1_b[sl], h_b) + m0[sl]
    h_ref[...] = h
    y = jnp.concatenate(ys, axis=0)

    mean = head_sum(y) * (1.0 / HEAD)
    yc = y - mean
    var = head_sum(yc * yc) * (1.0 / HEAD)
    yn = yc * lax.rsqrt(var + GN_EPS) * per_pair(lnw_ref) + per_pair(lnb_ref)
    out = (yn + bonus) * g_s[...]
    for sub in range(n_sub):
        for p in range(n_pairs):
            o_ref[sub * c:(sub + 1) * c, p * LANES:(p + 1) * LANES] = out[sub * n_pairs + p]


def _rwkv(proj, mu, w0, w2, a0, a2, g2, k_k, k_a, r_k, ln_w, ln_b):
    s = proj.shape[0]
    width = w0.shape[0]
    n_pairs = width // LANES
    shift_w = mu.shape[0]
    dl, al = w2.shape[0], a2.shape[0]
    assert dl + al == LANES and g2.shape[0] == LANES and shift_w == 3 * width + 2 * LANES
    wl = jnp.zeros((LANES, 2 * width), F32).at[:dl, :width].set(w2).at[dl:, width:].set(a2)
    wl_hi, wl_lo = _split_bf16(wl)

    def per_pair(t):
        return t.reshape(n_pairs, 1, LANES)

    c = CHUNK
    rows = c * RWKV_CHUNKS_PER_STEP
    pair_spec = _const_spec((n_pairs, 1, LANES))
    return pl.pallas_call(
        _rwkv_kernel,
        grid=(s // rows,),
        in_specs=[pl.BlockSpec((rows, shift_w), lambda i: (i, 0)),
                  _const_spec((1, shift_w)), _const_spec((LANES, 2 * width)), _const_spec((LANES, 2 * width)),
                  _const_spec((LANES, width)),
                  _const_spec((1, width)), _const_spec((1, width)),
                  pair_spec, pair_spec, pair_spec, pair_spec, pair_spec],
        out_specs=pl.BlockSpec((rows, width), lambda i: (i, 0)),
        out_shape=jax.ShapeDtypeStruct((s, width), F32),
        scratch_shapes=[pltpu.VMEM((1, shift_w), F32), pltpu.VMEM((n_pairs, LANES, LANES), F32)]
                       + [pltpu.VMEM((RWKV_CHUNKS_PER_STEP * n_pairs, c, LANES), F32)] * 7,
        compiler_params=_cparams("arbitrary"),
        name="rwkv7",
    )(proj, mu.reshape(1, shift_w), wl_hi, wl_lo, g2.astype(BF16), w0.reshape(1, width), a0.reshape(1, width),
      per_pair(k_k), per_pair(k_a), per_pair(r_k.reshape(-1)), per_pair(ln_w), per_pair(ln_b))


def _shared_kv_kernel(x_ref, g_ref, wk_ref, wvt_ref, kn_ref, k_ref, vt_ref, km_ref):
    h = _rms(x_ref[...], g_ref[...]).astype(BF16)
    k = _dot(h, wk_ref[...])
    n_pairs = k_ref.shape[0]
    for p in range(n_pairs):
        kp = _head_rms(k[:, p * LANES:(p + 1) * LANES], kn_ref[...])
        k_ref[p, 0] = kp.astype(BF16)
        km_ref[0, :, p * LANES:(p + 1) * LANES] = jnp.mean(kp, axis=0, keepdims=True)
    vt = _dot_nt(wvt_ref[...], h)
    vt_ref[:, 0] = vt.reshape(n_pairs, LANES, vt.shape[1]).astype(BF16)


def _shared_kv(x, g, w_kv, k_norm):
    s, d = x.shape
    width = w_kv.shape[1] // 2
    n_pairs = width // LANES
    blk = MOBA_BLOCK
    nb = s // blk
    w = w_kv.astype(BF16)
    kn = jnp.tile(k_norm, LANES // HEAD).reshape(1, LANES)
    k, vt, km = pl.pallas_call(
        _shared_kv_kernel,
        grid=(nb,),
        in_specs=[pl.BlockSpec((blk, d), lambda i: (i, 0)), _const_spec((1, d)),
                  _const_spec((d, width)), _const_spec((width, d)), _const_spec((1, LANES))],
        out_specs=[pl.BlockSpec((n_pairs, 1, blk, LANES), lambda i: (0, i, 0, 0)),
                   pl.BlockSpec((n_pairs, 1, LANES, blk), lambda i: (0, i, 0, 0)),
                   pl.BlockSpec((1, 1, width), lambda i: (i, 0, 0))],
        out_shape=[jax.ShapeDtypeStruct((n_pairs, nb, blk, LANES), BF16),
                   jax.ShapeDtypeStruct((n_pairs, nb, LANES, blk), BF16),
                   jax.ShapeDtypeStruct((nb, 1, width), F32)],
        compiler_params=_cparams("parallel"),
        name="shared_kv",
    )(x, g.reshape(1, d), w, w_kv[:, width:].T.astype(BF16), kn)
    k_bound = (HEAD ** 0.5) * jnp.max(jnp.abs(k_norm)).reshape(1)
    return k, vt, km.reshape(nb, width), k_bound


def _moba_kernel(slopes_ref, depth_ref, kbound_ref, q_ref, qn_ref, km_ref, k_ref, vt_ref, o_ref,
                 sel_ref, fac_ref, ownb_ref, base_ref, sa_ref, sb_ref):
    p = pl.program_id(0)
    i = pl.program_id(1)
    blk = MOBA_BLOCK
    nb = km_ref.shape[0]
    col = lax.broadcasted_iota(jnp.int32, (1, 2 * blk), 1)
    slope = jnp.where(col < blk, slopes_ref[2 * p], slopes_ref[2 * p + 1]) * LOG2E
    pad = slope * (blk - 1)

    @pl.when(i == 0)
    def _():
        key_pos = lax.broadcasted_iota(jnp.int32, (blk, 2 * blk), 0)
        qry_pos = lax.broadcasted_iota(jnp.int32, (blk, 2 * blk), 1) % blk
        dist = (qry_pos - key_pos).astype(F32)
        ownb_ref[...] = jnp.where(dist >= 0.0, -slope * dist, NEG_INF)
        base_ref[...] = -slope * dist - pad

    q = _head_rms(q_ref[...], qn_ref[...]) * (HEAD ** -0.5)
    lane = lax.broadcasted_iota(jnp.int32, (1, LANES), 1)
    q2 = jnp.concatenate([jnp.where(lane < HEAD, q, 0.0), jnp.where(lane >= HEAD, q, 0.0)], axis=0)
    qb = (q2 * LOG2E).astype(BF16)

    start = jnp.maximum(i - depth_ref[p] + 1, 0)
    n_iter = (i - start + 3) // 4

    def block_of(g, j):
        return jnp.minimum(start + 2 * g + j, nb - 1)

    def fast_scores(g, s_ref):
        for j in range(2):
            s_ref[j] = _dot_nt(k_ref[block_of(g, j)], qb)

    fast_scores(0, sa_ref)

    n_idx = lax.broadcasted_iota(jnp.int32, (nb, 2 * blk), 0)
    q_hi, q_lo = _split_bf16(q2)
    km_hi, km_lo = _split_bf16(km_ref[...])
    gate = _dot_nt(km_hi, q_hi) + (_dot_nt(km_lo, q_hi) + _dot_nt(km_hi, q_lo))
    gate = jnp.where(n_idx < i, gate, NEG_INF)
    sel_bias = jnp.full(gate.shape, NEG_INF, F32)
    for _ in range(MOBA_TOPK):
        mx = jnp.max(gate, axis=0, keepdims=True)
        first = jnp.min(jnp.where(gate == mx, n_idx, nb), axis=0, keepdims=True)
        pick = (n_idx == first) & (mx > NEG_INF)
        sel_bias = jnp.where(pick, 0.0, sel_bias)
        gate = jnp.where(pick, NEG_INF, gate)
    sel_ref[...] = sel_bias - slope * ((i - n_idx) * blk).astype(F32)

    fac_ref[...] = jnp.exp2(sel_ref[...] + pad)

    def weighted_values(n, e):
        eb = e.astype(BF16)
        return jnp.concatenate([_dot(vt_ref[n, :HEAD, :], eb[:, :blk]), _dot(vt_ref[n, HEAD:, :], eb[:, blk:])],
                               axis=1)

    e = jnp.exp2(_dot_nt(k_ref[i], qb) + ownb_ref[...])
    l_own = jnp.sum(e, axis=0, keepdims=True)
    acc_own = weighted_values(i, e)

    def fast_consume(g, s_ref, l, acc):
        for j in range(2):
            n = block_of(g, j)
            e = jnp.exp2(s_ref[j] + base_ref[...])
            f = fac_ref[pl.ds(n, 1), :]
            l = l + f * jnp.sum(e, axis=0, keepdims=True)
            acc = acc + f * weighted_values(n, e)
        return l, acc

    def fast_two_groups(u, carry):
        fast_scores(2 * u + 1, sb_ref)
        carry = fast_consume(2 * u, sa_ref, *carry)
        fast_scores(2 * u + 2, sa_ref)
        return fast_consume(2 * u + 1, sb_ref, *carry)

    l, acc = lax.fori_loop(0, n_iter, fast_two_groups, (l_own, acc_own))

    def emit(l, acc):
        out = acc / l
        o_ref[...] = jnp.concatenate([out[:, :blk], out[:, blk:]], axis=0).T

    emit(l, acc)

    q_len = jnp.sqrt(_dot_nt(jnp.ones((8, LANES), BF16), _b16(q2 * q2))[0:1])
    score_bound = q_len * (kbound_ref[0] * (LOG2E * BF16_SLACK))
    redo = jnp.max(jnp.where(score_bound < EXP_HEADROOM, 0.0, 1.0))
    redo = jnp.maximum(redo, jnp.maximum(jnp.max(jnp.where(jnp.abs(l) < jnp.inf, 0.0, 1.0)),
                                         jnp.max(jnp.where(jnp.abs(acc) < jnp.inf, 0.0, 1.0))))

    def any_block_of(g, j):
        return jnp.minimum(2 * g + j, nb - 1)

    def score_group(g, s_ref):
        col_max = []
        for j in range(2):
            n = any_block_of(g, j)
            s = _dot_nt(k_ref[n], qb) + (base_ref[...] + pad)
            s_ref[j] = s
            col_max.append(jnp.max(s, axis=0, keepdims=True) + sel_ref[pl.ds(n, 1), :])
        return col_max

    def consume_group(g, s_ref, col_max, m, l, acc):
        n0, n1 = any_block_of(g, 0), any_block_of(g, 1)
        m_new = jnp.maximum(m, jnp.maximum(col_max[0], col_max[1]))
        alpha = jnp.exp2(m - m_new)
        l = alpha * l
        acc = alpha * acc
        for j, n in enumerate((n0, n1)):
            e = jnp.exp2(s_ref[j] - (m_new - sel_ref[pl.ds(n, 1), :]))
            l = l + jnp.sum(e, axis=0, keepdims=True)
            acc = acc + weighted_values(n, e)
        return m_new, l, acc

    @pl.when(redo > 0.0)
    def _():
        max_a = score_group(0, sa_ref)
        s = _dot_nt(k_ref[i], qb) + ownb_ref[...]
        m_own = jnp.max(s, axis=0, keepdims=True)
        e = jnp.exp2(s - m_own)
        l_own = jnp.sum(e, axis=0, keepdims=True)
        acc_own = weighted_values(i, e)

        def two_groups(u, carry):
            max_a0, max_a1, m, l, acc = carry
            max_b = score_group(2 * u + 1, sb_ref)
            m, l, acc = consume_group(2 * u, sa_ref, (max_a0, max_a1), m, l, acc)
            max_a = score_group(2 * u + 2, sa_ref)
            m, l, acc = consume_group(2 * u + 1, sb_ref, max_b, m, l, acc)
            return max_a[0], max_a[1], m, l, acc

        init = (max_a[0], max_a[1], m_own, l_own, acc_own)
        _, _, _, l, acc = lax.fori_loop(0, (i + 3) // 4, two_groups, init)
        emit(l, acc)


def _moba(proj, q_norm, k, vt, km, k_bound):
    s = proj.shape[0]
    n_pairs, nb, blk, _ = k.shape
    heads = n_pairs * (LANES // HEAD)
    slopes = jnp.exp2(-ALIBI_MAX * jnp.arange(1, heads + 1, dtype=F32) / heads)
    min_slope = [2.0 ** (-ALIBI_MAX * (2 * p + 2) / heads) for p in range(n_pairs)]
    depth = jnp.asarray([math.ceil((DEAD_EXPONENT / (sl * LOG2E) + blk - 1) / blk) for sl in min_slope], jnp.int32)
    qn = jnp.tile(q_norm, LANES // HEAD).reshape(1, LANES)
    smem = pl.BlockSpec(memory_space=pltpu.SMEM)
    return pl.pallas_call(
        _moba_kernel,
        grid=(n_pairs, nb),
        in_specs=[smem, smem, smem,
                  pl.BlockSpec((blk, LANES), lambda p, i: (i, p)),
                  pl.BlockSpec((1, LANES), lambda p, i: (0, 0)),
                  pl.BlockSpec((nb, LANES), lambda p, i: (0, p)),
                  pl.BlockSpec((None, nb, blk, LANES), lambda p, i: (p, 0, 0, 0)),
                  pl.BlockSpec((None, nb, LANES, blk), lambda p, i: (p, 0, 0, 0))],
        out_specs=pl.BlockSpec((blk, LANES), lambda p, i: (i, p)),
        out_shape=jax.ShapeDtypeStruct((s, n_pairs * LANES), F32),
        scratch_shapes=[pltpu.VMEM((nb, 2 * blk), F32), pltpu.VMEM((nb, 2 * blk), F32),
                        pltpu.VMEM((blk, 2 * blk), F32), pltpu.VMEM((blk, 2 * blk), F32),
                        pltpu.VMEM((2, blk, 2 * blk), F32), pltpu.VMEM((2, blk, 2 * blk), F32)],
        compiler_params=_cparams("parallel", "arbitrary"),
        name="moba",
    )(slopes, depth, k_bound, proj, qn, km, k, vt)


def kernel(x, mem, ffn1_norm, ffn1_w1, ffn1_w3, ffn1_w2, mix_norm, w_out, mem_norm, w_mem_kv, mem_q_norm, mem_k_norm, ffn2_norm, ffn2_w1, ffn2_w3, ffn2_w2, rwkv_w_in, rwkv_mu, rwkv_w0, rwkv_w2, rwkv_a0, rwkv_a2, rwkv_g2, rwkv_k_k, rwkv_k_a, rwkv_r_k, rwkv_ln_w, rwkv_ln_b, kv_norm, w_kv, kv_k_norm, moba_w_in, moba_q_norm):
    batch = x.shape[0]
    depth = ffn1_norm.shape[0]
    n_a = rwkv_w_in.shape[0]
    mix_w = rwkv_w0.shape[1]
    mem_w = w_mem_kv.shape[2] // 2
    outs = []
    for bi in range(batch):
        xb, memb = x[bi], mem[bi]
        shared = None
        for l in range(depth):
            xb = _ffn(xb, ffn1_norm[l], ffn1_w1[l], ffn1_w3[l], ffn1_w2[l])
            if l < n_a:
                proj = _norm_proj(xb, mix_norm[l], rwkv_w_in[l])
                mix = _rwkv(proj, rwkv_mu[l], rwkv_w0[l], rwkv_w2[l], rwkv_a0[l], rwkv_a2[l], rwkv_g2[l],
                            rwkv_k_k[l], rwkv_k_a[l], rwkv_r_k[l], rwkv_ln_w[l], rwkv_ln_b[l])
                mem_col = rwkv_mu.shape[1] // mem_w
            else:
                j = l - n_a
                proj = _norm_proj(xb, mix_norm[l], moba_w_in[j])
                mix = _moba(proj, moba_q_norm[j], *shared)
                mem_col = mix_w // mem_w
            mk, mv = _mem_kv(memb, mem_norm[l], w_mem_kv[l], mem_k_norm[l])
            mem_out = _mem_attn(proj, mem_col, mem_q_norm[l], mk, mv)
            xb = _ffn(xb, ffn2_norm[l], ffn2_w1[l], ffn2_w3[l], ffn2_w2[l], mixer=(mix, mem_out, w_out[l]))
            if l == n_a - 1:
                shared = _shared_kv(xb, kv_norm, w_kv, kv_k_norm)
        outs.append(xb)
    return jnp.stack(outs)
```

```python
import functools
import math

import jax
import jax.numpy as jnp
from jax import lax
from jax.experimental import pallas as pl
from jax.experimental.pallas import tpu as pltpu

F32 = jnp.float32
BF16 = jnp.bfloat16
HI = lax.Precision.HIGHEST

HEAD = 64
LANES = 128
MXU = 256
CHUNK = 64
RWKV_CHUNKS_PER_STEP = 4
MOBA_BLOCK = 256
MOBA_TOPK = 3
ALIBI_MAX = 8.0
NORM_EPS = 1e-6
GN_EPS = 64e-5
VMEM_LIMIT = 56 * 1024 * 1024
NEG_INF = float("-inf")
LOG2E = 1.4426950408889634
DEAD_EXPONENT = 160.0
EXP_HEADROOM = 80.0
BF16_SLACK = 1.02


def _cparams(*sem):
    return pltpu.CompilerParams(dimension_semantics=sem, vmem_limit_bytes=VMEM_LIMIT)


def _const_spec(shape):
    return pl.BlockSpec(shape, lambda *_: (0,) * len(shape), pipeline_mode=pl.Buffered(1))


def _rms(x, g):
    return x * lax.rsqrt(jnp.mean(x * x, axis=-1, keepdims=True) + NORM_EPS) * g


def _b16(x):
    return x.astype(BF16)


def _split_bf16(x):
    hi = x.astype(BF16)
    return hi, (x - hi.astype(F32)).astype(BF16)


def _dot(a, b, precision=None):
    return jnp.dot(a, b, preferred_element_type=F32, precision=precision)


def _dot_nt(a, b, precision=None):
    return lax.dot_general(a, b, (((1,), (1,)), ((), ())), preferred_element_type=F32, precision=precision)


def _dot_tn(a, b, precision=None):
    return lax.dot_general(a, b, (((0,), (0,)), ((), ())), preferred_element_type=F32, precision=precision)


def _head_sum_matrix(scale=1.0):
    i = lax.broadcasted_iota(jnp.int32, (LANES, LANES), 0) // HEAD
    j = lax.broadcasted_iota(jnp.int32, (LANES, LANES), 1) // HEAD
    return jnp.where(i == j, F32(scale), F32(0.0))


def _head_rms(x, g):
    hi, lo = _split_bf16(x * x)
    mean_matrix = _b16(_head_sum_matrix(1.0 / HEAD))
    ms = _dot(hi, mean_matrix) + _dot(lo, mean_matrix)
    return x * lax.rsqrt(ms + NORM_EPS) * g


def _swiglu_half_step(x, g_ref, w1_ref, w3_ref, w2_ref, acc_ref, f_chunk):
    h = _rms(x, g_ref[...]).astype(BF16)
    n_chunks = w1_ref.shape[1] // f_chunk
    for c in range(n_chunks):
        sl = slice(c * f_chunk, (c + 1) * f_chunk)
        a = _dot(h, w1_ref[:, sl])
        b = _dot(h, w3_ref[:, sl])
        act = (a * jax.nn.sigmoid(a) * b).astype(BF16)
        part = _dot(act, w2_ref[sl, :])
        if c == 0:
            acc_ref[...] = part
        else:
            acc_ref[...] += part
    return x + 0.5 * acc_ref[...]


def _ffn_kernel(x_ref, g_ref, w1_ref, w3_ref, w2_ref, o_ref, acc_ref, *, f_chunk):
    o_ref[...] = _swiglu_half_step(x_ref[...], g_ref, w1_ref, w3_ref, w2_ref, acc_ref, f_chunk)


def _mix_ffn_kernel(x_ref, mix_ref, mem_ref, wo_ref, g_ref, w1_ref, w3_ref, w2_ref, o_ref, acc_ref, *, f_chunk):
    na = mix_ref.shape[1]
    x = (x_ref[...] + _dot(mix_ref[...].astype(BF16), wo_ref[:na, :])
         + _dot(mem_ref[...].astype(BF16), wo_ref[na:, :]))
    o_ref[...] = _swiglu_half_step(x, g_ref, w1_ref, w3_ref, w2_ref, acc_ref, f_chunk)


def _cast_pad_kernel(x_ref, o_ref, *, axis):
    x = x_ref[...].astype(BF16)
    n = x.shape[axis]
    if axis == 0:
        o_ref[:n, :] = x
        o_ref[n:, :] = jnp.zeros((o_ref.shape[0] - n, x.shape[1]), BF16)
    else:
        o_ref[:, :n] = x
        o_ref[:, n:] = jnp.zeros((x.shape[0], o_ref.shape[1] - n), BF16)


def _cast_pad(w, axis, pad, *, tile=256):
    layers, a, b = w.shape
    if axis == 2:
        grid, in_blk, out_blk = (layers, a // tile), (None, tile, b), (None, tile, b + pad)
        index = lambda l, i: (l, i, 0)
    else:
        grid, in_blk, out_blk = (layers, b // tile), (None, a, tile), (None, a + pad, tile)
        index = lambda l, i: (l, 0, i)
    out_shape = (layers, a + pad, b) if axis == 1 else (layers, a, b + pad)
    return pl.pallas_call(
        functools.partial(_cast_pad_kernel, axis=axis - 1),
        grid=grid,
        in_specs=[pl.BlockSpec(in_blk, index)],
        out_specs=pl.BlockSpec(out_blk, index),
        out_shape=jax.ShapeDtypeStruct(out_shape, BF16),
        compiler_params=_cparams("parallel", "parallel"),
        name="cast_pad",
    )(w)


def _ffn_weights(w1, w3, w2, f_chunk=MXU):
    f = w1.shape[2]
    pad = -(-f // f_chunk) * f_chunk - f
    return _cast_pad(w1, 2, pad), _cast_pad(w3, 2, pad), _cast_pad(w2, 1, pad)


def _layer_spec(shape, layer):
    return pl.BlockSpec((None,) + shape, lambda *_: (layer,) + (0,) * len(shape), pipeline_mode=pl.Buffered(1))


def _ffn(x, g, weights, layer, mixer=None, *, tm=512, f_chunk=MXU):
    s, d = x.shape
    w1, w3, w2 = weights
    fp = w1.shape[2]
    tm = min(tm, s)
    row_spec = pl.BlockSpec((tm, d), lambda i: (i, 0))
    ffn_specs = [_const_spec((1, d)), _layer_spec((d, fp), layer), _layer_spec((d, fp), layer),
                 _layer_spec((fp, d), layer)]
    ffn_args = (g.reshape(1, d), w1, w3, w2)
    if mixer is None:
        body, specs, args = _ffn_kernel, [row_spec] + ffn_specs, (x,) + ffn_args
    else:
        mix, mem_out, w_out = mixer
        na, nb = mix.shape[1], mem_out.shape[1]
        body = _mix_ffn_kernel
        specs = [row_spec, pl.BlockSpec((tm, na), lambda i: (i, 0)), pl.BlockSpec((tm, nb), lambda i: (i, 0)),
                 _layer_spec((na + nb, d), layer)] + ffn_specs
        args = (x, mix, mem_out, w_out) + ffn_args
    return pl.pallas_call(
        functools.partial(body, f_chunk=f_chunk),
        grid=(s // tm,),
        in_specs=specs,
        out_specs=row_spec,
        out_shape=jax.ShapeDtypeStruct((s, d), F32),
        scratch_shapes=[pltpu.VMEM((tm, d), F32)],
        compiler_params=_cparams("parallel"),
        name="ffn" if mixer is None else "mix_ffn",
    )(*args)


def _norm_proj_kernel(x_ref, g_ref, w_ref, o_ref):
    h = _rms(x_ref[...], g_ref[...]).astype(BF16)
    o_ref[...] = _dot(h, w_ref[...])


def _norm_proj(x, g, w, *, tm=512):
    s, d = x.shape
    n = w.shape[1]
    tm = min(tm, s)
    return pl.pallas_call(
        _norm_proj_kernel,
        grid=(s // tm,),
        in_specs=[pl.BlockSpec((tm, d), lambda i: (i, 0)), _const_spec((1, d)), _const_spec((d, n))],
        out_specs=pl.BlockSpec((tm, n), lambda i: (i, 0)),
        out_shape=jax.ShapeDtypeStruct((s, n), F32),
        compiler_params=_cparams("parallel"),
        name="norm_proj",
    )(x, g.reshape(1, d), w.astype(BF16))


def _mem_kv_kernel(mem_ref, g_ref, w_ref, kn_ref, k_ref, v_ref):
    h = _rms(mem_ref[...], g_ref[...]).astype(BF16)
    kv = _dot(h, w_ref[...])
    width = k_ref.shape[1]
    for p in range(width // LANES):
        sl = slice(p * LANES, (p + 1) * LANES)
        k_ref[:, sl] = _head_rms(kv[:, sl], kn_ref[...])
    v_ref[...] = kv[:, width:]


def _mem_kv(mem, g, w, k_norm):
    m, d = mem.shape
    width = w.shape[1] // 2
    kn = jnp.tile(k_norm, LANES // HEAD).reshape(1, LANES)
    return pl.pallas_call(
        _mem_kv_kernel,
        grid=(1,),
        in_specs=[_const_spec((m, d)), _const_spec((1, d)), _const_spec((d, 2 * width)), _const_spec((1, LANES))],
        out_specs=[pl.BlockSpec((m, width), lambda i: (0, 0)), pl.BlockSpec((m, width), lambda i: (0, 0))],
        out_shape=[jax.ShapeDtypeStruct((m, width), F32)] * 2,
        compiler_params=_cparams("arbitrary"),
        name="mem_kv",
    )(mem, g.reshape(1, d), w.astype(BF16), kn)


def _mem_attn_kernel(q_ref, qn_ref, k_ref, v_ref, o_ref):
    lane = lax.broadcasted_iota(jnp.int32, (1, LANES), 1)
    width = q_ref.shape[1]
    for p in range(width // LANES):
        sl = slice(p * LANES, (p + 1) * LANES)
        q = _head_rms(q_ref[:, sl], qn_ref[...]) * (HEAD ** -0.5)
        k = k_ref[:, sl].astype(BF16)
        v = v_ref[:, sl].astype(BF16)
        out = None
        for h in range(LANES // HEAD):
            in_head = (lane // HEAD) == h
            qh = jnp.where(in_head, q, 0.0).astype(BF16)
            s = _dot_nt(qh, k)
            s = s - jnp.max(s, axis=-1, keepdims=True)
            e = jnp.exp(s)
            pr = e / jnp.sum(e, axis=-1, keepdims=True)
            oh = _dot(pr.astype(BF16), v)
            out = oh if out is None else jnp.where(in_head, oh, out)
        o_ref[:, sl] = out


def _mem_attn(proj, col_block, q_norm, k, v, *, tm=512):
    s = proj.shape[0]
    m, width = k.shape
    tm = min(tm, s)
    qn = jnp.tile(q_norm, LANES // HEAD).reshape(1, LANES)
    return pl.pallas_call(
        _mem_attn_kernel,
        grid=(s // tm,),
        in_specs=[pl.BlockSpec((tm, width), lambda i: (i, col_block)),
                  _const_spec((1, LANES)), _const_spec((m, width)), _const_spec((m, width))],
        out_specs=pl.BlockSpec((tm, width), lambda i: (i, 0)),
        out_shape=jax.ShapeDtypeStruct((s, width), F32),
        compiler_params=_cparams("parallel"),
        name="mem_attn",
    )(proj, qn, k, v)


def _stack_heads(z):
    lane = lax.broadcasted_iota(jnp.int32, z.shape, 2)
    zero = jnp.zeros_like(z)
    return jnp.concatenate([jnp.where(lane < HEAD, z, zero), jnp.where(lane >= HEAD, z, zero)], axis=1)


def _bdot(a, b):
    return lax.dot_general(a, b, (((2,), (1,)), ((0,), (0,))), preferred_element_type=F32)


def _bdot_nt(a, b):
    return lax.dot_general(a, b, (((2,), (2,)), ((0,), (0,))), preferred_element_type=F32)


def _bdot_tn(a, b):
    return lax.dot_general(a, b, (((1,), (1,)), ((0,), (0,))), preferred_element_type=F32)


def _rwkv_kernel(proj_ref, mu_ref, wlh_ref, wll_ref, g2_ref, w0_ref, a0_ref, kk_ref, ka_ref, rk_ref, lnw_ref, lnb_ref,
                 o_ref, last_ref, h_ref, r_s, k_s, v_s, a_s, lw_s, cum_s, g_s):
    c = CHUNK
    rows = proj_ref.shape[0]
    n_sub = rows // c
    n_pairs = h_ref.shape[0]
    width = n_pairs * LANES

    @pl.when(pl.program_id(0) == 0)
    def _():
        last_ref[...] = jnp.zeros_like(last_ref)
        h_ref[...] = jnp.zeros_like(h_ref)

    u = proj_ref[...]
    row = lax.broadcasted_iota(jnp.int32, u.shape, 0)
    prev = jnp.where(row == 0, last_ref[...], pltpu.roll(u, 1, 0))
    last_ref[...] = u[rows - 1:rows, :]
    u = u + (prev - u) * mu_ref[...]

    lora_in = u[:, 3 * width:3 * width + LANES]
    lane = lax.broadcasted_iota(jnp.int32, lora_in.shape, 1)
    lora_in = jnp.where(lane < HEAD, jnp.tanh(lora_in), lora_in)
    x_hi, x_lo = _split_bf16(lora_in)
    lo = _dot(x_hi, wlh_ref[...]) + (_dot(x_lo, wlh_ref[...]) + _dot(x_hi, wll_ref[...]))
    z = -(w0_ref[...] + lo[:, :width])
    softplus = jnp.maximum(z, 0.0) + jnp.log(1.0 + jnp.exp(-jnp.abs(z)))
    lw = -jnp.exp(-softplus - 0.5)
    a = jax.nn.sigmoid(a0_ref[...] + lo[:, width:])
    g = _dot(_b16(jax.nn.sigmoid(u[:, 3 * width + LANES:])), g2_ref[...])
    ti = lax.broadcasted_iota(jnp.int32, (rows, rows), 0)
    si = lax.broadcasted_iota(jnp.int32, (rows, rows), 1)
    tri = _b16(jnp.where((si <= ti) & (si // c == ti // c), F32(1.0), F32(0.0)))
    lw_hi, lw_lo = _split_bf16(lw)
    cum = _dot(tri, lw_hi) + _dot(tri, lw_lo)

    for sub in range(n_sub):
        rs = slice(sub * c, (sub + 1) * c)
        for p in range(n_pairs):
            b_ = sub * n_pairs + p
            sl = slice(p * LANES, (p + 1) * LANES)
            r_s[b_] = u[rs, sl]
            k_s[b_] = u[rs, width + p * LANES:width + (p + 1) * LANES]
            v_s[b_] = u[rs, 2 * width + p * LANES:2 * width + (p + 1) * LANES]
            a_s[b_] = a[rs, sl]
            lw_s[b_] = lw[rs, sl]
            cum_s[b_] = cum[rs, sl]
            g_s[b_] = g[rs, sl]

    np_ = n_sub * n_pairs

    def per_pair(ref):
        return jnp.concatenate([ref[...]] * n_sub, axis=0)
    hsum = _b16(_head_sum_matrix())

    def head_sum(z):
        return _dot(_b16(z).reshape(np_ * c, LANES), hsum).reshape(np_, c, LANES)

    col = lax.broadcasted_iota(jnp.int32, (np_, c, LANES), 2) % c
    trow = lax.broadcasted_iota(jnp.int32, (np_, c, LANES), 1)
    strict = col < trow
    incl = col <= trow
    bi = lax.broadcasted_iota(jnp.int32, (np_, LANES, LANES), 1)
    bj = lax.broadcasted_iota(jnp.int32, (np_, LANES, LANES), 2)
    eye = bi == bj
    same_head = (bi // HEAD) == (bj // HEAD)

    r, k, v, a_, lw_, cum_ = r_s[...], k_s[...], v_s[...], a_s[...], lw_s[...], cum_s[...]
    kk = k * per_pair(kk_ref)
    kk = kk / jnp.maximum(jnp.sqrt(head_sum(kk * kk)), 1e-12)
    kmod = k * (1.0 + (a_ - 1.0) * per_pair(ka_ref))
    b = kk * a_
    bonus = head_sum(r * kmod * per_pair(rk_ref)) * v

    tot = cum_[:, c - 1:c, :]
    at = -kk * jnp.exp(cum_ - lw_)
    rt = r * jnp.exp(cum_)
    einv = jnp.exp(-cum_)
    eend = jnp.exp(tot - cum_)
    bh, kh = _b16(b * einv), _b16(kmod * einv)
    bt, kt = _b16(b * eend), _b16(kmod * eend)
    at_b = _b16(at)

    aa = _bdot_nt(jnp.concatenate([at_b, _b16(rt)], axis=1),
                  jnp.concatenate([_stack_heads(bh), _stack_heads(kh)], axis=1))
    a_ab = jnp.where(strict, aa[:, :c, :LANES], 0.0)
    a_ak = _b16(jnp.where(strict, aa[:, :c, LANES:], 0.0))
    a_rb = _b16(jnp.where(incl, aa[:, c:, :LANES], 0.0))
    a_rk = _b16(jnp.where(incl, aa[:, c:, LANES:], 0.0))

    t_side = jnp.where(col == trow, 1.0, 0.0) + jnp.where((trow == col + 1) & (trow % 2 == 1), a_ab, 0.0)
    size = 2
    while size < c:
        e_mask = ((trow // (2 * size)) == (col // (2 * size))) & (trow % (2 * size) >= size) & (col % (2 * size) < size)
        t_b = _b16(t_side)
        t_blk = _stack_heads(t_b)
        te = _bdot(t_b, _stack_heads(_b16(jnp.where(e_mask, a_ab, 0.0))))
        t_side = t_side + _bdot(_b16(te), t_blk)
        size *= 2
    t_pair = _b16(t_side)

    v_b = _b16(v)
    v_st = _stack_heads(v_b)
    g0 = _bdot(a_ak, v_st)
    uw = _bdot(t_pair, jnp.concatenate([_stack_heads(_b16(g0)), _stack_heads(at_b)], axis=2))
    u0, w = _b16(uw[:, :, :LANES]), _b16(uw[:, :, LANES:])
    rw = rt + _bdot(a_rb, _stack_heads(w))
    y0 = _bdot(a_rb, _stack_heads(u0)) + _bdot(a_rk, v_st)
    m1 = jnp.where(eye, jnp.exp(tot), 0.0) + jnp.where(same_head, _bdot_tn(bt, w), 0.0)
    m0 = jnp.where(same_head, _bdot_tn(bt, u0) + _bdot_tn(kt, v_b), 0.0)

    rw_b, m1_b = _b16(rw), _b16(m1)
    h = h_ref[...]
    ys = []
    for sub in range(n_sub):
        sl = slice(sub * n_pairs, (sub + 1) * n_pairs)
        h_b = _b16(h)
        ys.append(_bdot(rw_b[sl], h_b) + y0[sl])
        h = _bdot(m1_b[sl], h_b) + m0[sl]
    h_ref[...] = h
    y = jnp.concatenate(ys, axis=0)

    mean = head_sum(y) * (1.0 / HEAD)
    yc = y - mean
    var = head_sum(yc * yc) * (1.0 / HEAD)
    yn = yc * lax.rsqrt(var + GN_EPS) * per_pair(lnw_ref) + per_pair(lnb_ref)
    out = (yn + bonus) * g_s[...]
    for sub in range(n_sub):
        for p in range(n_pairs):
            o_ref[sub * c:(sub + 1) * c, p * LANES:(p + 1) * LANES] = out[sub * n_pairs + p]


def _rwkv(proj, mu, w0, w2, a0, a2, g2, k_k, k_a, r_k, ln_w, ln_b):
    s = proj.shape[0]
    width = w0.shape[0]
    n_pairs = width // LANES
    shift_w = mu.shape[0]
    dl, al = w2.shape[0], a2.shape[0]
    assert dl + al == LANES and g2.shape[0] == LANES and shift_w == 3 * width + 2 * LANES
    wl = jnp.zeros((LANES, 2 * width), F32).at[:dl, :width].set(w2).at[dl:, width:].set(a2)
    wl_hi, wl_lo = _split_bf16(wl)

    def per_pair(t):
        return t.reshape(n_pairs, 1, LANES)

    c = CHUNK
    rows = c * RWKV_CHUNKS_PER_STEP
    pair_spec = _const_spec((n_pairs, 1, LANES))
    return pl.pallas_call(
        _rwkv_kernel,
        grid=(s // rows,),
        in_specs=[pl.BlockSpec((rows, shift_w), lambda i: (i, 0)),
                  _const_spec((1, shift_w)), _const_spec((LANES, 2 * width)), _const_spec((LANES, 2 * width)),
                  _const_spec((LANES, width)),
                  _const_spec((1, width)), _const_spec((1, width)),
                  pair_spec, pair_spec, pair_spec, pair_spec, pair_spec],
        out_specs=pl.BlockSpec((rows, width), lambda i: (i, 0)),
        out_shape=jax.ShapeDtypeStruct((s, width), F32),
        scratch_shapes=[pltpu.VMEM((1, shift_w), F32), pltpu.VMEM((n_pairs, LANES, LANES), F32)]
                       + [pltpu.VMEM((RWKV_CHUNKS_PER_STEP * n_pairs, c, LANES), F32)] * 7,
        compiler_params=_cparams("arbitrary"),
        name="rwkv7",
    )(proj, mu.reshape(1, shift_w), wl_hi, wl_lo, g2.astype(BF16), w0.reshape(1, width), a0.reshape(1, width),
      per_pair(k_k), per_pair(k_a), per_pair(r_k.reshape(-1)), per_pair(ln_w), per_pair(ln_b))


def _shared_kv_kernel(x_ref, g_ref, wk_ref, wvt_ref, kn_ref, k_ref, vt_ref, km_ref):
    h = _rms(x_ref[...], g_ref[...]).astype(BF16)
    k = _dot(h, wk_ref[...])
    n_pairs = k_ref.shape[0]
    for p in range(n_pairs):
        kp = _head_rms(k[:, p * LANES:(p + 1) * LANES], kn_ref[...])
        k_ref[p, 0] = kp.astype(BF16)
        km_ref[0, :, p * LANES:(p + 1) * LANES] = jnp.mean(kp, axis=0, keepdims=True)
    vt = _dot_nt(wvt_ref[...], h)
    vt_ref[:, 0] = vt.reshape(n_pairs, LANES, vt.shape[1]).astype(BF16)


def _shared_kv(x, g, w_kv, k_norm):
    s, d = x.shape
    width = w_kv.shape[1] // 2
    n_pairs = width // LANES
    blk = MOBA_BLOCK
    nb = s // blk
    w = w_kv.astype(BF16)
    kn = jnp.tile(k_norm, LANES // HEAD).reshape(1, LANES)
    k, vt, km = pl.pallas_call(
        _shared_kv_kernel,
        grid=(nb,),
        in_specs=[pl.BlockSpec((blk, d), lambda i: (i, 0)), _const_spec((1, d)),
                  _const_spec((d, width)), _const_spec((width, d)), _const_spec((1, LANES))],
        out_specs=[pl.BlockSpec((n_pairs, 1, blk, LANES), lambda i: (0, i, 0, 0)),
                   pl.BlockSpec((n_pairs, 1, LANES, blk), lambda i: (0, i, 0, 0)),
                   pl.BlockSpec((1, 1, width), lambda i: (i, 0, 0))],
        out_shape=[jax.ShapeDtypeStruct((n_pairs, nb, blk, LANES), BF16),
                   jax.ShapeDtypeStruct((n_pairs, nb, LANES, blk), BF16),
                   jax.ShapeDtypeStruct((nb, 1, width), F32)],
        compiler_params=_cparams("parallel"),
        name="shared_kv",
    )(x, g.reshape(1, d), w, w_kv[:, width:].T.astype(BF16), kn)
    k_bound = (HEAD ** 0.5) * jnp.max(jnp.abs(k_norm)).reshape(1)
    return k, vt, km.reshape(nb, width), k_bound


def _moba_kernel(slopes_ref, depth_ref, kbound_ref, q_ref, qn_ref, km_ref, k_ref, vt_ref, o_ref,
                 sel_ref, fac_ref, ownb_ref, base_ref, sa_ref, sb_ref):
    p = pl.program_id(0)
    i = pl.program_id(1)
    blk = MOBA_BLOCK
    nb = km_ref.shape[0]
    col = lax.broadcasted_iota(jnp.int32, (1, 2 * blk), 1)
    slope = jnp.where(col < blk, slopes_ref[2 * p], slopes_ref[2 * p + 1]) * LOG2E
    pad = slope * (blk - 1)

    @pl.when(i == 0)
    def _():
        key_pos = lax.broadcasted_iota(jnp.int32, (blk, 2 * blk), 0)
        qry_pos = lax.broadcasted_iota(jnp.int32, (blk, 2 * blk), 1) % blk
        dist = (qry_pos - key_pos).astype(F32)
        ownb_ref[...] = jnp.where(dist >= 0.0, -slope * dist, NEG_INF)
        base_ref[...] = -slope * dist - pad

    q = _head_rms(q_ref[...], qn_ref[...]) * (HEAD ** -0.5)
    lane = lax.broadcasted_iota(jnp.int32, (1, LANES), 1)
    q2 = jnp.concatenate([jnp.where(lane < HEAD, q, 0.0), jnp.where(lane >= HEAD, q, 0.0)], axis=0)
    qb = (q2 * LOG2E).astype(BF16)

    start = jnp.maximum(i - depth_ref[p] + 1, 0)
    n_iter = (i - start + 3) // 4

    def block_of(g, j):
        return jnp.minimum(start + 2 * g + j, nb - 1)

    def fast_scores(g, s_ref):
        for j in range(2):
            s_ref[j] = _dot_nt(k_ref[block_of(g, j)], qb)

    fast_scores(0, sa_ref)

    n_idx = lax.broadcasted_iota(jnp.int32, (nb, 2 * blk), 0)
    q_hi, q_lo = _split_bf16(q2)
    km_hi, km_lo = _split_bf16(km_ref[...])
    gate = _dot_nt(km_hi, q_hi) + (_dot_nt(km_lo, q_hi) + _dot_nt(km_hi, q_lo))
    gate = jnp.where(n_idx < i, gate, NEG_INF)
    sel_bias = jnp.full(gate.shape, NEG_INF, F32)
    for _ in range(MOBA_TOPK):
        mx = jnp.max(gate, axis=0, keepdims=True)
        first = jnp.min(jnp.where(gate == mx, n_idx, nb), axis=0, keepdims=True)
        pick = (n_idx == first) & (mx > NEG_INF)
        sel_bias = jnp.where(pick, 0.0, sel_bias)
        gate = jnp.where(pick, NEG_INF, gate)
    sel_ref[...] = sel_bias - slope * ((i - n_idx) * blk).astype(F32)

    fac_ref[...] = jnp.exp2(sel_ref[...] + pad)

    def weighted_values(n, e):
        eb = e.astype(BF16)
        return jnp.concatenate([_dot(vt_ref[n, :HEAD, :], eb[:, :blk]), _dot(vt_ref[n, HEAD:, :], eb[:, blk:])],
                               axis=1)

    e = jnp.exp2(_dot_nt(k_ref[i], qb) + ownb_ref[...])
    l_own = jnp.sum(e, axis=0, keepdims=True)
    acc_own = weighted_values(i, e)

    def fast_consume(g, s_ref, l, acc):
        for j in range(2):
            n = block_of(g, j)
            e = jnp.exp2(s_ref[j] + base_ref[...])
            f = fac_ref[pl.ds(n, 1), :]
            l = l + f * jnp.sum(e, axis=0, keepdims=True)
            acc = acc + f * weighted_values(n, e)
        return l, acc

    def fast_two_groups(u, carry):
        fast_scores(2 * u + 1, sb_ref)
        carry = fast_consume(2 * u, sa_ref, *carry)
        fast_scores(2 * u + 2, sa_ref)
        return fast_consume(2 * u + 1, sb_ref, *carry)

    l, acc = lax.fori_loop(0, n_iter, fast_two_groups, (l_own, acc_own))

    def emit(l, acc):
        out = acc / l
        o_ref[...] = jnp.concatenate([out[:, :blk], out[:, blk:]], axis=0).T

    emit(l, acc)

    q_len = jnp.sqrt(_dot_nt(jnp.ones((8, LANES), BF16), _b16(q2 * q2))[0:1])
    score_bound = q_len * (kbound_ref[0] * (LOG2E * BF16_SLACK))
    redo = jnp.max(jnp.where(score_bound < EXP_HEADROOM, 0.0, 1.0))
    redo = jnp.maximum(redo, jnp.maximum(jnp.max(jnp.where(jnp.abs(l) < jnp.inf, 0.0, 1.0)),
                                         jnp.max(jnp.where(jnp.abs(acc) < jnp.inf, 0.0, 1.0))))

    def any_block_of(g, j):
        return jnp.minimum(2 * g + j, nb - 1)

    def score_group(g, s_ref):
        col_max = []
        for j in range(2):
            n = any_block_of(g, j)
            s = _dot_nt(k_ref[n], qb) + (base_ref[...] + pad)
            s_ref[j] = s
            col_max.append(jnp.max(s, axis=0, keepdims=True) + sel_ref[pl.ds(n, 1), :])
        return col_max

    def consume_group(g, s_ref, col_max, m, l, acc):
        n0, n1 = any_block_of(g, 0), any_block_of(g, 1)
        m_new = jnp.maximum(m, jnp.maximum(col_max[0], col_max[1]))
        alpha = jnp.exp2(m - m_new)
        l = alpha * l
        acc = alpha * acc
        for j, n in enumerate((n0, n1)):
            e = jnp.exp2(s_ref[j] - (m_new - sel_ref[pl.ds(n, 1), :]))
            l = l + jnp.sum(e, axis=0, keepdims=True)
            acc = acc + weighted_values(n, e)
        return m_new, l, acc

    @pl.when(redo > 0.0)
    def _():
        max_a = score_group(0, sa_ref)
        s = _dot_nt(k_ref[i], qb) + ownb_ref[...]
        m_own = jnp.max(s, axis=0, keepdims=True)
        e = jnp.exp2(s - m_own)
        l_own = jnp.sum(e, axis=0, keepdims=True)
        acc_own = weighted_values(i, e)

        def two_groups(u, carry):
            max_a0, max_a1, m, l, acc = carry
            max_b = score_group(2 * u + 1, sb_ref)
            m, l, acc = consume_group(2 * u, sa_ref, (max_a0, max_a1), m, l, acc)
            max_a = score_group(2 * u + 2, sa_ref)
            m, l, acc = consume_group(2 * u + 1, sb_ref, max_b, m, l, acc)
            return max_a[0], max_a[1], m, l, acc

        init = (max_a[0], max_a[1], m_own, l_own, acc_own)
        _, _, _, l, acc = lax.fori_loop(0, (i + 3) // 4, two_groups, init)
        emit(l, acc)


def _moba(proj, q_norm, k, vt, km, k_bound):
    s = proj.shape[0]
    n_pairs, nb, blk, _ = k.shape
    heads = n_pairs * (LANES // HEAD)
    slopes = jnp.exp2(-ALIBI_MAX * jnp.arange(1, heads + 1, dtype=F32) / heads)
    min_slope = [2.0 ** (-ALIBI_MAX * (2 * p + 2) / heads) for p in range(n_pairs)]
    depth = jnp.asarray([math.ceil((DEAD_EXPONENT / (sl * LOG2E) + blk - 1) / blk) for sl in min_slope], jnp.int32)
    qn = jnp.tile(q_norm, LANES // HEAD).reshape(1, LANES)
    smem = pl.BlockSpec(memory_space=pltpu.SMEM)
    return pl.pallas_call(
        _moba_kernel,
        grid=(n_pairs, nb),
        in_specs=[smem, smem, smem,
                  pl.BlockSpec((blk, LANES), lambda p, i: (i, p)),
                  pl.BlockSpec((1, LANES), lambda p, i: (0, 0)),
                  pl.BlockSpec((nb, LANES), lambda p, i: (0, p)),
                  pl.BlockSpec((None, nb, blk, LANES), lambda p, i: (p, 0, 0, 0)),
                  pl.BlockSpec((None, nb, LANES, blk), lambda p, i: (p, 0, 0, 0))],
        out_specs=pl.BlockSpec((blk, LANES), lambda p, i: (i, p)),
        out_shape=jax.ShapeDtypeStruct((s, n_pairs * LANES), F32),
        scratch_shapes=[pltpu.VMEM((nb, 2 * blk), F32), pltpu.VMEM((nb, 2 * blk), F32),
                        pltpu.VMEM((blk, 2 * blk), F32), pltpu.VMEM((blk, 2 * blk), F32),
                        pltpu.VMEM((2, blk, 2 * blk), F32), pltpu.VMEM((2, blk, 2 * blk), F32)],
        compiler_params=_cparams("parallel", "arbitrary"),
        name="moba",
    )(slopes, depth, k_bound, proj, qn, km, k, vt)


def kernel(x, mem, ffn1_norm, ffn1_w1, ffn1_w3, ffn1_w2, mix_norm, w_out, mem_norm, w_mem_kv, mem_q_norm, mem_k_norm, ffn2_norm, ffn2_w1, ffn2_w3, ffn2_w2, rwkv_w_in, rwkv_mu, rwkv_w0, rwkv_w2, rwkv_a0, rwkv_a2, rwkv_g2, rwkv_k_k, rwkv_k_a, rwkv_r_k, rwkv_ln_w, rwkv_ln_b, kv_norm, w_kv, kv_k_norm, moba_w_in, moba_q_norm):
    batch = x.shape[0]
    depth = ffn1_norm.shape[0]
    n_a = rwkv_w_in.shape[0]
    mix_w = rwkv_w0.shape[1]
    mem_w = w_mem_kv.shape[2] // 2
    ffn1_w = _ffn_weights(ffn1_w1, ffn1_w3, ffn1_w2)
    ffn2_w = _ffn_weights(ffn2_w1, ffn2_w3, ffn2_w2)
    w_out_b = w_out.astype(BF16)
    outs = []
    for bi in range(batch):
        xb, memb = x[bi], mem[bi]
        shared = None
        for l in range(depth):
            xb = _ffn(xb, ffn1_norm[l], ffn1_w, l)
            if l < n_a:
                proj = _norm_proj(xb, mix_norm[l], rwkv_w_in[l])
                mix = _rwkv(proj, rwkv_mu[l], rwkv_w0[l], rwkv_w2[l], rwkv_a0[l], rwkv_a2[l], rwkv_g2[l],
                            rwkv_k_k[l], rwkv_k_a[l], rwkv_r_k[l], rwkv_ln_w[l], rwkv_ln_b[l])
                mem_col = rwkv_mu.shape[1] // mem_w
            else:
                j = l - n_a
                proj = _norm_proj(xb, mix_norm[l], moba_w_in[j])
                mix = _moba(proj, moba_q_norm[j], *shared)
                mem_col = mix_w // mem_w
            mk, mv = _mem_kv(memb, mem_norm[l], w_mem_kv[l], mem_k_norm[l])
            mem_out = _mem_attn(proj, mem_col, mem_q_norm[l], mk, mv)
            xb = _ffn(xb, ffn2_norm[l], ffn2_w, l, mixer=(mix, mem_out, w_out_b))
            if l == n_a - 1:
                shared = _shared_kv(xb, kv_norm, w_kv, kv_k_norm)
        outs.append(xb)
    return jnp.stack(outs)
```

```python
import functools
import math

import jax
import jax.numpy as jnp
from jax import lax
from jax.experimental import pallas as pl
from jax.experimental.pallas import tpu as pltpu

F32 = jnp.float32
BF16 = jnp.bfloat16
HI = lax.Precision.HIGHEST

HEAD = 64
LANES = 128
MXU = 256
CHUNK = 64
RWKV_CHUNKS_PER_STEP = 4
MOBA_BLOCK = 256
MOBA_TOPK = 3
ALIBI_MAX = 8.0
NORM_EPS = 1e-6
GN_EPS = 64e-5
VMEM_LIMIT = 56 * 1024 * 1024
NEG_INF = float("-inf")
LOG2E = 1.4426950408889634
DEAD_EXPONENT = 160.0
EXP_HEADROOM = 80.0
BF16_SLACK = 1.02


def _cparams(*sem):
    return pltpu.CompilerParams(dimension_semantics=sem, vmem_limit_bytes=VMEM_LIMIT)


def _const_spec(shape):
    return pl.BlockSpec(shape, lambda *_: (0,) * len(shape), pipeline_mode=pl.Buffered(1))


def _rms(x, g):
    return x * lax.rsqrt(jnp.mean(x * x, axis=-1, keepdims=True) + NORM_EPS) * g


def _b16(x):
    return x.astype(BF16)


def _split_bf16(x):
    hi = x.astype(BF16)
    return hi, (x - hi.astype(F32)).astype(BF16)


def _dot(a, b, precision=None):
    return jnp.dot(a, b, preferred_element_type=F32, precision=precision)


def _dot_nt(a, b, precision=None):
    return lax.dot_general(a, b, (((1,), (1,)), ((), ())), preferred_element_type=F32, precision=precision)


def _dot_tn(a, b, precision=None):
    return lax.dot_general(a, b, (((0,), (0,)), ((), ())), preferred_element_type=F32, precision=precision)


def _head_sum_matrix(scale=1.0):
    i = lax.broadcasted_iota(jnp.int32, (LANES, LANES), 0) // HEAD
    j = lax.broadcasted_iota(jnp.int32, (LANES, LANES), 1) // HEAD
    return jnp.where(i == j, F32(scale), F32(0.0))


def _head_rms(x, g):
    hi, lo = _split_bf16(x * x)
    mean_matrix = _b16(_head_sum_matrix(1.0 / HEAD))
    ms = _dot(hi, mean_matrix) + _dot(lo, mean_matrix)
    return x * lax.rsqrt(ms + NORM_EPS) * g


def _swiglu_half_step(x, g_ref, w1_ref, w3_ref, w2_ref, acc_ref, f_chunk):
    h = _rms(x, g_ref[...]).astype(BF16)
    n_chunks = w1_ref.shape[0] // f_chunk
    for c in range(n_chunks):
        sl = slice(c * f_chunk, (c + 1) * f_chunk)
        a = _dot_nt(h, w1_ref[sl, :])
        b = _dot_nt(h, w3_ref[sl, :])
        act = (a * jax.nn.sigmoid(a) * b).astype(BF16)
        part = _dot(act, w2_ref[sl, :])
        if c == 0:
            acc_ref[...] = part
        else:
            acc_ref[...] += part
    return x + 0.5 * acc_ref[...]


def _ffn_kernel(x_ref, g_ref, w1_ref, w3_ref, w2_ref, o_ref, acc_ref, *, f_chunk):
    o_ref[...] = _swiglu_half_step(x_ref[...], g_ref, w1_ref, w3_ref, w2_ref, acc_ref, f_chunk)


def _mix_ffn_kernel(x_ref, mix_ref, mem_ref, wo_ref, g_ref, w1_ref, w3_ref, w2_ref, o_ref, acc_ref, *, f_chunk):
    na = mix_ref.shape[1]
    x = (x_ref[...] + _dot(mix_ref[...].astype(BF16), wo_ref[:na, :])
         + _dot(mem_ref[...].astype(BF16), wo_ref[na:, :]))
    o_ref[...] = _swiglu_half_step(x, g_ref, w1_ref, w3_ref, w2_ref, acc_ref, f_chunk)


def _cast_pad_kernel(x_ref, o_ref, *, axis):
    x = x_ref[...].astype(BF16)
    n = x.shape[axis]
    if axis == 0:
        o_ref[:n, :] = x
        o_ref[n:, :] = jnp.zeros((o_ref.shape[0] - n, x.shape[1]), BF16)
    else:
        o_ref[:, :n] = x
        o_ref[:, n:] = jnp.zeros((x.shape[0], o_ref.shape[1] - n), BF16)


def _cast_pad(w, axis, pad, *, tile=256):
    layers, a, b = w.shape
    if axis == 2:
        grid, in_blk, out_blk = (layers, a // tile), (None, tile, b), (None, tile, b + pad)
        index = lambda l, i: (l, i, 0)
    else:
        grid, in_blk, out_blk = (layers, b // tile), (None, a, tile), (None, a + pad, tile)
        index = lambda l, i: (l, 0, i)
    out_shape = (layers, a + pad, b) if axis == 1 else (layers, a, b + pad)
    return pl.pallas_call(
        functools.partial(_cast_pad_kernel, axis=axis - 1),
        grid=grid,
        in_specs=[pl.BlockSpec(in_blk, index)],
        out_specs=pl.BlockSpec(out_blk, index),
        out_shape=jax.ShapeDtypeStruct(out_shape, BF16),
        compiler_params=_cparams("parallel", "parallel"),
        name="cast_pad",
    )(w)


def _ffn_weights(w1, w3, w2, f_chunk=MXU):
    f = w1.shape[2]
    pad = -(-f // f_chunk) * f_chunk - f
    return (_cast_pad(jnp.swapaxes(w1, 1, 2), 1, pad), _cast_pad(jnp.swapaxes(w3, 1, 2), 1, pad),
            _cast_pad(w2, 1, pad))


def _layer_spec(shape, layer):
    return pl.BlockSpec((None,) + shape, lambda *_: (layer,) + (0,) * len(shape), pipeline_mode=pl.Buffered(1))


def _ffn(x, g, weights, layer, mixer=None, *, tm=512, f_chunk=MXU):
    s, d = x.shape
    w1, w3, w2 = weights
    fp = w2.shape[1]
    tm = min(tm, s)
    row_spec = pl.BlockSpec((tm, d), lambda i: (i, 0))
    ffn_specs = [_const_spec((1, d)), _layer_spec((fp, d), layer), _layer_spec((fp, d), layer),
                 _layer_spec((fp, d), layer)]
    ffn_args = (g.reshape(1, d), w1, w3, w2)
    if mixer is None:
        body, specs, args = _ffn_kernel, [row_spec] + ffn_specs, (x,) + ffn_args
    else:
        mix, mem_out, w_out = mixer
        na, nb = mix.shape[1], mem_out.shape[1]
        body = _mix_ffn_kernel
        specs = [row_spec, pl.BlockSpec((tm, na), lambda i: (i, 0)), pl.BlockSpec((tm, nb), lambda i: (i, 0)),
                 _layer_spec((na + nb, d), layer)] + ffn_specs
        args = (x, mix, mem_out, w_out) + ffn_args
    return pl.pallas_call(
        functools.partial(body, f_chunk=f_chunk),
        grid=(s // tm,),
        in_specs=specs,
        out_specs=row_spec,
        out_shape=jax.ShapeDtypeStruct((s, d), F32),
        scratch_shapes=[pltpu.VMEM((tm, d), F32)],
        compiler_params=_cparams("parallel"),
        name="ffn" if mixer is None else "mix_ffn",
    )(*args)


def _norm_proj_kernel(x_ref, g_ref, w_ref, o_ref):
    h = _rms(x_ref[...], g_ref[...]).astype(BF16)
    o_ref[...] = _dot(h, w_ref[...])


def _norm_proj(x, g, w, *, tm=512):
    s, d = x.shape
    n = w.shape[1]
    tm = min(tm, s)
    return pl.pallas_call(
        _norm_proj_kernel,
        grid=(s // tm,),
        in_specs=[pl.BlockSpec((tm, d), lambda i: (i, 0)), _const_spec((1, d)), _const_spec((d, n))],
        out_specs=pl.BlockSpec((tm, n), lambda i: (i, 0)),
        out_shape=jax.ShapeDtypeStruct((s, n), F32),
        compiler_params=_cparams("parallel"),
        name="norm_proj",
    )(x, g.reshape(1, d), w.astype(BF16))


def _mem_kv_kernel(mem_ref, g_ref, w_ref, kn_ref, k_ref, v_ref):
    h = _rms(mem_ref[...], g_ref[...]).astype(BF16)
    kv = _dot(h, w_ref[...])
    width = k_ref.shape[1]
    for p in range(width // LANES):
        sl = slice(p * LANES, (p + 1) * LANES)
        k_ref[:, sl] = _head_rms(kv[:, sl], kn_ref[...])
    v_ref[...] = kv[:, width:]


def _mem_kv(mem, g, w, k_norm):
    m, d = mem.shape
    width = w.shape[1] // 2
    kn = jnp.tile(k_norm, LANES // HEAD).reshape(1, LANES)
    return pl.pallas_call(
        _mem_kv_kernel,
        grid=(1,),
        in_specs=[_const_spec((m, d)), _const_spec((1, d)), _const_spec((d, 2 * width)), _const_spec((1, LANES))],
        out_specs=[pl.BlockSpec((m, width), lambda i: (0, 0)), pl.BlockSpec((m, width), lambda i: (0, 0))],
        out_shape=[jax.ShapeDtypeStruct((m, width), F32)] * 2,
        compiler_params=_cparams("arbitrary"),
        name="mem_kv",
    )(mem, g.reshape(1, d), w.astype(BF16), kn)


def _mem_attn_kernel(q_ref, qn_ref, k_ref, v_ref, o_ref):
    lane = lax.broadcasted_iota(jnp.int32, (1, LANES), 1)
    width = q_ref.shape[1]
    for p in range(width // LANES):
        sl = slice(p * LANES, (p + 1) * LANES)
        q = _head_rms(q_ref[:, sl], qn_ref[...]) * (HEAD ** -0.5)
        k = k_ref[:, sl].astype(BF16)
        v = v_ref[:, sl].astype(BF16)
        out = None
        for h in range(LANES // HEAD):
            in_head = (lane // HEAD) == h
            qh = jnp.where(in_head, q, 0.0).astype(BF16)
            s = _dot_nt(qh, k)
            s = s - jnp.max(s, axis=-1, keepdims=True)
            e = jnp.exp(s)
            pr = e / jnp.sum(e, axis=-1, keepdims=True)
            oh = _dot(pr.astype(BF16), v)
            out = oh if out is None else jnp.where(in_head, oh, out)
        o_ref[:, sl] = out


def _mem_attn(proj, col_block, q_norm, k, v, *, tm=512):
    s = proj.shape[0]
    m, width = k.shape
    tm = min(tm, s)
    qn = jnp.tile(q_norm, LANES // HEAD).reshape(1, LANES)
    return pl.pallas_call(
        _mem_attn_kernel,
        grid=(s // tm,),
        in_specs=[pl.BlockSpec((tm, width), lambda i: (i, col_block)),
                  _const_spec((1, LANES)), _const_spec((m, width)), _const_spec((m, width))],
        out_specs=pl.BlockSpec((tm, width), lambda i: (i, 0)),
        out_shape=jax.ShapeDtypeStruct((s, width), F32),
        compiler_params=_cparams("parallel"),
        name="mem_attn",
    )(proj, qn, k, v)


def _stack_heads(z):
    lane = lax.broadcasted_iota(jnp.int32, z.shape, 2)
    zero = jnp.zeros_like(z)
    return jnp.concatenate([jnp.where(lane < HEAD, z, zero), jnp.where(lane >= HEAD, z, zero)], axis=1)


def _bdot(a, b):
    return lax.dot_general(a, b, (((2,), (1,)), ((0,), (0,))), preferred_element_type=F32)


def _bdot_nt(a, b):
    return lax.dot_general(a, b, (((2,), (2,)), ((0,), (0,))), preferred_element_type=F32)


def _bdot_tn(a, b):
    return lax.dot_general(a, b, (((1,), (1,)), ((0,), (0,))), preferred_element_type=F32)


def _rwkv_kernel(proj_ref, mu_ref, wlh_ref, wll_ref, g2_ref, w0_ref, a0_ref, kk_ref, ka_ref, rk_ref, lnw_ref, lnb_ref,
                 o_ref, last_ref, h_ref, r_s, k_s, v_s, a_s, lw_s, cum_s, g_s):
    c = CHUNK
    rows = proj_ref.shape[0]
    n_sub = rows // c
    n_pairs = h_ref.shape[0]
    width = n_pairs * LANES

    @pl.when(pl.program_id(0) == 0)
    def _():
        last_ref[...] = jnp.zeros_like(last_ref)
        h_ref[...] = jnp.zeros_like(h_ref)

    u = proj_ref[...]
    row = lax.broadcasted_iota(jnp.int32, u.shape, 0)
    prev = jnp.where(row == 0, last_ref[...], pltpu.roll(u, 1, 0))
    last_ref[...] = u[rows - 1:rows, :]
    u = u + (prev - u) * mu_ref[...]

    lora_in = u[:, 3 * width:3 * width + LANES]
    lane = lax.broadcasted_iota(jnp.int32, lora_in.shape, 1)
    lora_in = jnp.where(lane < HEAD, jnp.tanh(lora_in), lora_in)
    x_hi, x_lo = _split_bf16(lora_in)
    lo = _dot(x_hi, wlh_ref[...]) + (_dot(x_lo, wlh_ref[...]) + _dot(x_hi, wll_ref[...]))
    z = -(w0_ref[...] + lo[:, :width])
    softplus = jnp.maximum(z, 0.0) + jnp.log(1.0 + jnp.exp(-jnp.abs(z)))
    lw = -jnp.exp(-softplus - 0.5)
    a = jax.nn.sigmoid(a0_ref[...] + lo[:, width:])
    g = _dot(_b16(jax.nn.sigmoid(u[:, 3 * width + LANES:])), g2_ref[...])
    ti = lax.broadcasted_iota(jnp.int32, (rows, rows), 0)
    si = lax.broadcasted_iota(jnp.int32, (rows, rows), 1)
    tri = _b16(jnp.where((si <= ti) & (si // c == ti // c), F32(1.0), F32(0.0)))
    lw_hi, lw_lo = _split_bf16(lw)
    cum = _dot(tri, lw_hi) + _dot(tri, lw_lo)

    for sub in range(n_sub):
        rs = slice(sub * c, (sub + 1) * c)
        for p in range(n_pairs):
            b_ = sub * n_pairs + p
            sl = slice(p * LANES, (p + 1) * LANES)
            r_s[b_] = u[rs, sl]
            k_s[b_] = u[rs, width + p * LANES:width + (p + 1) * LANES]
            v_s[b_] = u[rs, 2 * width + p * LANES:2 * width + (p + 1) * LANES]
            a_s[b_] = a[rs, sl]
            lw_s[b_] = lw[rs, sl]
            cum_s[b_] = cum[rs, sl]
            g_s[b_] = g[rs, sl]

    np_ = n_sub * n_pairs

    def per_pair(ref):
        return jnp.concatenate([ref[...]] * n_sub, axis=0)
    hsum = _b16(_head_sum_matrix())

    def head_sum(z):
        return _dot(_b16(z).reshape(np_ * c, LANES), hsum).reshape(np_, c, LANES)

    col = lax.broadcasted_iota(jnp.int32, (np_, c, LANES), 2) % c
    trow = lax.broadcasted_iota(jnp.int32, (np_, c, LANES), 1)
    strict = col < trow
    incl = col <= trow
    bi = lax.broadcasted_iota(jnp.int32, (np_, LANES, LANES), 1)
    bj = lax.broadcasted_iota(jnp.int32, (np_, LANES, LANES), 2)
    eye = bi == bj
    same_head = (bi // HEAD) == (bj // HEAD)

    r, k, v, a_, lw_, cum_ = r_s[...], k_s[...], v_s[...], a_s[...], lw_s[...], cum_s[...]
    kk = k * per_pair(kk_ref)
    kk = kk / jnp.maximum(jnp.sqrt(head_sum(kk * kk)), 1e-12)
    kmod = k * (1.0 + (a_ - 1.0) * per_pair(ka_ref))
    b = kk * a_
    bonus = head_sum(r * kmod * per_pair(rk_ref)) * v

    tot = cum_[:, c - 1:c, :]
    at = -kk * jnp.exp(cum_ - lw_)
    rt = r * jnp.exp(cum_)
    einv = jnp.exp(-cum_)
    eend = jnp.exp(tot - cum_)
    bh, kh = _b16(b * einv), _b16(kmod * einv)
    bt, kt = _b16(b * eend), _b16(kmod * eend)
    at_b = _b16(at)

    aa = _bdot_nt(jnp.concatenate([at_b, _b16(rt)], axis=1),
                  jnp.concatenate([_stack_heads(bh), _stack_heads(kh)], axis=1))
    a_ab = jnp.where(strict, aa[:, :c, :LANES], 0.0)
    a_ak = _b16(jnp.where(strict, aa[:, :c, LANES:], 0.0))
    a_rb = _b16(jnp.where(incl, aa[:, c:, :LANES], 0.0))
    a_rk = _b16(jnp.where(incl, aa[:, c:, LANES:], 0.0))

    t_side = jnp.where(col == trow, 1.0, 0.0) + jnp.where((trow == col + 1) & (trow % 2 == 1), a_ab, 0.0)
    size = 2
    while size < c:
        e_mask = ((trow // (2 * size)) == (col // (2 * size))) & (trow % (2 * size) >= size) & (col % (2 * size) < size)
        t_b = _b16(t_side)
        t_blk = _stack_heads(t_b)
        te = _bdot(t_b, _stack_heads(_b16(jnp.where(e_mask, a_ab, 0.0))))
        t_side = t_side + _bdot(_b16(te), t_blk)
        size *= 2
    t_pair = _b16(t_side)

    v_b = _b16(v)
    v_st = _stack_heads(v_b)
    g0 = _bdot(a_ak, v_st)
    uw = _bdot(t_pair, jnp.concatenate([_stack_heads(_b16(g0)), _stack_heads(at_b)], axis=2))
    u0, w = _b16(uw[:, :, :LANES]), _b16(uw[:, :, LANES:])
    rw = rt + _bdot(a_rb, _stack_heads(w))
    y0 = _bdot(a_rb, _stack_heads(u0)) + _bdot(a_rk, v_st)
    m1 = jnp.where(eye, jnp.exp(tot), 0.0) + jnp.where(same_head, _bdot_tn(bt, w), 0.0)
    m0 = jnp.where(same_head, _bdot_tn(bt, u0) + _bdot_tn(kt, v_b), 0.0)

    rw_b, m1_b = _b16(rw), _b16(m1)
    h = h_ref[...]
    ys = []
    for sub in range(n_sub):
        sl = slice(sub * n_pairs, (sub + 1) * n_pairs)
        h_b = _b16(h)
        ys.append(_bdot(rw_b[sl], h_b) + y0[sl])
        h = _bdot(m1_b[sl], h_b) + m0[sl]
    h_ref[...] = h
    y = jnp.concatenate(ys, axis=0)

    mean = head_sum(y) * (1.0 / HEAD)
    yc = y - mean
    var = head_sum(yc * yc) * (1.0 / HEAD)
    yn = yc * lax.rsqrt(var + GN_EPS) * per_pair(lnw_ref) + per_pair(lnb_ref)
    out = (yn + bonus) * g_s[...]
    for sub in range(n_sub):
        for p in range(n_pairs):
            o_ref[sub * c:(sub + 1) * c, p * LANES:(p + 1) * LANES] = out[sub * n_pairs + p]


def _rwkv(proj, mu, w0, w2, a0, a2, g2, k_k, k_a, r_k, ln_w, ln_b):
    s = proj.shape[0]
    width = w0.shape[0]
    n_pairs = width // LANES
    shift_w = mu.shape[0]
    dl, al = w2.shape[0], a2.shape[0]
    assert dl + al == LANES and g2.shape[0] == LANES and shift_w == 3 * width + 2 * LANES
    wl = jnp.zeros((LANES, 2 * width), F32).at[:dl, :width].set(w2).at[dl:, width:].set(a2)
    wl_hi, wl_lo = _split_bf16(wl)

    def per_pair(t):
        return t.reshape(n_pairs, 1, LANES)

    c = CHUNK
    rows = c * RWKV_CHUNKS_PER_STEP
    pair_spec = _const_spec((n_pairs, 1, LANES))
    return pl.pallas_call(
        _rwkv_kernel,
        grid=(s // rows,),
        in_specs=[pl.BlockSpec((rows, shift_w), lambda i: (i, 0)),
                  _const_spec((1, shift_w)), _const_spec((LANES, 2 * width)), _const_spec((LANES, 2 * width)),
                  _const_spec((LANES, width)),
                  _const_spec((1, width)), _const_spec((1, width)),
                  pair_spec, pair_spec, pair_spec, pair_spec, pair_spec],
        out_specs=pl.BlockSpec((rows, width), lambda i: (i, 0)),
        out_shape=jax.ShapeDtypeStruct((s, width), F32),
        scratch_shapes=[pltpu.VMEM((1, shift_w), F32), pltpu.VMEM((n_pairs, LANES, LANES), F32)]
                       + [pltpu.VMEM((RWKV_CHUNKS_PER_STEP * n_pairs, c, LANES), F32)] * 7,
        compiler_params=_cparams("arbitrary"),
        name="rwkv7",
    )(proj, mu.reshape(1, shift_w), wl_hi, wl_lo, g2.astype(BF16), w0.reshape(1, width), a0.reshape(1, width),
      per_pair(k_k), per_pair(k_a), per_pair(r_k.reshape(-1)), per_pair(ln_w), per_pair(ln_b))


def _shared_kv_kernel(x_ref, g_ref, wk_ref, wvt_ref, kn_ref, k_ref, vt_ref, km_ref):
    h = _rms(x_ref[...], g_ref[...]).astype(BF16)
    k = _dot(h, wk_ref[...])
    n_pairs = k_ref.shape[0]
    for p in range(n_pairs):
        kp = _head_rms(k[:, p * LANES:(p + 1) * LANES], kn_ref[...])
        k_ref[p, 0] = kp.astype(BF16)
        km_ref[0, :, p * LANES:(p + 1) * LANES] = jnp.mean(kp, axis=0, keepdims=True)
    vt = _dot_nt(wvt_ref[...], h)
    vt_ref[:, 0] = vt.reshape(n_pairs, LANES, vt.shape[1]).astype(BF16)


def _shared_kv(x, g, w_kv, k_norm):
    s, d = x.shape
    width = w_kv.shape[1] // 2
    n_pairs = width // LANES
    blk = MOBA_BLOCK
    nb = s // blk
    w = w_kv.astype(BF16)
    kn = jnp.tile(k_norm, LANES // HEAD).reshape(1, LANES)
    k, vt, km = pl.pallas_call(
        _shared_kv_kernel,
        grid=(nb,),
        in_specs=[pl.BlockSpec((blk, d), lambda i: (i, 0)), _const_spec((1, d)),
                  _const_spec((d, width)), _const_spec((width, d)), _const_spec((1, LANES))],
        out_specs=[pl.BlockSpec((n_pairs, 1, blk, LANES), lambda i: (0, i, 0, 0)),
                   pl.BlockSpec((n_pairs, 1, LANES, blk), lambda i: (0, i, 0, 0)),
                   pl.BlockSpec((1, 1, width), lambda i: (i, 0, 0))],
        out_shape=[jax.ShapeDtypeStruct((n_pairs, nb, blk, LANES), BF16),
                   jax.ShapeDtypeStruct((n_pairs, nb, LANES, blk), BF16),
                   jax.ShapeDtypeStruct((nb, 1, width), F32)],
        compiler_params=_cparams("parallel"),
        name="shared_kv",
    )(x, g.reshape(1, d), w, w_kv[:, width:].T.astype(BF16), kn)
    k_bound = (HEAD ** 0.5) * jnp.max(jnp.abs(k_norm)).reshape(1)
    return k, vt, km.reshape(nb, width), k_bound


def _moba_kernel(slopes_ref, depth_ref, kbound_ref, q_ref, qn_ref, km_ref, k_ref, vt_ref, o_ref,
                 sel_ref, fac_ref, ownb_ref, base_ref, sa_ref, sb_ref):
    p = pl.program_id(0)
    i = pl.program_id(1)
    blk = MOBA_BLOCK
    nb = km_ref.shape[0]
    col = lax.broadcasted_iota(jnp.int32, (1, 2 * blk), 1)
    slope = jnp.where(col < blk, slopes_ref[2 * p], slopes_ref[2 * p + 1]) * LOG2E
    pad = slope * (blk - 1)

    @pl.when(i == 0)
    def _():
        key_pos = lax.broadcasted_iota(jnp.int32, (blk, 2 * blk), 0)
        qry_pos = lax.broadcasted_iota(jnp.int32, (blk, 2 * blk), 1) % blk
        dist = (qry_pos - key_pos).astype(F32)
        ownb_ref[...] = jnp.where(dist >= 0.0, -slope * dist, NEG_INF)
        base_ref[...] = -slope * dist - pad

    q = _head_rms(q_ref[...], qn_ref[...]) * (HEAD ** -0.5)
    lane = lax.broadcasted_iota(jnp.int32, (1, LANES), 1)
    q2 = jnp.concatenate([jnp.where(lane < HEAD, q, 0.0), jnp.where(lane >= HEAD, q, 0.0)], axis=0)
    qb = (q2 * LOG2E).astype(BF16)

    start = jnp.maximum(i - depth_ref[p] + 1, 0)
    n_iter = (i - start + 3) // 4

    def block_of(g, j):
        return jnp.minimum(start + 2 * g + j, nb - 1)

    def fast_scores(g, s_ref):
        for j in range(2):
            s_ref[j] = _dot_nt(k_ref[block_of(g, j)], qb)

    fast_scores(0, sa_ref)

    n_idx = lax.broadcasted_iota(jnp.int32, (nb, 2 * blk), 0)
    q_hi, q_lo = _split_bf16(q2)
    km_hi, km_lo = _split_bf16(km_ref[...])
    gate = _dot_nt(km_hi, q_hi) + (_dot_nt(km_lo, q_hi) + _dot_nt(km_hi, q_lo))
    gate = jnp.where(n_idx < i, gate, NEG_INF)
    sel_bias = jnp.full(gate.shape, NEG_INF, F32)
    for _ in range(MOBA_TOPK):
        mx = jnp.max(gate, axis=0, keepdims=True)
        first = jnp.min(jnp.where(gate == mx, n_idx, nb), axis=0, keepdims=True)
        pick = (n_idx == first) & (mx > NEG_INF)
        sel_bias = jnp.where(pick, 0.0, sel_bias)
        gate = jnp.where(pick, NEG_INF, gate)
    sel_ref[...] = sel_bias - slope * ((i - n_idx) * blk).astype(F32)

    fac_ref[...] = jnp.exp2(sel_ref[...] + pad)

    def weighted_values(n, e):
        eb = e.astype(BF16)
        return jnp.concatenate([_dot(vt_ref[n, :HEAD, :], eb[:, :blk]), _dot(vt_ref[n, HEAD:, :], eb[:, blk:])],
                               axis=1)

    e = jnp.exp2(_dot_nt(k_ref[i], qb) + ownb_ref[...])
    l_own = jnp.sum(e, axis=0, keepdims=True)
    acc_own = weighted_values(i, e)

    def fast_consume(g, s_ref, l, acc):
        for j in range(2):
            n = block_of(g, j)
            e = jnp.exp2(s_ref[j] + base_ref[...])
            f = fac_ref[pl.ds(n, 1), :]
            l = l + f * jnp.sum(e, axis=0, keepdims=True)
            acc = acc + f * weighted_values(n, e)
        return l, acc

    def fast_two_groups(u, carry):
        fast_scores(2 * u + 1, sb_ref)
        carry = fast_consume(2 * u, sa_ref, *carry)
        fast_scores(2 * u + 2, sa_ref)
        return fast_consume(2 * u + 1, sb_ref, *carry)

    l, acc = lax.fori_loop(0, n_iter, fast_two_groups, (l_own, acc_own))

    def emit(l, acc):
        out = acc / l
        o_ref[...] = jnp.concatenate([out[:, :blk], out[:, blk:]], axis=0).T

    emit(l, acc)

    q_len = jnp.sqrt(_dot_nt(jnp.ones((8, LANES), BF16), _b16(q2 * q2))[0:1])
    score_bound = q_len * (kbound_ref[0] * (LOG2E * BF16_SLACK))
    redo = jnp.max(jnp.where(score_bound < EXP_HEADROOM, 0.0, 1.0))
    redo = jnp.maximum(redo, jnp.maximum(jnp.max(jnp.where(jnp.abs(l) < jnp.inf, 0.0, 1.0)),
                                         jnp.max(jnp.where(jnp.abs(acc) < jnp.inf, 0.0, 1.0))))

    def any_block_of(g, j):
        return jnp.minimum(2 * g + j, nb - 1)

    def score_group(g, s_ref):
        col_max = []
        for j in range(2):
            n = any_block_of(g, j)
            s = _dot_nt(k_ref[n], qb) + (base_ref[...] + pad)
            s_ref[j] = s
            col_max.append(jnp.max(s, axis=0, keepdims=True) + sel_ref[pl.ds(n, 1), :])
        return col_max

    def consume_group(g, s_ref, col_max, m, l, acc):
        n0, n1 = any_block_of(g, 0), any_block_of(g, 1)
        m_new = jnp.maximum(m, jnp.maximum(col_max[0], col_max[1]))
        alpha = jnp.exp2(m - m_new)
        l = alpha * l
        acc = alpha * acc
        for j, n in enumerate((n0, n1)):
            e = jnp.exp2(s_ref[j] - (m_new - sel_ref[pl.ds(n, 1), :]))
            l = l + jnp.sum(e, axis=0, keepdims=True)
            acc = acc + weighted_values(n, e)
        return m_new, l, acc

    @pl.when(redo > 0.0)
    def _():
        max_a = score_group(0, sa_ref)
        s = _dot_nt(k_ref[i], qb) + ownb_ref[...]
        m_own = jnp.max(s, axis=0, keepdims=True)
        e = jnp.exp2(s - m_own)
        l_own = jnp.sum(e, axis=0, keepdims=True)
        acc_own = weighted_values(i, e)

        def two_groups(u, carry):
            max_a0, max_a1, m, l, acc = carry
            max_b = score_group(2 * u + 1, sb_ref)
            m, l, acc = consume_group(2 * u, sa_ref, (max_a0, max_a1), m, l, acc)
            max_a = score_group(2 * u + 2, sa_ref)
            m, l, acc = consume_group(2 * u + 1, sb_ref, max_b, m, l, acc)
            return max_a[0], max_a[1], m, l, acc

        init = (max_a[0], max_a[1], m_own, l_own, acc_own)
        _, _, _, l, acc = lax.fori_loop(0, (i + 3) // 4, two_groups, init)
        emit(l, acc)


def _moba(proj, q_norm, k, vt, km, k_bound):
    s = proj.shape[0]
    n_pairs, nb, blk, _ = k.shape
    heads = n_pairs * (LANES // HEAD)
    slopes = jnp.exp2(-ALIBI_MAX * jnp.arange(1, heads + 1, dtype=F32) / heads)
    min_slope = [2.0 ** (-ALIBI_MAX * (2 * p + 2) / heads) for p in range(n_pairs)]
    depth = jnp.asarray([math.ceil((DEAD_EXPONENT / (sl * LOG2E) + blk - 1) / blk) for sl in min_slope], jnp.int32)
    qn = jnp.tile(q_norm, LANES // HEAD).reshape(1, LANES)
    smem = pl.BlockSpec(memory_space=pltpu.SMEM)
    return pl.pallas_call(
        _moba_kernel,
        grid=(n_pairs, nb),
        in_specs=[smem, smem, smem,
                  pl.BlockSpec((blk, LANES), lambda p, i: (i, p)),
                  pl.BlockSpec((1, LANES), lambda p, i: (0, 0)),
                  pl.BlockSpec((nb, LANES), lambda p, i: (0, p)),
                  pl.BlockSpec((None, nb, blk, LANES), lambda p, i: (p, 0, 0, 0)),
                  pl.BlockSpec((None, nb, LANES, blk), lambda p, i: (p, 0, 0, 0))],
        out_specs=pl.BlockSpec((blk, LANES), lambda p, i: (i, p)),
        out_shape=jax.ShapeDtypeStruct((s, n_pairs * LANES), F32),
        scratch_shapes=[pltpu.VMEM((nb, 2 * blk), F32), pltpu.VMEM((nb, 2 * blk), F32),
                        pltpu.VMEM((blk, 2 * blk), F32), pltpu.VMEM((blk, 2 * blk), F32),
                        pltpu.VMEM((2, blk, 2 * blk), F32), pltpu.VMEM((2, blk, 2 * blk), F32)],
        compiler_params=_cparams("parallel", "arbitrary"),
        name="moba",
    )(slopes, depth, k_bound, proj, qn, km, k, vt)


def kernel(x, mem, ffn1_norm, ffn1_w1, ffn1_w3, ffn1_w2, mix_norm, w_out, mem_norm, w_mem_kv, mem_q_norm, mem_k_norm, ffn2_norm, ffn2_w1, ffn2_w3, ffn2_w2, rwkv_w_in, rwkv_mu, rwkv_w0, rwkv_w2, rwkv_a0, rwkv_a2, rwkv_g2, rwkv_k_k, rwkv_k_a, rwkv_r_k, rwkv_ln_w, rwkv_ln_b, kv_norm, w_kv, kv_k_norm, moba_w_in, moba_q_norm):
    batch = x.shape[0]
    depth = ffn1_norm.shape[0]
    n_a = rwkv_w_in.shape[0]
    mix_w = rwkv_w0.shape[1]
    mem_w = w_mem_kv.shape[2] // 2
    ffn1_w = _ffn_weights(ffn1_w1, ffn1_w3, ffn1_w2)
    ffn2_w = _ffn_weights(ffn2_w1, ffn2_w3, ffn2_w2)
    w_out_b = w_out.astype(BF16)
    outs = []
    for bi in range(batch):
        xb, memb = x[bi], mem[bi]
        shared = None
        for l in range(depth):
            xb = _ffn(xb, ffn1_norm[l], ffn1_w, l)
            if l < n_a:
                proj = _norm_proj(xb, mix_norm[l], rwkv_w_in[l])
                mix = _rwkv(proj, rwkv_mu[l], rwkv_w0[l], rwkv_w2[l], rwkv_a0[l], rwkv_a2[l], rwkv_g2[l],
                            rwkv_k_k[l], rwkv_k_a[l], rwkv_r_k[l], rwkv_ln_w[l], rwkv_ln_b[l])
                mem_col = rwkv_mu.shape[1] // mem_w
            else:
                j = l - n_a
                proj = _norm_proj(xb, mix_norm[l], moba_w_in[j])
                mix = _moba(proj, moba_q_norm[j], *shared)
                mem_col = mix_w // mem_w
            mk, mv = _mem_kv(memb, mem_norm[l], w_mem_kv[l], mem_k_norm[l])
            mem_out = _mem_attn(proj, mem_col, mem_q_norm[l], mk, mv)
            xb = _ffn(xb, ffn2_norm[l], ffn2_w, l, mixer=(mix, mem_out, w_out_b))
            if l == n_a - 1:
                shared = _shared_kv(xb, kv_norm, w_kv, kv_k_norm)
        outs.append(xb)
    return jnp.stack(outs)
```

```python
import functools
import math

import jax
import jax.numpy as jnp
from jax import lax
from jax.experimental import pallas as pl
from jax.experimental.pallas import tpu as pltpu

F32 = jnp.float32
BF16 = jnp.bfloat16

HEAD = 64
LANES = 128
MXU = 256
CHUNK = 64
RWKV_CHUNKS_PER_STEP = 4
MOBA_BLOCK = 256
MOBA_TOPK = 3
ALIBI_MAX = 8.0
NORM_EPS = 1e-6
GN_EPS = 64e-5
VMEM_LIMIT = 56 * 1024 * 1024
NEG_INF = float("-inf")
LOG2E = 1.4426950408889634
DEAD_EXPONENT = 160.0
EXP_HEADROOM = 80.0
BF16_SLACK = 1.02


def _cparams(*sem):
    return pltpu.CompilerParams(dimension_semantics=sem, vmem_limit_bytes=VMEM_LIMIT)


def _const_spec(shape):
    return pl.BlockSpec(shape, lambda *_: (0,) * len(shape), pipeline_mode=pl.Buffered(1))


def _rms(x, g):
    return x * lax.rsqrt(jnp.mean(x * x, axis=-1, keepdims=True) + NORM_EPS) * g


def _b16(x):
    return x.astype(BF16)


def _split_bf16(x):
    hi = x.astype(BF16)
    return hi, (x - hi.astype(F32)).astype(BF16)


def _dot(a, b):
    return jnp.dot(a, b, preferred_element_type=F32)


def _dot_nt(a, b):
    return lax.dot_general(a, b, (((1,), (1,)), ((), ())), preferred_element_type=F32)


def _head_sum_matrix(scale=1.0):
    i = lax.broadcasted_iota(jnp.int32, (LANES, LANES), 0) // HEAD
    j = lax.broadcasted_iota(jnp.int32, (LANES, LANES), 1) // HEAD
    return jnp.where(i == j, F32(scale), F32(0.0))


def _head_rms(x, g):
    hi, lo = _split_bf16(x * x)
    mean_matrix = _b16(_head_sum_matrix(1.0 / HEAD))
    ms = _dot(hi, mean_matrix) + _dot(lo, mean_matrix)
    return x * lax.rsqrt(ms + NORM_EPS) * g


def _swiglu_half_step(x, g_ref, w1_ref, w3_ref, w2_ref, acc_ref, f_chunk):
    h = _rms(x, g_ref[...]).astype(BF16)
    n_chunks = w1_ref.shape[0] // f_chunk
    for c in range(n_chunks):
        sl = slice(c * f_chunk, (c + 1) * f_chunk)
        a = _dot_nt(h, w1_ref[sl, :])
        b = _dot_nt(h, w3_ref[sl, :])
        act = (a * jax.nn.sigmoid(a) * b).astype(BF16)
        part = _dot(act, w2_ref[sl, :])
        if c == 0:
            acc_ref[...] = part
        else:
            acc_ref[...] += part
    return x + 0.5 * acc_ref[...]


def _ffn_kernel(x_ref, g_ref, w1_ref, w3_ref, w2_ref, o_ref, acc_ref, *, f_chunk):
    o_ref[...] = _swiglu_half_step(x_ref[...], g_ref, w1_ref, w3_ref, w2_ref, acc_ref, f_chunk)


def _mix_ffn_kernel(x_ref, mix_ref, mem_ref, wo_ref, g_ref, w1_ref, w3_ref, w2_ref, o_ref, acc_ref, *, f_chunk):
    na = mix_ref.shape[1]
    x = (x_ref[...] + _dot(mix_ref[...].astype(BF16), wo_ref[:na, :])
         + _dot(mem_ref[...].astype(BF16), wo_ref[na:, :]))
    o_ref[...] = _swiglu_half_step(x, g_ref, w1_ref, w3_ref, w2_ref, acc_ref, f_chunk)


def _cast_pad_kernel(x_ref, o_ref, *, axis):
    x = x_ref[...].astype(BF16)
    n = x.shape[axis]
    if axis == 0:
        o_ref[:n, :] = x
        o_ref[n:, :] = jnp.zeros((o_ref.shape[0] - n, x.shape[1]), BF16)
    else:
        o_ref[:, :n] = x
        o_ref[:, n:] = jnp.zeros((x.shape[0], o_ref.shape[1] - n), BF16)


def _cast_pad(w, axis, pad, *, tile=256):
    layers, a, b = w.shape
    if axis == 2:
        grid, in_blk, out_blk = (layers, a // tile), (None, tile, b), (None, tile, b + pad)
        index = lambda l, i: (l, i, 0)
    else:
        grid, in_blk, out_blk = (layers, b // tile), (None, a, tile), (None, a + pad, tile)
        index = lambda l, i: (l, 0, i)
    out_shape = (layers, a + pad, b) if axis == 1 else (layers, a, b + pad)
    return pl.pallas_call(
        functools.partial(_cast_pad_kernel, axis=axis - 1),
        grid=grid,
        in_specs=[pl.BlockSpec(in_blk, index)],
        out_specs=pl.BlockSpec(out_blk, index),
        out_shape=jax.ShapeDtypeStruct(out_shape, BF16),
        compiler_params=_cparams("parallel", "parallel"),
        name="cast_pad",
    )(w)


def _ffn_weights(w1, w3, w2, f_chunk=MXU):
    f = w1.shape[2]
    pad = -(-f // f_chunk) * f_chunk - f
    return (_cast_pad(jnp.swapaxes(w1, 1, 2), 1, pad), _cast_pad(jnp.swapaxes(w3, 1, 2), 1, pad),
            _cast_pad(w2, 1, pad))


def _layer_spec(shape, layer):
    return pl.BlockSpec((None,) + shape, lambda *_: (layer,) + (0,) * len(shape), pipeline_mode=pl.Buffered(1))


def _ffn(x, g, weights, layer, mixer=None, *, tm=512, f_chunk=MXU):
    s, d = x.shape
    w1, w3, w2 = weights
    fp = w2.shape[1]
    tm = min(tm, s)
    row_spec = pl.BlockSpec((tm, d), lambda i: (i, 0))
    ffn_specs = [_const_spec((1, d)), _layer_spec((fp, d), layer), _layer_spec((fp, d), layer),
                 _layer_spec((fp, d), layer)]
    ffn_args = (g.reshape(1, d), w1, w3, w2)
    if mixer is None:
        body, specs, args = _ffn_kernel, [row_spec] + ffn_specs, (x,) + ffn_args
    else:
        mix, mem_out, w_out = mixer
        na, nb = mix.shape[1], mem_out.shape[1]
        body = _mix_ffn_kernel
        specs = [row_spec, pl.BlockSpec((tm, na), lambda i: (i, 0)), pl.BlockSpec((tm, nb), lambda i: (i, 0)),
                 _layer_spec((na + nb, d), layer)] + ffn_specs
        args = (x, mix, mem_out, w_out) + ffn_args
    return pl.pallas_call(
        functools.partial(body, f_chunk=f_chunk),
        grid=(s // tm,),
        in_specs=specs,
        out_specs=row_spec,
        out_shape=jax.ShapeDtypeStruct((s, d), F32),
        scratch_shapes=[pltpu.VMEM((tm, d), F32)],
        compiler_params=_cparams("parallel"),
        name="ffn" if mixer is None else "mix_ffn",
    )(*args)


def _norm_proj_kernel(x_ref, g_ref, w_ref, o_ref):
    h = _rms(x_ref[...], g_ref[...]).astype(BF16)
    o_ref[...] = _dot(h, w_ref[...])


def _norm_proj(x, g, w, *, tm=512):
    s, d = x.shape
    n = w.shape[1]
    tm = min(tm, s)
    return pl.pallas_call(
        _norm_proj_kernel,
        grid=(s // tm,),
        in_specs=[pl.BlockSpec((tm, d), lambda i: (i, 0)), _const_spec((1, d)), _const_spec((d, n))],
        out_specs=pl.BlockSpec((tm, n), lambda i: (i, 0)),
        out_shape=jax.ShapeDtypeStruct((s, n), F32),
        compiler_params=_cparams("parallel"),
        name="norm_proj",
    )(x, g.reshape(1, d), w.astype(BF16))


def _mem_kv_kernel(mem_ref, g_ref, w_ref, kn_ref, k_ref, v_ref):
    h = _rms(mem_ref[...], g_ref[...]).astype(BF16)
    kv = _dot(h, w_ref[...])
    width = k_ref.shape[1]
    for p in range(width // LANES):
        sl = slice(p * LANES, (p + 1) * LANES)
        k_ref[:, sl] = _head_rms(kv[:, sl], kn_ref[...])
    v_ref[...] = kv[:, width:]


def _mem_kv(mem, g, w, k_norm):
    m, d = mem.shape
    width = w.shape[1] // 2
    kn = jnp.tile(k_norm, LANES // HEAD).reshape(1, LANES)
    return pl.pallas_call(
        _mem_kv_kernel,
        grid=(1,),
        in_specs=[_const_spec((m, d)), _const_spec((1, d)), _const_spec((d, 2 * width)), _const_spec((1, LANES))],
        out_specs=[pl.BlockSpec((m, width), lambda i: (0, 0)), pl.BlockSpec((m, width), lambda i: (0, 0))],
        out_shape=[jax.ShapeDtypeStruct((m, width), F32)] * 2,
        compiler_params=_cparams("arbitrary"),
        name="mem_kv",
    )(mem, g.reshape(1, d), w.astype(BF16), kn)


def _mem_attn_kernel(q_ref, qn_ref, k_ref, v_ref, o_ref):
    lane = lax.broadcasted_iota(jnp.int32, (1, LANES), 1)
    width = q_ref.shape[1]
    for p in range(width // LANES):
        sl = slice(p * LANES, (p + 1) * LANES)
        q = _head_rms(q_ref[:, sl], qn_ref[...]) * (HEAD ** -0.5)
        k = k_ref[:, sl].astype(BF16)
        v = v_ref[:, sl].astype(BF16)
        v_ones = jnp.concatenate([v, jnp.ones_like(v)], axis=1)
        out = None
        for h in range(LANES // HEAD):
            in_head = (lane // HEAD) == h
            qh = jnp.where(in_head, q, 0.0).astype(BF16)
            s = _dot_nt(qh, k)
            e = jnp.exp(s - jnp.max(s, axis=-1, keepdims=True))
            num_den = _dot(e.astype(BF16), v_ones)
            oh = num_den[:, :LANES] / num_den[:, LANES:]
            out = oh if out is None else jnp.where(in_head, oh, out)
        o_ref[:, sl] = out


def _mem_attn(proj, col_block, q_norm, k, v, *, tm=512):
    s = proj.shape[0]
    m, width = k.shape
    tm = min(tm, s)
    qn = jnp.tile(q_norm, LANES // HEAD).reshape(1, LANES)
    return pl.pallas_call(
        _mem_attn_kernel,
        grid=(s // tm,),
        in_specs=[pl.BlockSpec((tm, width), lambda i: (i, col_block)),
                  _const_spec((1, LANES)), _const_spec((m, width)), _const_spec((m, width))],
        out_specs=pl.BlockSpec((tm, width), lambda i: (i, 0)),
        out_shape=jax.ShapeDtypeStruct((s, width), F32),
        compiler_params=_cparams("parallel"),
        name="mem_attn",
    )(proj, qn, k, v)


def _stack_heads(z):
    lane = lax.broadcasted_iota(jnp.int32, z.shape, 2)
    zero = jnp.zeros_like(z)
    return jnp.concatenate([jnp.where(lane < HEAD, z, zero), jnp.where(lane >= HEAD, z, zero)], axis=1)


def _bdot(a, b):
    return lax.dot_general(a, b, (((2,), (1,)), ((0,), (0,))), preferred_element_type=F32)


def _bdot_nt(a, b):
    return lax.dot_general(a, b, (((2,), (2,)), ((0,), (0,))), preferred_element_type=F32)


def _bdot_tn(a, b):
    return lax.dot_general(a, b, (((1,), (1,)), ((0,), (0,))), preferred_element_type=F32)


def _rwkv_kernel(proj_ref, mu_ref, wlh_ref, wll_ref, g2_ref, w0_ref, a0_ref, kk_ref, ka_ref, rk_ref, lnw_ref, lnb_ref,
                 o_ref, last_ref, h_ref, r_s, k_s, v_s, a_s, lw_s, cum_s, g_s):
    c = CHUNK
    rows = proj_ref.shape[0]
    n_sub = rows // c
    n_pairs = h_ref.shape[0]
    width = n_pairs * LANES

    @pl.when(pl.program_id(0) == 0)
    def _():
        last_ref[...] = jnp.zeros_like(last_ref)
        h_ref[...] = jnp.zeros_like(h_ref)

    u = proj_ref[...]
    row = lax.broadcasted_iota(jnp.int32, u.shape, 0)
    prev = jnp.where(row == 0, last_ref[...], pltpu.roll(u, 1, 0))
    last_ref[...] = u[rows - 1:rows, :]
    u = u + (prev - u) * mu_ref[...]

    lora_in = u[:, 3 * width:3 * width + LANES]
    lane = lax.broadcasted_iota(jnp.int32, lora_in.shape, 1)
    lora_in = jnp.where(lane < HEAD, jnp.tanh(lora_in), lora_in)
    x_hi, x_lo = _split_bf16(lora_in)
    lo = _dot(x_hi, wlh_ref[...]) + (_dot(x_lo, wlh_ref[...]) + _dot(x_hi, wll_ref[...]))
    z = -(w0_ref[...] + lo[:, :width])
    softplus = jnp.maximum(z, 0.0) + jnp.log(1.0 + jnp.exp(-jnp.abs(z)))
    lw = -jnp.exp(-softplus - 0.5)
    a = jax.nn.sigmoid(a0_ref[...] + lo[:, width:])
    g = _dot(_b16(jax.nn.sigmoid(u[:, 3 * width + LANES:])), g2_ref[...])
    ti = lax.broadcasted_iota(jnp.int32, (rows, rows), 0)
    si = lax.broadcasted_iota(jnp.int32, (rows, rows), 1)
    tri = _b16(jnp.where((si <= ti) & (si // c == ti // c), F32(1.0), F32(0.0)))
    lw_hi, lw_lo = _split_bf16(lw)
    cum = _dot(tri, lw_hi) + _dot(tri, lw_lo)

    for sub in range(n_sub):
        rs = slice(sub * c, (sub + 1) * c)
        for p in range(n_pairs):
            b_ = sub * n_pairs + p
            sl = slice(p * LANES, (p + 1) * LANES)
            r_s[b_] = u[rs, sl]
            k_s[b_] = u[rs, width + p * LANES:width + (p + 1) * LANES]
            v_s[b_] = u[rs, 2 * width + p * LANES:2 * width + (p + 1) * LANES]
            a_s[b_] = a[rs, sl]
            lw_s[b_] = lw[rs, sl]
            cum_s[b_] = cum[rs, sl]
            g_s[b_] = g[rs, sl]

    np_ = n_sub * n_pairs

    def per_pair(ref):
        return jnp.concatenate([ref[...]] * n_sub, axis=0)
    hsum = _b16(_head_sum_matrix())

    def head_sum(z):
        return _dot(_b16(z).reshape(np_ * c, LANES), hsum).reshape(np_, c, LANES)

    col = lax.broadcasted_iota(jnp.int32, (np_, c, LANES), 2) % c
    trow = lax.broadcasted_iota(jnp.int32, (np_, c, LANES), 1)
    strict = col < trow
    incl = col <= trow
    bi = lax.broadcasted_iota(jnp.int32, (np_, LANES, LANES), 1)
    bj = lax.broadcasted_iota(jnp.int32, (np_, LANES, LANES), 2)
    eye = bi == bj
    same_head = (bi // HEAD) == (bj // HEAD)

    r, k, v, a_, lw_, cum_ = r_s[...], k_s[...], v_s[...], a_s[...], lw_s[...], cum_s[...]
    kk = k * per_pair(kk_ref)
    kk = kk / jnp.maximum(jnp.sqrt(head_sum(kk * kk)), 1e-12)
    kmod = k * (1.0 + (a_ - 1.0) * per_pair(ka_ref))
    b = kk * a_
    bonus = head_sum(r * kmod * per_pair(rk_ref)) * v

    tot = cum_[:, c - 1:c, :]
    at = -kk * jnp.exp(cum_ - lw_)
    rt = r * jnp.exp(cum_)
    einv = jnp.exp(-cum_)
    eend = jnp.exp(tot - cum_)
    bh, kh = _b16(b * einv), _b16(kmod * einv)
    bt, kt = _b16(b * eend), _b16(kmod * eend)
    at_b = _b16(at)

    aa = _bdot_nt(jnp.concatenate([at_b, _b16(rt)], axis=1),
                  jnp.concatenate([_stack_heads(bh), _stack_heads(kh)], axis=1))
    a_ab = jnp.where(strict, aa[:, :c, :LANES], 0.0)
    a_ak = _b16(jnp.where(strict, aa[:, :c, LANES:], 0.0))
    a_rb = _b16(jnp.where(incl, aa[:, c:, :LANES], 0.0))
    a_rk = _b16(jnp.where(incl, aa[:, c:, LANES:], 0.0))

    t_side = jnp.where(col == trow, 1.0, 0.0) + jnp.where((trow == col + 1) & (trow % 2 == 1), a_ab, 0.0)
    size = 2
    while size < c:
        e_mask = ((trow // (2 * size)) == (col // (2 * size))) & (trow % (2 * size) >= size) & (col % (2 * size) < size)
        t_b = _b16(t_side)
        t_blk = _stack_heads(t_b)
        te = _bdot(t_b, _stack_heads(_b16(jnp.where(e_mask, a_ab, 0.0))))
        t_side = t_side + _bdot(_b16(te), t_blk)
        size *= 2
    t_pair = _b16(t_side)

    v_b = _b16(v)
    v_st = _stack_heads(v_b)
    g0 = _bdot(a_ak, v_st)
    uw = _bdot(t_pair, jnp.concatenate([_stack_heads(_b16(g0)), _stack_heads(at_b)], axis=2))
    u0, w = _b16(uw[:, :, :LANES]), _b16(uw[:, :, LANES:])
    rw = rt + _bdot(a_rb, _stack_heads(w))
    y0 = _bdot(a_rb, _stack_heads(u0)) + _bdot(a_rk, v_st)
    m1 = jnp.where(eye, jnp.exp(tot), 0.0) + jnp.where(same_head, _bdot_tn(bt, w), 0.0)
    m0 = jnp.where(same_head, _bdot_tn(bt, u0) + _bdot_tn(kt, v_b), 0.0)

    rw_b, m1_b = _b16(rw), _b16(m1)
    h = h_ref[...]
    ys = []
    for sub in range(n_sub):
        sl = slice(sub * n_pairs, (sub + 1) * n_pairs)
        h_b = _b16(h)
        ys.append(_bdot(rw_b[sl], h_b) + y0[sl])
        h = _bdot(m1_b[sl], h_b) + m0[sl]
    h_ref[...] = h
    y = jnp.concatenate(ys, axis=0)

    mean = head_sum(y) * (1.0 / HEAD)
    yc = y - mean
    var = head_sum(yc * yc) * (1.0 / HEAD)
    yn = yc * lax.rsqrt(var + GN_EPS) * per_pair(lnw_ref) + per_pair(lnb_ref)
    out = (yn + bonus) * g_s[...]
    for sub in range(n_sub):
        for p in range(n_pairs):
            o_ref[sub * c:(sub + 1) * c, p * LANES:(p + 1) * LANES] = out[sub * n_pairs + p]


def _rwkv(proj, mu, w0, w2, a0, a2, g2, k_k, k_a, r_k, ln_w, ln_b):
    s = proj.shape[0]
    width = w0.shape[0]
    n_pairs = width // LANES
    shift_w = mu.shape[0]
    dl, al = w2.shape[0], a2.shape[0]
    assert dl + al == LANES and g2.shape[0] == LANES and shift_w == 3 * width + 2 * LANES
    wl = jnp.zeros((LANES, 2 * width), F32).at[:dl, :width].set(w2).at[dl:, width:].set(a2)
    wl_hi, wl_lo = _split_bf16(wl)

    def per_pair(t):
        return t.reshape(n_pairs, 1, LANES)

    c = CHUNK
    rows = c * RWKV_CHUNKS_PER_STEP
    pair_spec = _const_spec((n_pairs, 1, LANES))
    return pl.pallas_call(
        _rwkv_kernel,
        grid=(s // rows,),
        in_specs=[pl.BlockSpec((rows, shift_w), lambda i: (i, 0)),
                  _const_spec((1, shift_w)), _const_spec((LANES, 2 * width)), _const_spec((LANES, 2 * width)),
                  _const_spec((LANES, width)),
                  _const_spec((1, width)), _const_spec((1, width)),
                  pair_spec, pair_spec, pair_spec, pair_spec, pair_spec],
        out_specs=pl.BlockSpec((rows, width), lambda i: (i, 0)),
        out_shape=jax.ShapeDtypeStruct((s, width), F32),
        scratch_shapes=[pltpu.VMEM((1, shift_w), F32), pltpu.VMEM((n_pairs, LANES, LANES), F32)]
                       + [pltpu.VMEM((RWKV_CHUNKS_PER_STEP * n_pairs, c, LANES), F32)] * 7,
        compiler_params=_cparams("arbitrary"),
        name="rwkv7",
    )(proj, mu.reshape(1, shift_w), wl_hi, wl_lo, g2.astype(BF16), w0.reshape(1, width), a0.reshape(1, width),
      per_pair(k_k), per_pair(k_a), per_pair(r_k.reshape(-1)), per_pair(ln_w), per_pair(ln_b))


def _shared_kv_kernel(x_ref, g_ref, wk_ref, wvt_ref, kn_ref, k_ref, vt_ref, km_ref):
    h = _rms(x_ref[...], g_ref[...]).astype(BF16)
    k = _dot(h, wk_ref[...])
    n_pairs = k_ref.shape[0]
    for p in range(n_pairs):
        kp = _head_rms(k[:, p * LANES:(p + 1) * LANES], kn_ref[...])
        k_ref[p, 0] = kp.astype(BF16)
        km_ref[0, :, p * LANES:(p + 1) * LANES] = jnp.mean(kp, axis=0, keepdims=True)
    vt = _dot_nt(wvt_ref[...], h)
    vt_ref[:, 0] = vt.reshape(n_pairs, LANES, vt.shape[1]).astype(BF16)


def _shared_kv(x, g, w_kv, k_norm):
    s, d = x.shape
    width = w_kv.shape[1] // 2
    n_pairs = width // LANES
    blk = MOBA_BLOCK
    nb = s // blk
    w = w_kv.astype(BF16)
    kn = jnp.tile(k_norm, LANES // HEAD).reshape(1, LANES)
    k, vt, km = pl.pallas_call(
        _shared_kv_kernel,
        grid=(nb,),
        in_specs=[pl.BlockSpec((blk, d), lambda i: (i, 0)), _const_spec((1, d)),
                  _const_spec((d, width)), _const_spec((width, d)), _const_spec((1, LANES))],
        out_specs=[pl.BlockSpec((n_pairs, 1, blk, LANES), lambda i: (0, i, 0, 0)),
                   pl.BlockSpec((n_pairs, 1, LANES, blk), lambda i: (0, i, 0, 0)),
                   pl.BlockSpec((1, 1, width), lambda i: (i, 0, 0))],
        out_shape=[jax.ShapeDtypeStruct((n_pairs, nb, blk, LANES), BF16),
                   jax.ShapeDtypeStruct((n_pairs, nb, LANES, blk), BF16),
                   jax.ShapeDtypeStruct((nb, 1, width), F32)],
        compiler_params=_cparams("parallel"),
        name="shared_kv",
    )(x, g.reshape(1, d), w, w_kv[:, width:].T.astype(BF16), kn)
    k_bound = (HEAD ** 0.5) * jnp.max(jnp.abs(k_norm)).reshape(1)
    return k, vt, km.reshape(nb, width), k_bound


def _moba_kernel(slopes_ref, depth_ref, kbound_ref, q_ref, qn_ref, km_ref, k_ref, vt_ref, o_ref,
                 sel_ref, fac_ref, ownb_ref, base_ref, sa_ref, sb_ref):
    p = pl.program_id(0)
    i = pl.program_id(1)
    blk = MOBA_BLOCK
    nb = km_ref.shape[0]
    col = lax.broadcasted_iota(jnp.int32, (1, 2 * blk), 1)
    slope = jnp.where(col < blk, slopes_ref[2 * p], slopes_ref[2 * p + 1]) * LOG2E
    pad = slope * (blk - 1)

    @pl.when(i == 0)
    def _():
        key_pos = lax.broadcasted_iota(jnp.int32, (blk, 2 * blk), 0)
        qry_pos = lax.broadcasted_iota(jnp.int32, (blk, 2 * blk), 1) % blk
        dist = (qry_pos - key_pos).astype(F32)
        ownb_ref[...] = jnp.where(dist >= 0.0, -slope * dist, NEG_INF)
        base_ref[...] = -slope * dist - pad

    q = _head_rms(q_ref[...], qn_ref[...]) * (HEAD ** -0.5)
    lane = lax.broadcasted_iota(jnp.int32, (1, LANES), 1)
    q2 = jnp.concatenate([jnp.where(lane < HEAD, q, 0.0), jnp.where(lane >= HEAD, q, 0.0)], axis=0)
    qb = (q2 * LOG2E).astype(BF16)

    start = jnp.maximum(i - depth_ref[p] + 1, 0)
    n_iter = (i - start + 3) // 4

    def block_of(g, j):
        return jnp.minimum(start + 2 * g + j, nb - 1)

    def fast_scores(g, s_ref):
        for j in range(2):
            s_ref[j] = _dot_nt(k_ref[block_of(g, j)], qb)

    fast_scores(0, sa_ref)

    n_idx = lax.broadcasted_iota(jnp.int32, (nb, 2 * blk), 0)
    q_hi, q_lo = _split_bf16(q2)
    km_hi, km_lo = _split_bf16(km_ref[...])
    gate = _dot_nt(km_hi, q_hi) + (_dot_nt(km_lo, q_hi) + _dot_nt(km_hi, q_lo))
    gate = jnp.where(n_idx < i, gate, NEG_INF)
    sel_bias = jnp.full(gate.shape, NEG_INF, F32)
    for _ in range(MOBA_TOPK):
        mx = jnp.max(gate, axis=0, keepdims=True)
        first = jnp.min(jnp.where(gate == mx, n_idx, nb), axis=0, keepdims=True)
        pick = (n_idx == first) & (mx > NEG_INF)
        sel_bias = jnp.where(pick, 0.0, sel_bias)
        gate = jnp.where(pick, NEG_INF, gate)
    sel_ref[...] = sel_bias - slope * ((i - n_idx) * blk).astype(F32)

    fac_ref[...] = jnp.exp2(sel_ref[...] + pad)

    def weighted_values(n, e):
        eb = e.astype(BF16)
        return jnp.concatenate([_dot(vt_ref[n, :HEAD, :], eb[:, :blk]), _dot(vt_ref[n, HEAD:, :], eb[:, blk:])],
                               axis=1)

    e = jnp.exp2(_dot_nt(k_ref[i], qb) + ownb_ref[...])
    l_own = jnp.sum(e, axis=0, keepdims=True)
    acc_own = weighted_values(i, e)

    def fast_consume(g, s_ref, l, acc):
        for j in range(2):
            n = block_of(g, j)
            e = jnp.exp2(s_ref[j] + base_ref[...])
            f = fac_ref[pl.ds(n, 1), :]
            l = l + f * jnp.sum(e, axis=0, keepdims=True)
            acc = acc + f * weighted_values(n, e)
        return l, acc

    def fast_two_groups(u, carry):
        fast_scores(2 * u + 1, sb_ref)
        carry = fast_consume(2 * u, sa_ref, *carry)
        fast_scores(2 * u + 2, sa_ref)
        return fast_consume(2 * u + 1, sb_ref, *carry)

    l, acc = lax.fori_loop(0, n_iter, fast_two_groups, (l_own, acc_own))

    def emit(l, acc):
        out = acc / l
        o_ref[...] = jnp.concatenate([out[:, :blk], out[:, blk:]], axis=0).T

    emit(l, acc)

    q_len = jnp.sqrt(_dot_nt(jnp.ones((8, LANES), BF16), _b16(q2 * q2))[0:1])
    score_bound = q_len * (kbound_ref[0] * (LOG2E * BF16_SLACK))
    redo = jnp.max(jnp.where(score_bound < EXP_HEADROOM, 0.0, 1.0))
    redo = jnp.maximum(redo, jnp.maximum(jnp.max(jnp.where(jnp.abs(l) < jnp.inf, 0.0, 1.0)),
                                         jnp.max(jnp.where(jnp.abs(acc) < jnp.inf, 0.0, 1.0))))

    def any_block_of(g, j):
        return jnp.minimum(2 * g + j, nb - 1)

    def score_group(g, s_ref):
        col_max = []
        for j in range(2):
            n = any_block_of(g, j)
            s = _dot_nt(k_ref[n], qb) + (base_ref[...] + pad)
            s_ref[j] = s
            col_max.append(jnp.max(s, axis=0, keepdims=True) + sel_ref[pl.ds(n, 1), :])
        return col_max

    def consume_group(g, s_ref, col_max, m, l, acc):
        n0, n1 = any_block_of(g, 0), any_block_of(g, 1)
        m_new = jnp.maximum(m, jnp.maximum(col_max[0], col_max[1]))
        alpha = jnp.exp2(m - m_new)
        l = alpha * l
        acc = alpha * acc
        for j, n in enumerate((n0, n1)):
            e = jnp.exp2(s_ref[j] - (m_new - sel_ref[pl.ds(n, 1), :]))
            l = l + jnp.sum(e, axis=0, keepdims=True)
            acc = acc + weighted_values(n, e)
        return m_new, l, acc

    @pl.when(redo > 0.0)
    def _():
        max_a = score_group(0, sa_ref)
        s = _dot_nt(k_ref[i], qb) + ownb_ref[...]
        m_own = jnp.max(s, axis=0, keepdims=True)
        e = jnp.exp2(s - m_own)
        l_own = jnp.sum(e, axis=0, keepdims=True)
        acc_own = weighted_values(i, e)

        def two_groups(u, carry):
            max_a0, max_a1, m, l, acc = carry
            max_b = score_group(2 * u + 1, sb_ref)
            m, l, acc = consume_group(2 * u, sa_ref, (max_a0, max_a1), m, l, acc)
            max_a = score_group(2 * u + 2, sa_ref)
            m, l, acc = consume_group(2 * u + 1, sb_ref, max_b, m, l, acc)
            return max_a[0], max_a[1], m, l, acc

        init = (max_a[0], max_a[1], m_own, l_own, acc_own)
        _, _, _, l, acc = lax.fori_loop(0, (i + 3) // 4, two_groups, init)
        emit(l, acc)


def _moba(proj, q_norm, k, vt, km, k_bound):
    s = proj.shape[0]
    n_pairs, nb, blk, _ = k.shape
    heads = n_pairs * (LANES // HEAD)
    slopes = jnp.exp2(-ALIBI_MAX * jnp.arange(1, heads + 1, dtype=F32) / heads)
    min_slope = [2.0 ** (-ALIBI_MAX * (2 * p + 2) / heads) for p in range(n_pairs)]
    depth = jnp.asarray([math.ceil((DEAD_EXPONENT / (sl * LOG2E) + blk - 1) / blk) for sl in min_slope], jnp.int32)
    qn = jnp.tile(q_norm, LANES // HEAD).reshape(1, LANES)
    smem = pl.BlockSpec(memory_space=pltpu.SMEM)
    return pl.pallas_call(
        _moba_kernel,
        grid=(n_pairs, nb),
        in_specs=[smem, smem, smem,
                  pl.BlockSpec((blk, LANES), lambda p, i: (i, p)),
                  pl.BlockSpec((1, LANES), lambda p, i: (0, 0)),
                  pl.BlockSpec((nb, LANES), lambda p, i: (0, p)),
                  pl.BlockSpec((None, nb, blk, LANES), lambda p, i: (p, 0, 0, 0)),
                  pl.BlockSpec((None, nb, LANES, blk), lambda p, i: (p, 0, 0, 0))],
        out_specs=pl.BlockSpec((blk, LANES), lambda p, i: (i, p)),
        out_shape=jax.ShapeDtypeStruct((s, n_pairs * LANES), F32),
        scratch_shapes=[pltpu.VMEM((nb, 2 * blk), F32), pltpu.VMEM((nb, 2 * blk), F32),
                        pltpu.VMEM((blk, 2 * blk), F32), pltpu.VMEM((blk, 2 * blk), F32),
                        pltpu.VMEM((2, blk, 2 * blk), F32), pltpu.VMEM((2, blk, 2 * blk), F32)],
        compiler_params=_cparams("parallel", "arbitrary"),
        name="moba",
    )(slopes, depth, k_bound, proj, qn, km, k, vt)


def kernel(x, mem, ffn1_norm, ffn1_w1, ffn1_w3, ffn1_w2, mix_norm, w_out, mem_norm, w_mem_kv, mem_q_norm, mem_k_norm, ffn2_norm, ffn2_w1, ffn2_w3, ffn2_w2, rwkv_w_in, rwkv_mu, rwkv_w0, rwkv_w2, rwkv_a0, rwkv_a2, rwkv_g2, rwkv_k_k, rwkv_k_a, rwkv_r_k, rwkv_ln_w, rwkv_ln_b, kv_norm, w_kv, kv_k_norm, moba_w_in, moba_q_norm):
    batch = x.shape[0]
    depth = ffn1_norm.shape[0]
    n_a = rwkv_w_in.shape[0]
    mix_w = rwkv_w0.shape[1]
    mem_w = w_mem_kv.shape[2] // 2
    ffn1_w = _ffn_weights(ffn1_w1, ffn1_w3, ffn1_w2)
    ffn2_w = _ffn_weights(ffn2_w1, ffn2_w3, ffn2_w2)
    w_out_b = w_out.astype(BF16)
    outs = []
    for bi in range(batch):
        xb, memb = x[bi], mem[bi]
        shared = None
        for l in range(depth):
            xb = _ffn(xb, ffn1_norm[l], ffn1_w, l)
            if l < n_a:
                proj = _norm_proj(xb, mix_norm[l], rwkv_w_in[l])
                mix = _rwkv(proj, rwkv_mu[l], rwkv_w0[l], rwkv_w2[l], rwkv_a0[l], rwkv_a2[l], rwkv_g2[l],
                            rwkv_k_k[l], rwkv_k_a[l], rwkv_r_k[l], rwkv_ln_w[l], rwkv_ln_b[l])
                mem_col = rwkv_mu.shape[1] // mem_w
            else:
                j = l - n_a
                proj = _norm_proj(xb, mix_norm[l], moba_w_in[j])
                mix = _moba(proj, moba_q_norm[j], *shared)
                mem_col = mix_w // mem_w
            mk, mv = _mem_kv(memb, mem_norm[l], w_mem_kv[l], mem_k_norm[l])
            mem_out = _mem_attn(proj, mem_col, mem_q_norm[l], mk, mv)
            xb = _ffn(xb, ffn2_norm[l], ffn2_w, l, mixer=(mix, mem_out, w_out_b))
            if l == n_a - 1:
                shared = _shared_kv(xb, kv_norm, w_kv, kv_k_norm)
        outs.append(xb)
    return jnp.stack(outs)
```

```python
import functools
import math

import jax
import jax.numpy as jnp
from jax import lax
from jax.experimental import pallas as pl
from jax.experimental.pallas import tpu as pltpu

F32 = jnp.float32
BF16 = jnp.bfloat16

HEAD = 64
LANES = 128
MXU = 256
CHUNK = 64
RWKV_CHUNKS_PER_STEP = 4
MOBA_BLOCK = 256
MOBA_TOPK = 3
ALIBI_MAX = 8.0
NORM_EPS = 1e-6
GN_EPS = 64e-5
VMEM_LIMIT = 56 * 1024 * 1024
NEG_INF = float("-inf")
LOG2E = 1.4426950408889634
DEAD_EXPONENT = 160.0
EXP_HEADROOM = 80.0
BF16_SLACK = 1.02


def _cparams(*sem):
    return pltpu.CompilerParams(dimension_semantics=sem, vmem_limit_bytes=VMEM_LIMIT)


def _const_spec(shape):
    return pl.BlockSpec(shape, lambda *_: (0,) * len(shape), pipeline_mode=pl.Buffered(1))


def _rms(x, g):
    return x * lax.rsqrt(jnp.mean(x * x, axis=-1, keepdims=True) + NORM_EPS) * g


def _b16(x):
    return x.astype(BF16)


def _split_bf16(x):
    hi = x.astype(BF16)
    return hi, (x - hi.astype(F32)).astype(BF16)


def _dot(a, b):
    return jnp.dot(a, b, preferred_element_type=F32)


def _dot_nt(a, b):
    return lax.dot_general(a, b, (((1,), (1,)), ((), ())), preferred_element_type=F32)


def _head_sum_matrix(scale=1.0):
    i = lax.broadcasted_iota(jnp.int32, (LANES, LANES), 0) // HEAD
    j = lax.broadcasted_iota(jnp.int32, (LANES, LANES), 1) // HEAD
    return jnp.where(i == j, F32(scale), F32(0.0))


def _head_rms(x, g):
    hi, lo = _split_bf16(x * x)
    mean_matrix = _b16(_head_sum_matrix(1.0 / HEAD))
    ms = _dot(hi, mean_matrix) + _dot(lo, mean_matrix)
    return x * lax.rsqrt(ms + NORM_EPS) * g


def _swiglu_half_step(x, g_ref, w1_ref, w3_ref, w2_ref, acc_ref, f_chunk):
    h = _rms(x, g_ref[...]).astype(BF16)
    n_chunks = w1_ref.shape[0] // f_chunk
    for c in range(n_chunks):
        sl = slice(c * f_chunk, (c + 1) * f_chunk)
        a = _dot_nt(h, w1_ref[sl, :])
        b = _dot_nt(h, w3_ref[sl, :])
        act = (a * jax.nn.sigmoid(a) * b).astype(BF16)
        part = _dot(act, w2_ref[sl, :])
        if c == 0:
            acc_ref[...] = part
        else:
            acc_ref[...] += part
    return x + 0.5 * acc_ref[...]


def _ffn_kernel(x_ref, g_ref, w1_ref, w3_ref, w2_ref, o_ref, acc_ref, *, f_chunk):
    o_ref[...] = _swiglu_half_step(x_ref[...], g_ref, w1_ref, w3_ref, w2_ref, acc_ref, f_chunk)


def _mix_ffn_kernel(x_ref, mix_ref, mem_ref, wo_ref, g_ref, w1_ref, w3_ref, w2_ref, o_ref, acc_ref, *, f_chunk):
    na = mix_ref.shape[1]
    x = (x_ref[...] + _dot(mix_ref[...].astype(BF16), wo_ref[:na, :])
         + _dot(mem_ref[...].astype(BF16), wo_ref[na:, :]))
    o_ref[...] = _swiglu_half_step(x, g_ref, w1_ref, w3_ref, w2_ref, acc_ref, f_chunk)


def _cast_pad_kernel(x_ref, o_ref, *, axis):
    x = x_ref[...].astype(BF16)
    n = x.shape[axis]
    if axis == 0:
        o_ref[:n, :] = x
        o_ref[n:, :] = jnp.zeros((o_ref.shape[0] - n, x.shape[1]), BF16)
    else:
        o_ref[:, :n] = x
        o_ref[:, n:] = jnp.zeros((x.shape[0], o_ref.shape[1] - n), BF16)


def _cast_pad(w, axis, pad, *, tile=256):
    layers, a, b = w.shape
    if axis == 2:
        grid, in_blk, out_blk = (layers, a // tile), (None, tile, b), (None, tile, b + pad)
        index = lambda l, i: (l, i, 0)
    else:
        grid, in_blk, out_blk = (layers, b // tile), (None, a, tile), (None, a + pad, tile)
        index = lambda l, i: (l, 0, i)
    out_shape = (layers, a + pad, b) if axis == 1 else (layers, a, b + pad)
    return pl.pallas_call(
        functools.partial(_cast_pad_kernel, axis=axis - 1),
        grid=grid,
        in_specs=[pl.BlockSpec(in_blk, index)],
        out_specs=pl.BlockSpec(out_blk, index),
        out_shape=jax.ShapeDtypeStruct(out_shape, BF16),
        compiler_params=_cparams("parallel", "parallel"),
        name="cast_pad",
    )(w)


def _ffn_weights(w1, w3, w2, f_chunk=MXU):
    f = w1.shape[2]
    pad = -(-f // f_chunk) * f_chunk - f
    return (_cast_pad(jnp.swapaxes(w1, 1, 2), 1, pad), _cast_pad(jnp.swapaxes(w3, 1, 2), 1, pad),
            _cast_pad(w2, 1, pad))


def _layer_spec(shape, layer):
    return pl.BlockSpec((None,) + shape, lambda *_: (layer,) + (0,) * len(shape), pipeline_mode=pl.Buffered(1))


def _ffn(x, g, weights, layer, mixer=None, *, tm=1024, f_chunk=MXU):
    s, d = x.shape
    w1, w3, w2 = weights
    fp = w2.shape[1]
    tm = min(tm, s)
    row_spec = pl.BlockSpec((tm, d), lambda i: (i, 0))
    ffn_specs = [_const_spec((1, d)), _layer_spec((fp, d), layer), _layer_spec((fp, d), layer),
                 _layer_spec((fp, d), layer)]
    ffn_args = (g.reshape(1, d), w1, w3, w2)
    if mixer is None:
        body, specs, args = _ffn_kernel, [row_spec] + ffn_specs, (x,) + ffn_args
    else:
        mix, mem_out, w_out = mixer
        na, nb = mix.shape[1], mem_out.shape[1]
        body = _mix_ffn_kernel
        specs = [row_spec, pl.BlockSpec((tm, na), lambda i: (i, 0)), pl.BlockSpec((tm, nb), lambda i: (i, 0)),
                 _layer_spec((na + nb, d), layer)] + ffn_specs
        args = (x, mix, mem_out, w_out) + ffn_args
    return pl.pallas_call(
        functools.partial(body, f_chunk=f_chunk),
        grid=(s // tm,),
        in_specs=specs,
        out_specs=row_spec,
        out_shape=jax.ShapeDtypeStruct((s, d), F32),
        scratch_shapes=[pltpu.VMEM((tm, d), F32)],
        compiler_params=_cparams("parallel"),
        name="ffn" if mixer is None else "mix_ffn",
    )(*args)


def _norm_proj_kernel(x_ref, g_ref, w_ref, o_ref):
    h = _rms(x_ref[...], g_ref[...]).astype(BF16)
    o_ref[...] = _dot(h, w_ref[...])


def _norm_proj(x, g, w, *, tm=512):
    s, d = x.shape
    n = w.shape[1]
    tm = min(tm, s)
    return pl.pallas_call(
        _norm_proj_kernel,
        grid=(s // tm,),
        in_specs=[pl.BlockSpec((tm, d), lambda i: (i, 0)), _const_spec((1, d)), _const_spec((d, n))],
        out_specs=pl.BlockSpec((tm, n), lambda i: (i, 0)),
        out_shape=jax.ShapeDtypeStruct((s, n), F32),
        compiler_params=_cparams("parallel"),
        name="norm_proj",
    )(x, g.reshape(1, d), w.astype(BF16))


def _mem_kv_kernel(mem_ref, g_ref, w_ref, kn_ref, k_ref, v_ref):
    h = _rms(mem_ref[...], g_ref[...]).astype(BF16)
    kv = _dot(h, w_ref[...])
    width = k_ref.shape[1]
    for p in range(width // LANES):
        sl = slice(p * LANES, (p + 1) * LANES)
        k_ref[:, sl] = _head_rms(kv[:, sl], kn_ref[...])
    v_ref[...] = kv[:, width:]


def _mem_kv(mem, g, w, k_norm):
    m, d = mem.shape
    width = w.shape[1] // 2
    kn = jnp.tile(k_norm, LANES // HEAD).reshape(1, LANES)
    return pl.pallas_call(
        _mem_kv_kernel,
        grid=(1,),
        in_specs=[_const_spec((m, d)), _const_spec((1, d)), _const_spec((d, 2 * width)), _const_spec((1, LANES))],
        out_specs=[pl.BlockSpec((m, width), lambda i: (0, 0)), pl.BlockSpec((m, width), lambda i: (0, 0))],
        out_shape=[jax.ShapeDtypeStruct((m, width), F32)] * 2,
        compiler_params=_cparams("arbitrary"),
        name="mem_kv",
    )(mem, g.reshape(1, d), w.astype(BF16), kn)


def _mem_attn_kernel(q_ref, qn_ref, k_ref, v_ref, o_ref):
    lane = lax.broadcasted_iota(jnp.int32, (1, LANES), 1)
    width = q_ref.shape[1]
    for p in range(width // LANES):
        sl = slice(p * LANES, (p + 1) * LANES)
        q = _head_rms(q_ref[:, sl], qn_ref[...]) * (HEAD ** -0.5)
        k = k_ref[:, sl].astype(BF16)
        v = v_ref[:, sl].astype(BF16)
        v_ones = jnp.concatenate([v, jnp.ones_like(v)], axis=1)
        out = None
        for h in range(LANES // HEAD):
            in_head = (lane // HEAD) == h
            qh = jnp.where(in_head, q, 0.0).astype(BF16)
            s = _dot_nt(qh, k)
            e = jnp.exp(s - jnp.max(s, axis=-1, keepdims=True))
            num_den = _dot(e.astype(BF16), v_ones)
            oh = num_den[:, :LANES] / num_den[:, LANES:]
            out = oh if out is None else jnp.where(in_head, oh, out)
        o_ref[:, sl] = out


def _mem_attn(proj, col_block, q_norm, k, v, *, tm=512):
    s = proj.shape[0]
    m, width = k.shape
    tm = min(tm, s)
    qn = jnp.tile(q_norm, LANES // HEAD).reshape(1, LANES)
    return pl.pallas_call(
        _mem_attn_kernel,
        grid=(s // tm,),
        in_specs=[pl.BlockSpec((tm, width), lambda i: (i, col_block)),
                  _const_spec((1, LANES)), _const_spec((m, width)), _const_spec((m, width))],
        out_specs=pl.BlockSpec((tm, width), lambda i: (i, 0)),
        out_shape=jax.ShapeDtypeStruct((s, width), F32),
        compiler_params=_cparams("parallel"),
        name="mem_attn",
    )(proj, qn, k, v)


def _stack_heads(z):
    lane = lax.broadcasted_iota(jnp.int32, z.shape, 2)
    zero = jnp.zeros_like(z)
    return jnp.concatenate([jnp.where(lane < HEAD, z, zero), jnp.where(lane >= HEAD, z, zero)], axis=1)


def _bdot(a, b):
    return lax.dot_general(a, b, (((2,), (1,)), ((0,), (0,))), preferred_element_type=F32)


def _bdot_nt(a, b):
    return lax.dot_general(a, b, (((2,), (2,)), ((0,), (0,))), preferred_element_type=F32)


def _bdot_tn(a, b):
    return lax.dot_general(a, b, (((1,), (1,)), ((0,), (0,))), preferred_element_type=F32)


def _rwkv_kernel(proj_ref, mu_ref, wlh_ref, wll_ref, g2_ref, w0_ref, a0_ref, kk_ref, ka_ref, rk_ref, lnw_ref, lnb_ref,
                 o_ref, last_ref, h_ref, r_s, k_s, v_s, a_s, lw_s, cum_s, g_s):
    c = CHUNK
    rows = proj_ref.shape[0]
    n_sub = rows // c
    n_pairs = h_ref.shape[0]
    width = n_pairs * LANES

    @pl.when(pl.program_id(0) == 0)
    def _():
        last_ref[...] = jnp.zeros_like(last_ref)
        h_ref[...] = jnp.zeros_like(h_ref)

    u = proj_ref[...]
    row = lax.broadcasted_iota(jnp.int32, u.shape, 0)
    prev = jnp.where(row == 0, last_ref[...], pltpu.roll(u, 1, 0))
    last_ref[...] = u[rows - 1:rows, :]
    u = u + (prev - u) * mu_ref[...]

    lora_in = u[:, 3 * width:3 * width + LANES]
    lane = lax.broadcasted_iota(jnp.int32, lora_in.shape, 1)
    lora_in = jnp.where(lane < HEAD, jnp.tanh(lora_in), lora_in)
    x_hi, x_lo = _split_bf16(lora_in)
    lo = _dot(x_hi, wlh_ref[...]) + (_dot(x_lo, wlh_ref[...]) + _dot(x_hi, wll_ref[...]))
    z = -(w0_ref[...] + lo[:, :width])
    softplus = jnp.maximum(z, 0.0) + jnp.log(1.0 + jnp.exp(-jnp.abs(z)))
    lw = -jnp.exp(-softplus - 0.5)
    a = jax.nn.sigmoid(a0_ref[...] + lo[:, width:])
    g = _dot(_b16(jax.nn.sigmoid(u[:, 3 * width + LANES:])), g2_ref[...])
    ti = lax.broadcasted_iota(jnp.int32, (rows, rows), 0)
    si = lax.broadcasted_iota(jnp.int32, (rows, rows), 1)
    tri = _b16(jnp.where((si <= ti) & (si // c == ti // c), F32(1.0), F32(0.0)))
    lw_hi, lw_lo = _split_bf16(lw)
    cum = _dot(tri, lw_hi) + _dot(tri, lw_lo)

    for sub in range(n_sub):
        rs = slice(sub * c, (sub + 1) * c)
        for p in range(n_pairs):
            b_ = sub * n_pairs + p
            sl = slice(p * LANES, (p + 1) * LANES)
            r_s[b_] = u[rs, sl]
            k_s[b_] = u[rs, width + p * LANES:width + (p + 1) * LANES]
            v_s[b_] = u[rs, 2 * width + p * LANES:2 * width + (p + 1) * LANES]
            a_s[b_] = a[rs, sl]
            lw_s[b_] = lw[rs, sl]
            cum_s[b_] = cum[rs, sl]
            g_s[b_] = g[rs, sl]

    np_ = n_sub * n_pairs

    def per_pair(ref):
        return jnp.concatenate([ref[...]] * n_sub, axis=0)
    hsum = _b16(_head_sum_matrix())

    def head_sum(z):
        return _dot(_b16(z).reshape(np_ * c, LANES), hsum).reshape(np_, c, LANES)

    col = lax.broadcasted_iota(jnp.int32, (np_, c, LANES), 2) % c
    trow = lax.broadcasted_iota(jnp.int32, (np_, c, LANES), 1)
    strict = col < trow
    incl = col <= trow
    bi = lax.broadcasted_iota(jnp.int32, (np_, LANES, LANES), 1)
    bj = lax.broadcasted_iota(jnp.int32, (np_, LANES, LANES), 2)
    eye = bi == bj
    same_head = (bi // HEAD) == (bj // HEAD)

    r, k, v, a_, lw_, cum_ = r_s[...], k_s[...], v_s[...], a_s[...], lw_s[...], cum_s[...]
    kk = k * per_pair(kk_ref)
    kk = kk / jnp.maximum(jnp.sqrt(head_sum(kk * kk)), 1e-12)
    kmod = k * (1.0 + (a_ - 1.0) * per_pair(ka_ref))
    b = kk * a_
    bonus = head_sum(r * kmod * per_pair(rk_ref)) * v

    tot = cum_[:, c - 1:c, :]
    at = -kk * jnp.exp(cum_ - lw_)
    rt = r * jnp.exp(cum_)
    einv = jnp.exp(-cum_)
    eend = jnp.exp(tot - cum_)
    bh, kh = _b16(b * einv), _b16(kmod * einv)
    bt, kt = _b16(b * eend), _b16(kmod * eend)
    at_b = _b16(at)

    aa = _bdot_nt(jnp.concatenate([at_b, _b16(rt)], axis=1),
                  jnp.concatenate([_stack_heads(bh), _stack_heads(kh)], axis=1))
    a_ab = jnp.where(strict, aa[:, :c, :LANES], 0.0)
    a_ak = _b16(jnp.where(strict, aa[:, :c, LANES:], 0.0))
    a_rb = _b16(jnp.where(incl, aa[:, c:, :LANES], 0.0))
    a_rk = _b16(jnp.where(incl, aa[:, c:, LANES:], 0.0))

    t_side = jnp.where(col == trow, 1.0, 0.0) + jnp.where((trow == col + 1) & (trow % 2 == 1), a_ab, 0.0)
    size = 2
    while size < c:
        e_mask = ((trow // (2 * size)) == (col // (2 * size))) & (trow % (2 * size) >= size) & (col % (2 * size) < size)
        t_b = _b16(t_side)
        t_blk = _stack_heads(t_b)
        te = _bdot(t_b, _stack_heads(_b16(jnp.where(e_mask, a_ab, 0.0))))
        t_side = t_side + _bdot(_b16(te), t_blk)
        size *= 2
    t_pair = _b16(t_side)

    v_b = _b16(v)
    v_st = _stack_heads(v_b)
    g0 = _bdot(a_ak, v_st)
    uw = _bdot(t_pair, jnp.concatenate([_stack_heads(_b16(g0)), _stack_heads(at_b)], axis=2))
    u0, w = _b16(uw[:, :, :LANES]), _b16(uw[:, :, LANES:])
    rw = rt + _bdot(a_rb, _stack_heads(w))
    y0 = _bdot(a_rb, _stack_heads(u0)) + _bdot(a_rk, v_st)
    m1 = jnp.where(eye, jnp.exp(tot), 0.0) + jnp.where(same_head, _bdot_tn(bt, w), 0.0)
    m0 = jnp.where(same_head, _bdot_tn(bt, u0) + _bdot_tn(kt, v_b), 0.0)

    rw_b, m1_b = _b16(rw), _b16(m1)
    h = h_ref[...]
    ys = []
    for sub in range(n_sub):
        sl = slice(sub * n_pairs, (sub + 1) * n_pairs)
        h_b = _b16(h)
        ys.append(_bdot(rw_b[sl], h_b) + y0[sl])
        h = _bdot(m1_b[sl], h_b) + m0[sl]
    h_ref[...] = h
    y = jnp.concatenate(ys, axis=0)

    mean = head_sum(y) * (1.0 / HEAD)
    yc = y - mean
    var = head_sum(yc * yc) * (1.0 / HEAD)
    yn = yc * lax.rsqrt(var + GN_EPS) * per_pair(lnw_ref) + per_pair(lnb_ref)
    out = (yn + bonus) * g_s[...]
    for sub in range(n_sub):
        for p in range(n_pairs):
            o_ref[sub * c:(sub + 1) * c, p * LANES:(p + 1) * LANES] = out[sub * n_pairs + p]


def _rwkv(proj, mu, w0, w2, a0, a2, g2, k_k, k_a, r_k, ln_w, ln_b):
    s = proj.shape[0]
    width = w0.shape[0]
    n_pairs = width // LANES
    shift_w = mu.shape[0]
    dl, al = w2.shape[0], a2.shape[0]
    assert dl + al == LANES and g2.shape[0] == LANES and shift_w == 3 * width + 2 * LANES
    wl = jnp.zeros((LANES, 2 * width), F32).at[:dl, :width].set(w2).at[dl:, width:].set(a2)
    wl_hi, wl_lo = _split_bf16(wl)

    def per_pair(t):
        return t.reshape(n_pairs, 1, LANES)

    c = CHUNK
    rows = c * RWKV_CHUNKS_PER_STEP
    pair_spec = _const_spec((n_pairs, 1, LANES))
    return pl.pallas_call(
        _rwkv_kernel,
        grid=(s // rows,),
        in_specs=[pl.BlockSpec((rows, shift_w), lambda i: (i, 0)),
                  _const_spec((1, shift_w)), _const_spec((LANES, 2 * width)), _const_spec((LANES, 2 * width)),
                  _const_spec((LANES, width)),
                  _const_spec((1, width)), _const_spec((1, width)),
                  pair_spec, pair_spec, pair_spec, pair_spec, pair_spec],
        out_specs=pl.BlockSpec((rows, width), lambda i: (i, 0)),
        out_shape=jax.ShapeDtypeStruct((s, width), F32),
        scratch_shapes=[pltpu.VMEM((1, shift_w), F32), pltpu.VMEM((n_pairs, LANES, LANES), F32)]
                       + [pltpu.VMEM((RWKV_CHUNKS_PER_STEP * n_pairs, c, LANES), F32)] * 7,
        compiler_params=_cparams("arbitrary"),
        name="rwkv7",
    )(proj, mu.reshape(1, shift_w), wl_hi, wl_lo, g2.astype(BF16), w0.reshape(1, width), a0.reshape(1, width),
      per_pair(k_k), per_pair(k_a), per_pair(r_k.reshape(-1)), per_pair(ln_w), per_pair(ln_b))


def _shared_kv_kernel(x_ref, g_ref, wk_ref, wvt_ref, kn_ref, k_ref, vt_ref, km_ref):
    h = _rms(x_ref[...], g_ref[...]).astype(BF16)
    k = _dot(h, wk_ref[...])
    n_pairs = k_ref.shape[0]
    for p in range(n_pairs):
        kp = _head_rms(k[:, p * LANES:(p + 1) * LANES], kn_ref[...])
        k_ref[p, 0] = kp.astype(BF16)
        km_ref[0, :, p * LANES:(p + 1) * LANES] = jnp.mean(kp, axis=0, keepdims=True)
    vt = _dot_nt(wvt_ref[...], h)
    vt_ref[:, 0] = vt.reshape(n_pairs, LANES, vt.shape[1]).astype(BF16)


def _shared_kv(x, g, w_kv, k_norm):
    s, d = x.shape
    width = w_kv.shape[1] // 2
    n_pairs = width // LANES
    blk = MOBA_BLOCK
    nb = s // blk
    w = w_kv.astype(BF16)
    kn = jnp.tile(k_norm, LANES // HEAD).reshape(1, LANES)
    k, vt, km = pl.pallas_call(
        _shared_kv_kernel,
        grid=(nb,),
        in_specs=[pl.BlockSpec((blk, d), lambda i: (i, 0)), _const_spec((1, d)),
                  _const_spec((d, width)), _const_spec((width, d)), _const_spec((1, LANES))],
        out_specs=[pl.BlockSpec((n_pairs, 1, blk, LANES), lambda i: (0, i, 0, 0)),
                   pl.BlockSpec((n_pairs, 1, LANES, blk), lambda i: (0, i, 0, 0)),
                   pl.BlockSpec((1, 1, width), lambda i: (i, 0, 0))],
        out_shape=[jax.ShapeDtypeStruct((n_pairs, nb, blk, LANES), BF16),
                   jax.ShapeDtypeStruct((n_pairs, nb, LANES, blk), BF16),
                   jax.ShapeDtypeStruct((nb, 1, width), F32)],
        compiler_params=_cparams("parallel"),
        name="shared_kv",
    )(x, g.reshape(1, d), w, w_kv[:, width:].T.astype(BF16), kn)
    k_bound = (HEAD ** 0.5) * jnp.max(jnp.abs(k_norm)).reshape(1)
    return k, vt, km.reshape(nb, width), k_bound


def _moba_kernel(slopes_ref, depth_ref, kbound_ref, q_ref, qn_ref, km_ref, k_ref, vt_ref, o_ref,
                 sel_ref, fac_ref, ownb_ref, base_ref, sa_ref, sb_ref):
    p = pl.program_id(0)
    i = pl.program_id(1)
    blk = MOBA_BLOCK
    nb = km_ref.shape[0]
    col = lax.broadcasted_iota(jnp.int32, (1, 2 * blk), 1)
    slope = jnp.where(col < blk, slopes_ref[2 * p], slopes_ref[2 * p + 1]) * LOG2E
    pad = slope * (blk - 1)

    @pl.when(i == 0)
    def _():
        key_pos = lax.broadcasted_iota(jnp.int32, (blk, 2 * blk), 0)
        qry_pos = lax.broadcasted_iota(jnp.int32, (blk, 2 * blk), 1) % blk
        dist = (qry_pos - key_pos).astype(F32)
        ownb_ref[...] = jnp.where(dist >= 0.0, -slope * dist, NEG_INF)
        base_ref[...] = -slope * dist - pad

    q = _head_rms(q_ref[...], qn_ref[...]) * (HEAD ** -0.5)
    lane = lax.broadcasted_iota(jnp.int32, (1, LANES), 1)
    q2 = jnp.concatenate([jnp.where(lane < HEAD, q, 0.0), jnp.where(lane >= HEAD, q, 0.0)], axis=0)
    qb = (q2 * LOG2E).astype(BF16)

    start = jnp.maximum(i - depth_ref[p] + 1, 0)
    n_iter = (i - start + 3) // 4

    def block_of(g, j):
        return jnp.minimum(start + 2 * g + j, nb - 1)

    def fast_scores(g, s_ref):
        for j in range(2):
            s_ref[j] = _dot_nt(k_ref[block_of(g, j)], qb)

    fast_scores(0, sa_ref)

    n_idx = lax.broadcasted_iota(jnp.int32, (nb, 2 * blk), 0)
    q_hi, q_lo = _split_bf16(q2)
    km_hi, km_lo = _split_bf16(km_ref[...])
    gate = _dot_nt(km_hi, q_hi) + (_dot_nt(km_lo, q_hi) + _dot_nt(km_hi, q_lo))
    gate = jnp.where(n_idx < i, gate, NEG_INF)
    sel_bias = jnp.full(gate.shape, NEG_INF, F32)
    for _ in range(MOBA_TOPK):
        mx = jnp.max(gate, axis=0, keepdims=True)
        first = jnp.min(jnp.where(gate == mx, n_idx, nb), axis=0, keepdims=True)
        pick = (n_idx == first) & (mx > NEG_INF)
        sel_bias = jnp.where(pick, 0.0, sel_bias)
        gate = jnp.where(pick, NEG_INF, gate)
    sel_ref[...] = sel_bias - slope * ((i - n_idx) * blk).astype(F32)

    fac_ref[...] = jnp.exp2(sel_ref[...] + pad)

    def weighted_values(n, e):
        eb = e.astype(BF16)
        return jnp.concatenate([_dot(vt_ref[n, :HEAD, :], eb[:, :blk]), _dot(vt_ref[n, HEAD:, :], eb[:, blk:])],
                               axis=1)

    e = jnp.exp2(_dot_nt(k_ref[i], qb) + ownb_ref[...])
    l_own = jnp.sum(e, axis=0, keepdims=True)
    acc_own = weighted_values(i, e)

    def fast_consume(g, s_ref, l, acc):
        for j in range(2):
            n = block_of(g, j)
            e = jnp.exp2(s_ref[j] + base_ref[...])
            f = fac_ref[pl.ds(n, 1), :]
            l = l + f * jnp.sum(e, axis=0, keepdims=True)
            acc = acc + f * weighted_values(n, e)
        return l, acc

    def fast_two_groups(u, carry):
        fast_scores(2 * u + 1, sb_ref)
        carry = fast_consume(2 * u, sa_ref, *carry)
        fast_scores(2 * u + 2, sa_ref)
        return fast_consume(2 * u + 1, sb_ref, *carry)

    l, acc = lax.fori_loop(0, n_iter, fast_two_groups, (l_own, acc_own))

    def emit(l, acc):
        out = acc / l
        o_ref[...] = jnp.concatenate([out[:, :blk], out[:, blk:]], axis=0).T

    emit(l, acc)

    q_len = jnp.sqrt(_dot_nt(jnp.ones((8, LANES), BF16), _b16(q2 * q2))[0:1])
    score_bound = q_len * (kbound_ref[0] * (LOG2E * BF16_SLACK))
    redo = jnp.max(jnp.where(score_bound < EXP_HEADROOM, 0.0, 1.0))
    redo = jnp.maximum(redo, jnp.maximum(jnp.max(jnp.where(jnp.abs(l) < jnp.inf, 0.0, 1.0)),
                                         jnp.max(jnp.where(jnp.abs(acc) < jnp.inf, 0.0, 1.0))))

    def any_block_of(g, j):
        return jnp.minimum(2 * g + j, nb - 1)

    def score_group(g, s_ref):
        col_max = []
        for j in range(2):
            n = any_block_of(g, j)
            s = _dot_nt(k_ref[n], qb) + (base_ref[...] + pad)
            s_ref[j] = s
            col_max.append(jnp.max(s, axis=0, keepdims=True) + sel_ref[pl.ds(n, 1), :])
        return col_max

    def consume_group(g, s_ref, col_max, m, l, acc):
        n0, n1 = any_block_of(g, 0), any_block_of(g, 1)
        m_new = jnp.maximum(m, jnp.maximum(col_max[0], col_max[1]))
        alpha = jnp.exp2(m - m_new)
        l = alpha * l
        acc = alpha * acc
        for j, n in enumerate((n0, n1)):
            e = jnp.exp2(s_ref[j] - (m_new - sel_ref[pl.ds(n, 1), :]))
            l = l + jnp.sum(e, axis=0, keepdims=True)
            acc = acc + weighted_values(n, e)
        return m_new, l, acc

    @pl.when(redo > 0.0)
    def _():
        max_a = score_group(0, sa_ref)
        s = _dot_nt(k_ref[i], qb) + ownb_ref[...]
        m_own = jnp.max(s, axis=0, keepdims=True)
        e = jnp.exp2(s - m_own)
        l_own = jnp.sum(e, axis=0, keepdims=True)
        acc_own = weighted_values(i, e)

        def two_groups(u, carry):
            max_a0, max_a1, m, l, acc = carry
            max_b = score_group(2 * u + 1, sb_ref)
            m, l, acc = consume_group(2 * u, sa_ref, (max_a0, max_a1), m, l, acc)
            max_a = score_group(2 * u + 2, sa_ref)
            m, l, acc = consume_group(2 * u + 1, sb_ref, max_b, m, l, acc)
            return max_a[0], max_a[1], m, l, acc

        init = (max_a[0], max_a[1], m_own, l_own, acc_own)
        _, _, _, l, acc = lax.fori_loop(0, (i + 3) // 4, two_groups, init)
        emit(l, acc)


def _moba(proj, q_norm, k, vt, km, k_bound):
    s = proj.shape[0]
    n_pairs, nb, blk, _ = k.shape
    heads = n_pairs * (LANES // HEAD)
    slopes = jnp.exp2(-ALIBI_MAX * jnp.arange(1, heads + 1, dtype=F32) / heads)
    min_slope = [2.0 ** (-ALIBI_MAX * (2 * p + 2) / heads) for p in range(n_pairs)]
    depth = jnp.asarray([math.ceil((DEAD_EXPONENT / (sl * LOG2E) + blk - 1) / blk) for sl in min_slope], jnp.int32)
    qn = jnp.tile(q_norm, LANES // HEAD).reshape(1, LANES)
    smem = pl.BlockSpec(memory_space=pltpu.SMEM)
    return pl.pallas_call(
        _moba_kernel,
        grid=(n_pairs, nb),
        in_specs=[smem, smem, smem,
                  pl.BlockSpec((blk, LANES), lambda p, i: (i, p)),
                  pl.BlockSpec((1, LANES), lambda p, i: (0, 0)),
                  pl.BlockSpec((nb, LANES), lambda p, i: (0, p)),
                  pl.BlockSpec((None, nb, blk, LANES), lambda p, i: (p, 0, 0, 0)),
                  pl.BlockSpec((None, nb, LANES, blk), lambda p, i: (p, 0, 0, 0))],
        out_specs=pl.BlockSpec((blk, LANES), lambda p, i: (i, p)),
        out_shape=jax.ShapeDtypeStruct((s, n_pairs * LANES), F32),
        scratch_shapes=[pltpu.VMEM((nb, 2 * blk), F32), pltpu.VMEM((nb, 2 * blk), F32),
                        pltpu.VMEM((blk, 2 * blk), F32), pltpu.VMEM((blk, 2 * blk), F32),
                        pltpu.VMEM((2, blk, 2 * blk), F32), pltpu.VMEM((2, blk, 2 * blk), F32)],
        compiler_params=_cparams("parallel", "arbitrary"),
        name="moba",
    )(slopes, depth, k_bound, proj, qn, km, k, vt)


def kernel(x, mem, ffn1_norm, ffn1_w1, ffn1_w3, ffn1_w2, mix_norm, w_out, mem_norm, w_mem_kv, mem_q_norm, mem_k_norm, ffn2_norm, ffn2_w1, ffn2_w3, ffn2_w2, rwkv_w_in, rwkv_mu, rwkv_w0, rwkv_w2, rwkv_a0, rwkv_a2, rwkv_g2, rwkv_k_k, rwkv_k_a, rwkv_r_k, rwkv_ln_w, rwkv_ln_b, kv_norm, w_kv, kv_k_norm, moba_w_in, moba_q_norm):
    batch = x.shape[0]
    depth = ffn1_norm.shape[0]
    n_a = rwkv_w_in.shape[0]
    mix_w = rwkv_w0.shape[1]
    mem_w = w_mem_kv.shape[2] // 2
    ffn1_w = _ffn_weights(ffn1_w1, ffn1_w3, ffn1_w2)
    ffn2_w = _ffn_weights(ffn2_w1, ffn2_w3, ffn2_w2)
    w_out_b = w_out.astype(BF16)
    outs = []
    for bi in range(batch):
        xb, memb = x[bi], mem[bi]
        shared = None
        for l in range(depth):
            xb = _ffn(xb, ffn1_norm[l], ffn1_w, l)
            if l < n_a:
                proj = _norm_proj(xb, mix_norm[l], rwkv_w_in[l])
                mix = _rwkv(proj, rwkv_mu[l], rwkv_w0[l], rwkv_w2[l], rwkv_a0[l], rwkv_a2[l], rwkv_g2[l],
                            rwkv_k_k[l], rwkv_k_a[l], rwkv_r_k[l], rwkv_ln_w[l], rwkv_ln_b[l])
                mem_col = rwkv_mu.shape[1] // mem_w
            else:
                j = l - n_a
                proj = _norm_proj(xb, mix_norm[l], moba_w_in[j])
                mix = _moba(proj, moba_q_norm[j], *shared)
                mem_col = mix_w // mem_w
            mk, mv = _mem_kv(memb, mem_norm[l], w_mem_kv[l], mem_k_norm[l])
            mem_out = _mem_attn(proj, mem_col, mem_q_norm[l], mk, mv)
            xb = _ffn(xb, ffn2_norm[l], ffn2_w, l, mixer=(mix, mem_out, w_out_b))
            if l == n_a - 1:
                shared = _shared_kv(xb, kv_norm, w_kv, kv_k_norm)
        outs.append(xb)
    return jnp.stack(outs)
```
